```python
import math
import jax
import jax.numpy as jnp
from jax import lax
import numpy as np

D_MODEL = 1024
BATCH = 4
SEQ = 4096
DEPTH = 1

D_PLE = 256
NSA_HEADS = 8
NSA_GROUPS = 2
NSA_HEAD_DIM = 64
CMP_BLOCK = 32
CMP_STRIDE = 16
CMP_HIDDEN = 256
SLC_BLOCK = 64
SLC_TOPK = 16
SLC_LOCAL = 2
WINDOW = 512
DIFF_HEADS = 4
DIFF_HEAD_DIM = 64
REL_BUCKETS = 32
REL_MAX_EXACT = 16
REL_MAX_DIST = 128
N_BIAS_HEADS = NSA_HEADS + DIFF_HEADS
D_FF = 2816
CONV_WIDTH = 3
Q_BLOCK = 128
ALPHA = (2.0 * DEPTH) ** 0.25
BETA = (8.0 * DEPTH) ** -0.25
LN_EPS = 1e-5
NEG_INF = -1e30
BIG = 1e30

NSA_Q_W = NSA_HEADS * NSA_HEAD_DIM
NSA_KV_W = NSA_GROUPS * NSA_HEAD_DIM
NSA_GATE_W = NSA_HEADS * 3
DIFF_QK_W = DIFF_HEADS * 2 * DIFF_HEAD_DIM
DIFF_V_W = DIFF_HEADS * 2 * DIFF_HEAD_DIM
IN_SIZES = (NSA_Q_W, NSA_KV_W, NSA_KV_W, NSA_KV_W, NSA_KV_W, NSA_KV_W, NSA_KV_W, NSA_GATE_W, DIFF_QK_W, DIFF_QK_W, DIFF_V_W, D_MODEL, D_MODEL)
IN_VALUE_PARTS = (2, 4, 6, 10)
D_IN = sum(IN_SIZES)

kernel_name = 'hybrid_nsa_diffattn_convffn_deepnorm'


def layer_norm(x, g, b):
    xf = x.astype(jnp.float32)
    mu = jnp.mean(xf, axis=-1, keepdims=True)
    xc = xf - mu
    var = jnp.mean(xc * xc, axis=-1, keepdims=True)
    return (xc * lax.rsqrt(var + LN_EPS) * g + b).astype(x.dtype)


def rms_norm(x, g):
    xf = x.astype(jnp.float32)
    return (xf * lax.rsqrt(jnp.mean(xf * xf, axis=-1, keepdims=True) + LN_EPS) * g).astype(x.dtype)


def rel_bucket(dist):
    n = jnp.maximum(dist, 0)
    large = REL_MAX_EXACT + (jnp.log(jnp.maximum(n, 1).astype(jnp.float32) / REL_MAX_EXACT)
                             / math.log(REL_MAX_DIST / REL_MAX_EXACT)
                             * (REL_BUCKETS - REL_MAX_EXACT)).astype(jnp.int32)
    large = jnp.minimum(large, REL_BUCKETS - 1)
    return jnp.where(n < REL_MAX_EXACT, n, large)


def compress_kv(kv, pe, w1, w2):
    B, S, G, Dh = kv.shape
    n_cmp = (S - CMP_BLOCK) // CMP_STRIDE + 1
    idx = np.arange(n_cmp)[:, None] * CMP_STRIDE + np.arange(CMP_BLOCK)[None, :]
    blocks = kv[:, idx] + pe[None, None, :, None, :]
    blocks = blocks.transpose(0, 1, 3, 2, 4).reshape(B, n_cmp, G, CMP_BLOCK * Dh)
    return jax.nn.gelu(blocks @ w1) @ w2


def slc_from_cmp(n_cmp, n_slc):
    ratio = SLC_BLOCK // CMP_STRIDE
    span = CMP_BLOCK // CMP_STRIDE
    j, m, n = np.meshgrid(np.arange(n_slc), np.arange(ratio), np.arange(span), indexing='ij')
    i = ratio * j + m - n
    ok = (i >= 0) & (i < n_cmp)
    mat = np.zeros((n_cmp, n_slc), np.float32)
    np.add.at(mat, (i[ok], j[ok]), 1.0)
    return jnp.asarray(mat)


def nsa_attention(q, k_cmp, v_cmp, k_slc, v_slc, k_win, v_win, gates,
                  pe_k, w1_k, w2_k, pe_v, w1_v, w2_v, table):
    B, S = q.shape[:2]
    G, R, Dh = NSA_GROUPS, NSA_HEADS // NSA_GROUPS, NSA_HEAD_DIM
    scale = Dh ** -0.5
    qg = q.reshape(B, S, G, R, Dh)
    pos = jnp.arange(S)
    tab = table[:, :NSA_HEADS].reshape(REL_BUCKETS, G, R)

    kc = compress_kv(k_cmp, pe_k, w1_k, w2_k)
    vc = compress_kv(v_cmp, pe_v, w1_v, w2_v)
    n_cmp = kc.shape[1]
    blk_end = jnp.arange(n_cmp) * CMP_STRIDE + CMP_BLOCK - 1
    dist_c = pos[:, None] - blk_end[None, :]
    valid_c = dist_c >= 0
    bias_c = jnp.transpose(tab[rel_bucket(dist_c)], (2, 3, 0, 1))
    logit_c = jnp.einsum('bsgrd,bngd->bgrsn', qg, kc).astype(jnp.float32) * scale + bias_c
    logit_c = jnp.where(valid_c, logit_c, NEG_INF)
    p_cmp = jnp.where(valid_c, jax.nn.softmax(logit_c, axis=-1), 0.0)
    o_cmp = jnp.einsum('bgrsn,bngd->bsgrd', p_cmp.astype(vc.dtype), vc)

    n_slc = S // SLC_BLOCK
    p_slc = jnp.einsum('bgrsn,nj->bgsj', p_cmp, slc_from_cmp(n_cmp, n_slc))
    j = jnp.arange(n_slc)[None, :]
    cur = (pos // SLC_BLOCK)[:, None]
    blk_valid = j <= cur
    forced = (j == 0) | ((cur - j >= 0) & (cur - j < SLC_LOCAL))
    score = jnp.where(forced, BIG, jnp.where(blk_valid, p_slc, NEG_INF))
    k_top = min(SLC_TOPK, n_slc)
    _, sel = lax.top_k(score, k_top)

    ks_blocks = k_slc.reshape(B, n_slc, SLC_BLOCK, G, Dh).transpose(0, 3, 1, 2, 4)
    vs_blocks = v_slc.reshape(B, n_slc, SLC_BLOCK, G, Dh).transpose(0, 3, 1, 2, 4)
    pad = ((0, 0), (WINDOW, 0), (0, 0), (0, 0))
    kw_pad = jnp.pad(k_win, pad)
    vw_pad = jnp.pad(v_win, pad)

    nqb = S // Q_BLOCK
    q_blocks = qg.reshape(B, nqb, Q_BLOCK, G, R, Dh).swapaxes(0, 1)
    sel_blocks = sel.reshape(B, G, nqb, Q_BLOCK, k_top).transpose(2, 0, 1, 3, 4)
    b_ix = jnp.arange(B)[:, None, None, None]
    g_ix = jnp.arange(G)[None, :, None, None]
    g_ix5 = jnp.arange(G)[None, :, None, None, None]
    blk_off = jnp.arange(SLC_BLOCK)
    win_off = jnp.arange(WINDOW + Q_BLOCK) - WINDOW

    def block_fn(args):
        qb, selb, ib = args
        q0 = ib * Q_BLOCK
        t = q0 + jnp.arange(Q_BLOCK)
        ks = ks_blocks[b_ix, g_ix, selb]
        vs = vs_blocks[b_ix, g_ix, selb]
        kpos = selb[..., None] * SLC_BLOCK + blk_off
        d_s = t[None, None, :, None, None] - kpos
        b_s = jnp.moveaxis(tab[rel_bucket(d_s), g_ix5], -1, 2)
        s = jnp.einsum('bqgrd,bgqkld->bgrqkl', qb, ks).astype(jnp.float32) * scale + b_s
        s = jnp.where((d_s >= 0)[:, :, None], s, NEG_INF)
        ps = jax.nn.softmax(s.reshape(B, G, R, Q_BLOCK, k_top * SLC_BLOCK), axis=-1)
        ps = ps.reshape(B, G, R, Q_BLOCK, k_top, SLC_BLOCK)
        o_s = jnp.einsum('bgrqkl,bgqkld->bqgrd', ps.astype(vs.dtype), vs)
        kw = lax.dynamic_slice_in_dim(kw_pad, q0, WINDOW + Q_BLOCK, axis=1)
        vw = lax.dynamic_slice_in_dim(vw_pad, q0, WINDOW + Q_BLOCK, axis=1)
        spos = q0 + win_off
        d_w = t[:, None] - spos[None, :]
        valid_w = (d_w >= 0) & (d_w < WINDOW) & (spos[None, :] >= 0)
        b_w = jnp.transpose(tab[rel_bucket(d_w)], (2, 3, 0, 1))
        sw = jnp.einsum('bqgrd,bsgd->bgrqs', qb, kw).astype(jnp.float32) * scale + b_w
        sw = jnp.where(valid_w, sw, NEG_INF)
        pw = jax.nn.softmax(sw, axis=-1)
        o_w = jnp.einsum('bgrqs,bsgd->bqgrd', pw.astype(vw.dtype), vw)
        return o_s, o_w

    o_slc, o_win = lax.map(block_fn, (q_blocks, sel_blocks, jnp.arange(nqb)))
    o_slc = o_slc.swapaxes(0, 1).reshape(B, S, G, R, Dh)
    o_win = o_win.swapaxes(0, 1).reshape(B, S, G, R, Dh)
    g = jax.nn.sigmoid(gates).reshape(B, S, G, R, 3)
    out = g[..., 0:1] * o_cmp + g[..., 1:2] * o_slc + g[..., 2:3] * o_win
    return out.reshape(B, S, NSA_Q_W)


def diff_attention(q, k, v, lq1, lk1, lq2, lk2, subln_g, table, lambda_init):
    B, S = q.shape[:2]
    Hd, d = DIFF_HEADS, DIFF_HEAD_DIM
    scale = d ** -0.5
    q = q.reshape(B, S, Hd, 2, d)
    k = k.reshape(B, S, Hd, 2, d)
    v = v.reshape(B, S, Hd, 2 * d)
    lam = (jnp.exp(jnp.sum(lq1 * lk1).astype(jnp.float32))
           - jnp.exp(jnp.sum(lq2 * lk2).astype(jnp.float32)) + lambda_init)
    tab = table[:, NSA_HEADS:]
    kpos = jnp.arange(S)
    nqb = S // Q_BLOCK
    q_blocks = q.reshape(B, nqb, Q_BLOCK, Hd, 2, d).swapaxes(0, 1)

    def block_fn(args):
        qb, ib = args
        t = ib * Q_BLOCK + jnp.arange(Q_BLOCK)
        dist = t[:, None] - kpos[None, :]
        bias = jnp.transpose(tab[rel_bucket(dist)], (2, 0, 1))[:, None]
        s = jnp.einsum('bqhcd,bshcd->bhcqs', qb, k).astype(jnp.float32) * scale + bias
        s = jnp.where(dist >= 0, s, NEG_INF)
        a = jax.nn.softmax(s, axis=-1)
        attn = a[:, :, 0] - lam * a[:, :, 1]
        return jnp.einsum('bhqs,bshe->bqhe', attn.astype(v.dtype), v)

    o = lax.map(block_fn, (q_blocks, jnp.arange(nqb)))
    o = o.swapaxes(0, 1).reshape(B, S, Hd, 2 * d)
    o = rms_norm(o, subln_g) * (1.0 - lambda_init)
    return o.reshape(B, S, DIFF_V_W)


def causal_dwconv(h, w, b):
    S = h.shape[1]
    hp = jnp.pad(h, ((0, 0), (CONV_WIDTH - 1, 0), (0, 0)))
    out = b
    for kk in range(CONV_WIDTH):
        out = out + w[kk] * hp[:, kk:kk + S]
    return out


def setup_inputs(seed: int = 0) -> dict:
    key = jax.random.key(seed)
    ks = jax.random.split(key, 32)

    def nrm(k, shape, scale):
        return jax.random.normal(k, shape, jnp.float32) * scale

    col_scale = np.concatenate([np.full((s,), BETA if i in IN_VALUE_PARTS else 1.0, np.float32)
                                for i, s in enumerate(IN_SIZES)])
    cmp_in = CMP_BLOCK * NSA_HEAD_DIM
    return {
        'x': nrm(ks[0], (BATCH, SEQ, D_MODEL), 1.0),
        'p': nrm(ks[1], (DEPTH, BATCH, SEQ, D_PLE), 1.0),
        'w_in': nrm(ks[2], (DEPTH, D_MODEL, D_IN), D_MODEL ** -0.5) * jnp.asarray(col_scale),
        'nsa_cmp_pe_k': nrm(ks[3], (DEPTH, CMP_BLOCK, NSA_HEAD_DIM), 0.1),
        'nsa_cmp_w1_k': nrm(ks[4], (DEPTH, cmp_in, CMP_HIDDEN), cmp_in ** -0.5),
        'nsa_cmp_w2_k': nrm(ks[5], (DEPTH, CMP_HIDDEN, NSA_HEAD_DIM), CMP_HIDDEN ** -0.5),
        'nsa_cmp_pe_v': nrm(ks[6], (DEPTH, CMP_BLOCK, NSA_HEAD_DIM), 0.1),
        'nsa_cmp_w1_v': nrm(ks[7], (DEPTH, cmp_in, CMP_HIDDEN), cmp_in ** -0.5),
        'nsa_cmp_w2_v': nrm(ks[8], (DEPTH, CMP_HIDDEN, NSA_HEAD_DIM), CMP_HIDDEN ** -0.5),
        'diff_lambda_q1': nrm(ks[9], (DEPTH, DIFF_HEAD_DIM), 0.1),
        'diff_lambda_k1': nrm(ks[10], (DEPTH, DIFF_HEAD_DIM), 0.1),
        'diff_lambda_q2': nrm(ks[11], (DEPTH, DIFF_HEAD_DIM), 0.1),
        'diff_lambda_k2': nrm(ks[12], (DEPTH, DIFF_HEAD_DIM), 0.1),
        'diff_subln_g': 1.0 + nrm(ks[13], (DEPTH, 2 * DIFF_HEAD_DIM), 0.02),
        'w_branch_nsa': nrm(ks[14], (DEPTH, NSA_Q_W, D_MODEL), NSA_Q_W ** -0.5 * BETA),
        'w_branch_diff': nrm(ks[15], (DEPTH, DIFF_V_W, D_MODEL), DIFF_V_W ** -0.5 * BETA),
        'w_out': nrm(ks[16], (DEPTH, D_MODEL, D_MODEL), D_MODEL ** -0.5 * BETA),
        'ln1_g': 1.0 + nrm(ks[17], (DEPTH, D_MODEL), 0.02),
        'ln1_b': nrm(ks[18], (DEPTH, D_MODEL), 0.02),
        'w_ffn_in': nrm(ks[19], (DEPTH, D_MODEL, 2 * D_FF), D_MODEL ** -0.5),
        'ffn_conv_w': nrm(ks[20], (DEPTH, CONV_WIDTH, D_FF), CONV_WIDTH ** -0.5),
        'ffn_conv_b': nrm(ks[21], (DEPTH, D_FF), 0.02),
        'w_ffn_down': nrm(ks[22], (DEPTH, D_FF, D_MODEL), D_FF ** -0.5 * BETA),
        'ln2_g': 1.0 + nrm(ks[23], (DEPTH, D_MODEL), 0.02),
        'ln2_b': nrm(ks[24], (DEPTH, D_MODEL), 0.02),
        'w_ple_proj': nrm(ks[25], (DEPTH, D_PLE, D_MODEL), D_PLE ** -0.5),
        'w_ple_gate': nrm(ks[26], (DEPTH, D_MODEL, D_MODEL), D_MODEL ** -0.5),
        'rel_bias_table': nrm(ks[27], (REL_BUCKETS, N_BIAS_HEADS), 0.3),
    }


def reference(x, p, w_in, nsa_cmp_pe_k, nsa_cmp_w1_k, nsa_cmp_w2_k, nsa_cmp_pe_v, nsa_cmp_w1_v,
              nsa_cmp_w2_v, diff_lambda_q1, diff_lambda_k1, diff_lambda_q2, diff_lambda_k2,
              diff_subln_g, w_branch_nsa, w_branch_diff, w_out, ln1_g, ln1_b, w_ffn_in,
              ffn_conv_w, ffn_conv_b, w_ffn_down, ln2_g, ln2_b, w_ple_proj, w_ple_gate,
              rel_bias_table):
    B, S, _ = x.shape
    splits = [int(c) for c in np.cumsum(IN_SIZES)[:-1]]
    kv_shape = (B, S, NSA_GROUPS, NSA_HEAD_DIM)
    for l in range(DEPTH):
        lambda_init = 0.8 - 0.6 * math.exp(-0.3 * l)
        proj = x @ w_in[l]
        (nsa_q, k_cmp, v_cmp, k_slc, v_slc, k_win, v_win, nsa_g,
         d_q, d_k, d_v, gate_nsa, gate_diff) = jnp.split(proj, splits, axis=-1)
        y_nsa = nsa_attention(nsa_q, k_cmp.reshape(kv_shape), v_cmp.reshape(kv_shape),
                              k_slc.reshape(kv_shape), v_slc.reshape(kv_shape),
                              k_win.reshape(kv_shape), v_win.reshape(kv_shape), nsa_g,
                              nsa_cmp_pe_k[l], nsa_cmp_w1_k[l], nsa_cmp_w2_k[l],
                              nsa_cmp_pe_v[l], nsa_cmp_w1_v[l], nsa_cmp_w2_v[l], rel_bias_table)
        y_diff = diff_attention(d_q, d_k, d_v, diff_lambda_q1[l], diff_lambda_k1[l],
                                diff_lambda_q2[l], diff_lambda_k2[l], diff_subln_g[l],
                                rel_bias_table, lambda_init)
        merged = (jax.nn.sigmoid(gate_nsa) * (y_nsa @ w_branch_nsa[l])
                  + jax.nn.sigmoid(gate_diff) * (y_diff @ w_branch_diff[l]))
        x = layer_norm(ALPHA * x + merged @ w_out[l], ln1_g[l], ln1_b[l])
        gu = x @ w_ffn_in[l]
        g, u = jnp.split(gu, 2, axis=-1)
        g = causal_dwconv(g, ffn_conv_w[l], ffn_conv_b[l])
        x = layer_norm(ALPHA * x + (jax.nn.gelu(g) * u) @ w_ffn_down[l], ln2_g[l], ln2_b[l])
        x = x + jax.nn.sigmoid(x @ w_ple_gate[l]) * (p[l] @ w_ple_proj[l])
    return x
```

```python
import functools
import math

import jax
import jax.numpy as jnp
import numpy as np
from jax import lax
from jax.experimental import pallas as pl
from jax.experimental.pallas import tpu as pltpu

F32 = jnp.float32
BF16 = jnp.bfloat16

NSA_HEADS = 8
NSA_GROUPS = 2
NSA_REP = NSA_HEADS // NSA_GROUPS
HEAD_DIM = 64
CMP_BLOCK = 32
CMP_STRIDE = 16
CMP_HIDDEN = 256
SLC_BLOCK = 64
SLC_TOPK = 16
SLC_LOCAL = 2
WINDOW = 512
DIFF_HEADS = 4
REL_BUCKETS = 32
REL_MAX_EXACT = 16
REL_MAX_DIST = 128
D_FF = 2816
CONV_WIDTH = 3
LN_EPS = 1e-5
NEG_INF = -1e30
BIG = 1e30
MASK_EXP = 100

LANES = 128
SUBLANES = 8
VMEM_LIMIT = 56 * 1024 * 1024

NSA_TQ = 128
DIFF_TQ = 256
ROW_TILE = 512
FF_CHUNK = 256
HALO = 16


def _rel_breakpoints():
    n = np.arange(0, 4 * REL_MAX_DIST)
    large = REL_MAX_EXACT + (np.log(np.maximum(n, 1).astype(np.float32) / REL_MAX_EXACT)
                             / np.float32(math.log(REL_MAX_DIST / REL_MAX_EXACT))
                             * (REL_BUCKETS - REL_MAX_EXACT)).astype(np.int32)
    bucket = np.where(n < REL_MAX_EXACT, n, np.minimum(large, REL_BUCKETS - 1))
    assert np.all(np.diff(bucket) >= 0)
    return [int(np.argmax(bucket >= b)) for b in range(1, REL_BUCKETS)]


REL_BREAKS = _rel_breakpoints()


def _dot(a, b):
    return jnp.dot(a, b, preferred_element_type=F32)


def _dot_nt(a, b):
    return lax.dot_general(a, b, (((1,), (1,)), ((), ())), preferred_element_type=F32)


def _rel_bias(dist, tab_ref, head, shift):
    val = jnp.full(dist.shape, tab_ref[0, head] - shift, F32)
    for b, brk in enumerate(REL_BREAKS, start=1):
        val = jnp.where(dist >= brk, tab_ref[b, head] - shift, val)
    return val


def _gelu(x):
    c = math.sqrt(2.0 / math.pi)
    return 0.5 * x * (1.0 + jnp.tanh(c * (x + 0.044715 * (x * x * x))))


def _layer_norm(z, g, b):
    mu = jnp.mean(z, axis=-1, keepdims=True)
    zc = z - mu
    var = jnp.mean(zc * zc, axis=-1, keepdims=True)
    return zc * lax.rsqrt(var + LN_EPS) * g + b


def _params(*sem):
    return pltpu.CompilerParams(dimension_semantics=sem, vmem_limit_bytes=VMEM_LIMIT)


def _proj_kernel(x_ref, wm_ref, wg_ref, om_ref, og_ref):
    xb = x_ref[...].astype(BF16)
    n = wm_ref.shape[1]
    for c in range(0, n, 2 * LANES):
        w = min(2 * LANES, n - c)
        om_ref[:, c:c + w] = _dot(xb, wm_ref[:, c:c + w]).astype(om_ref.dtype)
    og_ref[...] = _dot(xb, wg_ref[...])


def _proj(x2d, w_main, w_gate):
    T, D = x2d.shape
    n = w_main.shape[1]
    return pl.pallas_call(
        _proj_kernel,
        grid=(T // ROW_TILE,),
        in_specs=[pl.BlockSpec((ROW_TILE, D), lambda i: (i, 0)),
                  pl.BlockSpec((D, n), lambda i: (0, 0)),
                  pl.BlockSpec((D, LANES), lambda i: (0, 0))],
        out_specs=[pl.BlockSpec((ROW_TILE, n), lambda i: (i, 0)),
                   pl.BlockSpec((ROW_TILE, LANES), lambda i: (i, 0))],
        out_shape=[jax.ShapeDtypeStruct((T, n), BF16), jax.ShapeDtypeStruct((T, LANES), F32)],
        compiler_params=_params("arbitrary"),
        name="proj",
    )(x2d, w_main, w_gate)


def _compress_kernel(c_ref, pe_ref, w1_ref, w2_ref, o_ref):
    ncp = c_ref.shape[3]
    acc = jnp.zeros((ncp, LANES), F32)
    for g in range(NSA_GROUPS):
        ch = c_ref[0, 0, g].astype(F32)
        a = _dot((ch + pe_ref[0, 0]).astype(BF16), w1_ref[0, 0])
        b = _dot((ch + pe_ref[0, 1]).astype(BF16), w1_ref[0, 1])
        h = a + pltpu.roll(b, ncp - 1, 0)
        acc = acc + _dot(_gelu(h).astype(BF16), w2_ref[0, g])
    o_ref[0, 0] = acc.astype(o_ref.dtype)


def _compress(chunks, pe, w1, w2):
    _, B, G, ncp, cw = chunks.shape
    return pl.pallas_call(
        _compress_kernel,
        grid=(2, B),
        in_specs=[pl.BlockSpec((1, 1, G, ncp, cw), lambda s, b: (s, b, 0, 0, 0)),
                  pl.BlockSpec((1, 2, 1, cw), lambda s, b: (s, 0, 0, 0)),
                  pl.BlockSpec((1, 2, cw, CMP_HIDDEN), lambda s, b: (s, 0, 0, 0)),
                  pl.BlockSpec((1, G, CMP_HIDDEN, LANES), lambda s, b: (s, 0, 0, 0))],
        out_specs=pl.BlockSpec((1, 1, ncp, LANES), lambda s, b: (s, b, 0, 0)),
        out_shape=jax.ShapeDtypeStruct((2, B, ncp, LANES), BF16),
        compiler_params=_params("arbitrary", "arbitrary"),
        name="compress",
    )(chunks, pe, w1, w2)


def _cmp_kernel(tab_ref, q_ref, kc_ref, vc_ref, gate_ref, matt_ref, ocmp_ref, sel_ref, *, n_slc):
    tq = q_ref.shape[1]
    ncp = kc_ref.shape[2]
    q0 = pl.program_id(1) * tq
    lane = lax.broadcasted_iota(jnp.int32, (tq, LANES), 1)
    t_idx = q0 + lax.broadcasted_iota(jnp.int32, (tq, ncp), 0)
    c_idx = lax.broadcasted_iota(jnp.int32, (tq, ncp), 1)
    dist = t_idx - (c_idx * CMP_STRIDE + (CMP_BLOCK - 1))
    valid = dist >= 0
    gates = jax.nn.sigmoid(gate_ref[0])
    eye = (lax.broadcasted_iota(jnp.int32, (tq, tq), 0)
           == lax.broadcasted_iota(jnp.int32, (tq, tq), 1)).astype(BF16)
    kc = kc_ref[0, 0]
    vc = vc_ref[0, 0]
    jrow_i = lax.broadcasted_iota(jnp.int32, (n_slc, tq), 0)
    cur = jnp.right_shift(q0 + lax.broadcasted_iota(jnp.int32, (n_slc, tq), 1), int(math.log2(SLC_BLOCK)))
    forced = (jrow_i == 0) | ((cur - jrow_i >= 0) & (cur - jrow_i < SLC_LOCAL))
    blk_valid = jrow_i <= cur
    jrow = jrow_i.astype(F32)
    gated = []
    for g in range(NSA_GROUPS):
        lane_g = (lane >= HEAD_DIM * g) & (lane < HEAD_DIM * (g + 1))
        psum = jnp.zeros((tq, ncp), F32)
        outs = []
        for r in range(NSA_REP):
            head = g * NSA_REP + r
            qb = q_ref[0, :, r * LANES:(r + 1) * LANES]
            qb = jnp.where(lane_g, qb, jnp.zeros_like(qb))
            logit = _dot_nt(qb, kc) + _rel_bias(dist, tab_ref, head, 0.0)
            logit = jnp.where(valid, logit, NEG_INF)
            m = jnp.max(logit, axis=-1, keepdims=True)
            e = jnp.exp(logit - m)
            p = jnp.where(valid, e / jnp.sum(e, axis=-1, keepdims=True), 0.0)
            psum = psum + p
            o = _dot(p.astype(BF16), vc)
            outs.append(gates[:, head * 3:head * 3 + 1] * o)
        gated.append(outs)
        hi = psum.astype(BF16)
        lo = (psum - hi.astype(F32)).astype(BF16)
        p_slc = _dot_nt(matt_ref[...], hi) + _dot_nt(matt_ref[...], lo)
        score = jnp.where(forced, BIG, jnp.where(blk_valid, p_slc, NEG_INF))
        sel = jnp.zeros((n_slc, tq), F32)
        for _ in range(min(SLC_TOPK, n_slc)):
            mx = jnp.max(score, axis=0, keepdims=True)
            idx = jnp.min(jnp.where(score == mx, jrow, float(n_slc)), axis=0, keepdims=True)
            hit = jrow == idx
            sel = jnp.where(hit, 1.0, sel)
            score = jnp.where(hit, -3.0e38, score)
        selm1 = (sel - 1.0).astype(BF16)
        if n_slc < LANES:
            selm1 = jnp.concatenate([selm1, jnp.zeros((LANES - n_slc, tq), BF16)], axis=0)
        sel_ref[0, g] = _dot_nt(eye, selm1).astype(sel_ref.dtype)
    for r in range(NSA_REP):
        ocmp_ref[0, :, r * LANES:(r + 1) * LANES] = jnp.where(lane < HEAD_DIM, gated[0][r], gated[1][r])


def _cmp_attention(tab, proj3, cmp_kv, gates3, matt, n_slc):
    B, S, _ = proj3.shape
    ncp = cmp_kv.shape[2]
    tq = NSA_TQ
    return pl.pallas_call(
        functools.partial(_cmp_kernel, n_slc=n_slc),
        grid=(B, S // tq),
        in_specs=[pl.BlockSpec(memory_space=pltpu.SMEM),
                  pl.BlockSpec((1, tq, 4 * LANES), lambda b, i: (b, i, 0)),
                  pl.BlockSpec((1, 1, ncp, LANES), lambda b, i: (0, b, 0, 0)),
                  pl.BlockSpec((1, 1, ncp, LANES), lambda b, i: (1, b, 0, 0)),
                  pl.BlockSpec((1, tq, LANES), lambda b, i: (b, i, 0)),
                  pl.BlockSpec((n_slc, ncp), lambda b, i: (0, 0))],
        out_specs=[pl.BlockSpec((1, tq, 4 * LANES), lambda b, i: (b, i, 0)),
                   pl.BlockSpec((1, NSA_GROUPS, tq, LANES), lambda b, i: (b, 0, i, 0))],
        out_shape=[jax.ShapeDtypeStruct((B, S, 4 * LANES), F32),
                   jax.ShapeDtypeStruct((B, NSA_GROUPS, S, LANES), BF16)],
        compiler_params=_params("arbitrary", "arbitrary"),
        name="cmp",
    )(tab, proj3, cmp_kv, cmp_kv, gates3, matt)


def _flash_step(s, v, m_ref, l_ref, acc_ref):
    m_old = m_ref[...]
    m_new = jnp.maximum(m_old, jnp.max(s, axis=-1, keepdims=True))
    alpha = jnp.exp(m_old - m_new)
    p = jnp.exp(s - m_new[:, :1])
    l_ref[...] = alpha * l_ref[...] + jnp.sum(p, axis=-1, keepdims=True)
    acc_ref[...] = alpha * acc_ref[...] + _dot(p.astype(BF16), v)
    m_ref[...] = m_new


def _flash_init(m_ref, l_ref, acc_ref):
    m_ref[...] = jnp.full(m_ref.shape, NEG_INF, F32)
    l_ref[...] = jnp.zeros(l_ref.shape, F32)
    acc_ref[...] = jnp.zeros(acc_ref.shape, F32)


def _nsa_kernel(tab_ref, q_ref, ks_ref, vs_ref, kw_ref, vw_ref, sel_ref, et_ref, gate_ref, ocmp_ref,
                y_ref, tb_ref, far_ref, m_ref, l_ref, acc_ref, part_ref):
    tq = q_ref.shape[1]
    n_win = WINDOW // tq
    qt = pl.program_id(1)

    @pl.when((pl.program_id(0) == 0) & (qt == 0))
    def _():
        ti = lax.broadcasted_iota(jnp.int32, (tq, tq), 0)
        ki = lax.broadcasted_iota(jnp.int32, (tq, tq), 1)
        for g in range(NSA_GROUPS):
            for r in range(NSA_REP):
                head = g * NSA_REP + r
                far_bias = tab_ref[REL_BUCKETS - 1, head]
                rows = slice(r * tq, (r + 1) * tq)
                tb_ref[g, 0, rows, :] = jnp.where(ti >= ki, _rel_bias(ti - ki, tab_ref, head, far_bias), NEG_INF)
                tb_ref[g, 1, rows, :] = _rel_bias(ti - ki + tq, tab_ref, head, far_bias)
        for r in range(NSA_REP):
            far_ref[r * tq:(r + 1) * tq, :] = jnp.where(ti < ki, 0.0, NEG_INF)

    lane = lax.broadcasted_iota(jnp.int32, (tq, LANES), 1)
    gates = jax.nn.sigmoid(gate_ref[0])

    def k_tile(ref, kt):
        return ref[0, pl.ds(pl.multiple_of(kt * tq, tq), tq), :]

    for g in range(NSA_GROUPS):
        lane_g = (lane >= HEAD_DIM * g) & (lane < HEAD_DIM * (g + 1))
        q_parts = []
        for r in range(NSA_REP):
            qb = q_ref[0, :, r * LANES:(r + 1) * LANES]
            q_parts.append(jnp.where(lane_g, qb, jnp.zeros_like(qb)))
        q_g = jnp.concatenate(q_parts, axis=0)
        sel_g = jnp.concatenate([sel_ref[0, g]] * NSA_REP, axis=0)
        qs_g = jnp.concatenate([q_g, sel_g], axis=1)

        def slc_step(kt, bias):
            et = et_ref[pl.ds(pl.multiple_of(kt * tq, tq), tq), :]
            s = _dot_nt(qs_g, jnp.concatenate([k_tile(ks_ref, kt), et], axis=1))
            if bias is not None:
                s = s + bias
            _flash_step(s, k_tile(vs_ref, kt), m_ref, l_ref, acc_ref)

        _flash_init(m_ref, l_ref, acc_ref)

        def far_body(kt, carry):
            slc_step(kt, None)
            return carry

        lax.fori_loop(0, jnp.maximum(qt - 1, 0), far_body, 0)

        @pl.when(qt >= 1)
        def _():
            slc_step(qt - 1, tb_ref[g, 1])

        slc_step(qt, tb_ref[g, 0])
        part_ref[...] = acc_ref[...] / l_ref[...]

        _flash_init(m_ref, l_ref, acc_ref)
        for j in range(n_win, -1, -1):
            if j == n_win:
                bias_of = lambda: far_ref[...]
            elif j == 1:
                bias_of = lambda: tb_ref[g, 1]
            elif j == 0:
                bias_of = lambda: tb_ref[g, 0]
            else:
                bias_of = None

            def win_step(j=j, bias_of=bias_of):
                s = _dot_nt(q_g, k_tile(kw_ref, qt - j))
                if bias_of is not None:
                    s = s + bias_of()
                _flash_step(s, k_tile(vw_ref, qt - j), m_ref, l_ref, acc_ref)

            if j == 0:
                win_step()
            else:
                pl.when(qt >= j)(win_step)
        o_win = acc_ref[...] / l_ref[...]
        o_slc = part_ref[...]

        for r in range(NSA_REP):
            head = g * NSA_REP + r
            rows = slice(r * tq, (r + 1) * tq)
            y = (gates[:, head * 3 + 1:head * 3 + 2] * o_slc[rows]
                 + gates[:, head * 3 + 2:head * 3 + 3] * o_win[rows])
            cols = slice(r * LANES, (r + 1) * LANES)
            if g == 0:
                y_ref[0, :, cols] = (ocmp_ref[0, :, cols] + y).astype(y_ref.dtype)
            else:
                y0 = y_ref[0, :, cols]
                y1 = (ocmp_ref[0, :, cols] + y).astype(y_ref.dtype)
                y_ref[0, :, cols] = jnp.where(lane < HEAD_DIM, y0, y1)


def _nsa_attention(tab, proj3, sel, et, gates3, ocmp, col_blocks):
    B, S, _ = proj3.shape
    tq = NSA_TQ
    assert WINDOW % tq == 0 and WINDOW // tq >= 2 and S % tq == 0
    ks_c, vs_c, kw_c, vw_c = col_blocks
    rows = NSA_REP * tq

    def kv_spec(c):
        return pl.BlockSpec((1, S, LANES), lambda b, i: (b, 0, c))

    return pl.pallas_call(
        _nsa_kernel,
        grid=(B, S // tq),
        in_specs=[pl.BlockSpec(memory_space=pltpu.SMEM),
                  pl.BlockSpec((1, tq, 4 * LANES), lambda b, i: (b, i, 0)),
                  kv_spec(ks_c), kv_spec(vs_c), kv_spec(kw_c), kv_spec(vw_c),
                  pl.BlockSpec((1, NSA_GROUPS, tq, LANES), lambda b, i: (b, 0, i, 0)),
                  pl.BlockSpec((S, LANES), lambda b, i: (0, 0)),
                  pl.BlockSpec((1, tq, LANES), lambda b, i: (b, i, 0)),
                  pl.BlockSpec((1, tq, 4 * LANES), lambda b, i: (b, i, 0))],
        out_specs=pl.BlockSpec((1, tq, 4 * LANES), lambda b, i: (b, i, 0)),
        out_shape=jax.ShapeDtypeStruct((B, S, 4 * LANES), BF16),
        scratch_shapes=[pltpu.VMEM((NSA_GROUPS, 2, rows, tq), F32),
                        pltpu.VMEM((rows, tq), F32),
                        pltpu.VMEM((rows, LANES), F32),
                        pltpu.VMEM((rows, LANES), F32),
                        pltpu.VMEM((rows, LANES), F32),
                        pltpu.VMEM((rows, LANES), F32)],
        compiler_params=_params("arbitrary", "arbitrary"),
        name="nsa",
    )(tab, proj3, proj3, proj3, proj3, proj3, sel, et, gates3, ocmp)


def _diff_kernel(tab_ref, q_ref, k_ref, v_ref, lam_ref, g_ref, y_ref, tb_ref, m_ref, l_ref, acc_ref,
                 *, lambda_init):
    tq = q_ref.shape[1]
    h = pl.program_id(1)
    qt = pl.program_id(2)

    @pl.when(qt == 0)
    def _():
        ti = lax.broadcasted_iota(jnp.int32, (tq, tq), 0)
        ki = lax.broadcasted_iota(jnp.int32, (tq, tq), 1)
        head = NSA_HEADS + h
        far_bias = tab_ref[REL_BUCKETS - 1, head]
        tb_ref[0] = jnp.where(ti >= ki, _rel_bias(ti - ki, tab_ref, head, far_bias), NEG_INF)
        tb_ref[1] = _rel_bias(ti - ki + tq, tab_ref, head, far_bias)

    lane = lax.broadcasted_iota(jnp.int32, (tq, LANES), 1)
    qb = q_ref[0]
    zero = jnp.zeros_like(qb)
    q2 = jnp.concatenate([jnp.where(lane < HEAD_DIM, qb, zero), jnp.where(lane >= HEAD_DIM, qb, zero)], axis=0)

    def k_tile(ref, kt):
        return ref[0, pl.ds(pl.multiple_of(kt * tq, tq), tq), :]

    def step(kt, bias):
        s = _dot_nt(q2, k_tile(k_ref, kt))
        if bias is not None:
            s = s + jnp.concatenate([bias, bias], axis=0)
        _flash_step(s, k_tile(v_ref, kt), m_ref, l_ref, acc_ref)

    _flash_init(m_ref, l_ref, acc_ref)

    def far_body(kt, carry):
        step(kt, None)
        return carry

    lax.fori_loop(0, jnp.maximum(qt - 1, 0), far_body, 0)

    @pl.when(qt >= 1)
    def _():
        step(qt - 1, tb_ref[1])

    step(qt, tb_ref[0])
    a = acc_ref[...] / l_ref[...]
    lq1, lk1, lq2, lk2 = lam_ref[0:1, :], lam_ref[1:2, :], lam_ref[2:3, :], lam_ref[3:4, :]
    lam = (jnp.exp(jnp.sum(lq1 * lk1, axis=-1, keepdims=True))
           - jnp.exp(jnp.sum(lq2 * lk2, axis=-1, keepdims=True)) + lambda_init)
    o = a[:tq] - lam * a[tq:]
    o = o * lax.rsqrt(jnp.mean(o * o, axis=-1, keepdims=True) + LN_EPS) * g_ref[...]
    y_ref[0] = (o * (1.0 - lambda_init)).astype(y_ref.dtype)


def _diff_attention(tab, proj3, lam4, subln_g, col_blocks, lambda_init):
    B, S, _ = proj3.shape
    tq = min(DIFF_TQ, S)
    q_c, k_c, v_c = col_blocks
    return pl.pallas_call(
        functools.partial(_diff_kernel, lambda_init=lambda_init),
        grid=(B, DIFF_HEADS, S // tq),
        in_specs=[pl.BlockSpec(memory_space=pltpu.SMEM),
                  pl.BlockSpec((1, tq, LANES), lambda b, h, i: (b, i, q_c + h)),
                  pl.BlockSpec((1, S, LANES), lambda b, h, i: (b, 0, k_c + h)),
                  pl.BlockSpec((1, S, LANES), lambda b, h, i: (b, 0, v_c + h)),
                  pl.BlockSpec((SUBLANES, HEAD_DIM), lambda b, h, i: (0, 0)),
                  pl.BlockSpec((1, LANES), lambda b, h, i: (0, 0))],
        out_specs=pl.BlockSpec((1, tq, LANES), lambda b, h, i: (b, i, h)),
        out_shape=jax.ShapeDtypeStruct((B, S, DIFF_HEADS * LANES), BF16),
        scratch_shapes=[pltpu.VMEM((2, tq, tq), F32),
                        pltpu.VMEM((2 * tq, LANES), F32),
                        pltpu.VMEM((2 * tq, LANES), F32),
                        pltpu.VMEM((2 * tq, LANES), F32)],
        compiler_params=_params("arbitrary", "arbitrary", "arbitrary"),
        name="diff",
    )(tab, proj3, proj3, proj3, lam4, subln_g)


def _merge_kernel(x_ref, yn_ref, yd_ref, wgn_ref, wgd_ref, wbn_ref, wbd_ref, wo_ref, g_ref, b_ref, o_ref,
                  *, alpha):
    x = x_ref[...]
    xb = x.astype(BF16)
    merged = (jax.nn.sigmoid(_dot(xb, wgn_ref[...])) * _dot(yn_ref[...], wbn_ref[...])
              + jax.nn.sigmoid(_dot(xb, wgd_ref[...])) * _dot(yd_ref[...], wbd_ref[...]))
    z = alpha * x + _dot(merged.astype(BF16), wo_ref[...])
    o_ref[...] = _layer_norm(z, g_ref[...], b_ref[...])


def _merge(x2d, y_nsa, y_diff, wgn, wgd, wbn, wbd, wo, ln_g, ln_b, alpha):
    T, D = x2d.shape
    const = lambda shape: pl.BlockSpec(shape, lambda i: (0, 0))
    return pl.pallas_call(
        functools.partial(_merge_kernel, alpha=alpha),
        grid=(T // ROW_TILE,),
        in_specs=[pl.BlockSpec((ROW_TILE, D), lambda i: (i, 0)),
                  pl.BlockSpec((ROW_TILE, y_nsa.shape[1]), lambda i: (i, 0)),
                  pl.BlockSpec((ROW_TILE, y_diff.shape[1]), lambda i: (i, 0)),
                  const(wgn.shape), const(wgd.shape), const(wbn.shape), const(wbd.shape), const(wo.shape),
                  const((1, D)), const((1, D))],
        out_specs=pl.BlockSpec((ROW_TILE, D), lambda i: (i, 0)),
        out_shape=jax.ShapeDtypeStruct((T, D), F32),
        compiler_params=_params("arbitrary"),
        name="merge",
    )(x2d, y_nsa, y_diff, wgn, wgd, wbn, wbd, wo, ln_g, ln_b)


def _ffn_kernel(x_ref, halo_ref, p_ref, wg_ref, wu_ref, cw_ref, cb_ref, wd_ref, g_ref, b_ref, wpg_ref, wpp_ref,
                o_ref, acc_ref, *, alpha, tiles_per_seq):
    tm = x_ref.shape[0]
    x = x_ref[...]
    xb = x.astype(BF16)
    keep = (pl.program_id(0) % tiles_per_seq != 0).astype(F32)
    hb = halo_ref[...].astype(BF16)
    row = lax.broadcasted_iota(jnp.int32, (tm, FF_CHUNK), 0)
    acc_ref[...] = jnp.zeros(acc_ref.shape, F32)

    def body(c, carry):
        gm = _dot(xb, wg_ref[c])
        gh = _dot(hb, wg_ref[c]) * keep
        um = _dot(xb, wu_ref[c])
        h1 = gh[HALO - 1:HALO, :]
        h2 = gh[HALO - 2:HALO - 1, :]
        g1 = jnp.where(row == 0, h1, pltpu.roll(gm, 1, 0))
        g2 = jnp.where(row == 0, h2, jnp.where(row == 1, h1, pltpu.roll(gm, 2, 0)))
        cw = cw_ref[c]
        conv = cb_ref[c] + cw[0:1, :] * g2 + cw[1:2, :] * g1 + cw[2:3, :] * gm
        act = (_gelu(conv) * um).astype(BF16)
        acc_ref[...] += _dot(act, wd_ref[c])
        return carry

    lax.fori_loop(0, wg_ref.shape[0], body, 0)
    x2 = _layer_norm(alpha * x + acc_ref[...], g_ref[...], b_ref[...])
    gate = jax.nn.sigmoid(_dot(x2.astype(BF16), wpg_ref[...]))
    o_ref[...] = x2 + gate * _dot(p_ref[...].astype(BF16), wpp_ref[...])


def _ffn(x1, p2d, wg, wu, cw, cb, wd, ln_g, ln_b, wpg, wpp, alpha, seq):
    T, D = x1.shape
    tm = ROW_TILE
    assert seq % tm == 0
    nc = wg.shape[0]
    hb = tm // HALO
    const3 = lambda shape: pl.BlockSpec(shape, lambda i: (0, 0, 0))
    const2 = lambda shape: pl.BlockSpec(shape, lambda i: (0, 0))
    return pl.pallas_call(
        functools.partial(_ffn_kernel, alpha=alpha, tiles_per_seq=seq // tm),
        grid=(T // tm,),
        in_specs=[pl.BlockSpec((tm, D), lambda i: (i, 0)),
                  pl.BlockSpec((HALO, D), lambda i: (jnp.maximum(i * hb - 1, 0), 0)),
                  pl.BlockSpec((tm, p2d.shape[1]), lambda i: (i, 0)),
                  const3(wg.shape), const3(wu.shape), const3(cw.shape), const3(cb.shape), const3(wd.shape),
                  const2((1, D)), const2((1, D)), const2(wpg.shape), const2(wpp.shape)],
        out_specs=pl.BlockSpec((tm, D), lambda i: (i, 0)),
        out_shape=jax.ShapeDtypeStruct((T, D), F32),
        scratch_shapes=[pltpu.VMEM((tm, D), F32)],
        compiler_params=_params("arbitrary"),
        name="ffn",
    )(x1, x1, p2d, wg, wu, cw, cb, wd, ln_g, ln_b, wpg, wpp)


def _slc_from_cmp_t(ncp, n_slc):
    ratio = SLC_BLOCK // CMP_STRIDE
    span = CMP_BLOCK // CMP_STRIDE
    mat = np.zeros((n_slc, ncp), np.float32)
    for j in range(n_slc):
        for m in range(ratio):
            for n in range(span):
                i = ratio * j + m - n
                if 0 <= i < ncp - 1:
                    mat[j, i] += 1.0
    return mat


def _layer(x, p_l, w_in, pe_k, w1_k, w2_k, pe_v, w1_v, w2_v, lq1, lk1, lq2, lk2, subln_g, w_bn, w_bd, w_out,
           ln1_g, ln1_b, w_ffn_in, conv_w, conv_b, w_down, ln2_g, ln2_b, w_pp, w_pg, tab, lambda_init,
           alpha):
    B, S, D = x.shape
    T = B * S
    ncp = S // CMP_STRIDE
    n_slc = S // SLC_BLOCK
    q_w = NSA_HEADS * HEAD_DIM
    kv_w = NSA_GROUPS * HEAD_DIM
    dqk_w = DIFF_HEADS * 2 * HEAD_DIM
    sizes = (q_w,) + (kv_w,) * 6 + (NSA_HEADS * 3, dqk_w, dqk_w, dqk_w, D, D)
    offs = np.concatenate([[0], np.cumsum(sizes)])
    col = lambda i: w_in[:, int(offs[i]):int(offs[i + 1])]
    scale = HEAD_DIM ** -0.5

    n_idx = np.arange(q_w)
    perm = (NSA_REP * ((n_idx % LANES) // HEAD_DIM) + n_idx // LANES) * HEAD_DIM + n_idx % HEAD_DIM
    w_main = jnp.concatenate([col(0)[:, perm] * scale] + [col(i) for i in range(1, 7)]
                             + [col(8) * scale, col(9), col(10)], axis=1).astype(BF16)
    w_gate = jnp.pad(col(7), ((0, 0), (0, LANES - NSA_HEADS * 3))).astype(BF16)
    x2d = x.reshape(T, D)
    proj, gates = _proj(x2d, w_main, w_gate)
    proj3 = proj.reshape(B, S, proj.shape[1])
    gates3 = gates.reshape(B, S, LANES)
    c_kcmp, c_vcmp, c_kslc, c_vslc, c_kwin, c_vwin = (q_w // LANES + i for i in range(6))
    c_dq = q_w // LANES + 6
    c_dk = c_dq + DIFF_HEADS
    c_dv = c_dk + DIFF_HEADS

    kv_cmp = proj3[:, :, c_kcmp * LANES:(c_vcmp + 1) * LANES]
    chunks = kv_cmp.reshape(B, ncp, CMP_STRIDE, 2, NSA_GROUPS, HEAD_DIM).transpose(3, 0, 4, 1, 2, 5)
    chunks = chunks.reshape(2, B, NSA_GROUPS, ncp, CMP_STRIDE * HEAD_DIM)
    half = CMP_STRIDE * HEAD_DIM
    pe = jnp.stack([pe_k, pe_v]).reshape(2, 2, 1, half)
    w1 = jnp.stack([w1_k, w1_v]).reshape(2, 2, half, CMP_HIDDEN).astype(BF16)
    w2 = jnp.stack([w2_k, w2_v])
    w2p = jnp.stack([jnp.pad(w2, ((0, 0), (0, 0), (g * HEAD_DIM, LANES - (g + 1) * HEAD_DIM)))
                     for g in range(NSA_GROUPS)], axis=1).astype(BF16)
    cmp_kv = _compress(chunks, pe, w1, w2p)

    matt = jnp.asarray(_slc_from_cmp_t(ncp, n_slc), BF16)
    ocmp, sel = _cmp_attention(tab, proj3, cmp_kv, gates3, matt, n_slc)
    et_np = np.zeros((S, LANES), np.float32)
    et_np[np.arange(S), np.arange(S) // SLC_BLOCK] = 2.0 ** MASK_EXP
    y_nsa = _nsa_attention(tab, proj3, sel, jnp.asarray(et_np, BF16), gates3, ocmp,
                           (c_kslc, c_vslc, c_kwin, c_vwin))

    lam4 = jnp.pad(jnp.stack([lq1, lk1, lq2, lk2]), ((0, SUBLANES - 4), (0, 0)))
    y_diff = _diff_attention(tab, proj3, lam4, subln_g.reshape(1, LANES), (c_dq, c_dk, c_dv), lambda_init)

    x1 = _merge(x2d, y_nsa.reshape(T, q_w), y_diff.reshape(T, dqk_w),
                col(11).astype(BF16), col(12).astype(BF16), w_bn[perm].astype(BF16), w_bd.astype(BF16),
                w_out.astype(BF16), ln1_g.reshape(1, D), ln1_b.reshape(1, D), alpha)

    nc = D_FF // FF_CHUNK
    chunked = lambda w: w.reshape(w.shape[0], nc, FF_CHUNK).transpose(1, 0, 2)
    wg = chunked(w_ffn_in[:, :D_FF]).astype(BF16)
    wu = chunked(w_ffn_in[:, D_FF:]).astype(BF16)
    cw = chunked(jnp.pad(conv_w, ((0, SUBLANES - CONV_WIDTH), (0, 0))))
    cb = chunked(conv_b.reshape(1, D_FF))
    wd = w_down.reshape(nc, FF_CHUNK, D).astype(BF16)
    out = _ffn(x1, p_l.reshape(T, p_l.shape[-1]), wg, wu, cw, cb, wd, ln2_g.reshape(1, D), ln2_b.reshape(1, D),
               w_pg.astype(BF16), w_pp.astype(BF16), alpha, S)
    return out.reshape(B, S, D)


def kernel(x, p, w_in, nsa_cmp_pe_k, nsa_cmp_w1_k, nsa_cmp_w2_k, nsa_cmp_pe_v, nsa_cmp_w1_v, nsa_cmp_w2_v, diff_lambda_q1, diff_lambda_k1, diff_lambda_q2, diff_lambda_k2, diff_subln_g, w_branch_nsa, w_branch_diff, w_out, ln1_g, ln1_b, w_ffn_in, ffn_conv_w, ffn_conv_b, w_ffn_down, ln2_g, ln2_b, w_ple_proj, w_ple_gate, rel_bias_table):
    depth = w_in.shape[0]
    alpha = (2.0 * depth) ** 0.25
    for l in range(depth):
        lambda_init = 0.8 - 0.6 * math.exp(-0.3 * l)
        x = _layer(x, p[l], w_in[l], nsa_cmp_pe_k[l], nsa_cmp_w1_k[l], nsa_cmp_w2_k[l], nsa_cmp_pe_v[l],
                   nsa_cmp_w1_v[l], nsa_cmp_w2_v[l], diff_lambda_q1[l], diff_lambda_k1[l], diff_lambda_q2[l],
                   diff_lambda_k2[l], diff_subln_g[l], w_branch_nsa[l], w_branch_diff[l], w_out[l], ln1_g[l],
                   ln1_b[l], w_ffn_in[l], ffn_conv_w[l], ffn_conv_b[l], w_ffn_down[l], ln2_g[l], ln2_b[l],
                   w_ple_proj[l], w_ple_gate[l], rel_bias_table, lambda_init, alpha)
    return x
```

```python
import functools
import math

import jax
import jax.numpy as jnp
import numpy as np
from jax import lax
from jax.experimental import pallas as pl
from jax.experimental.pallas import tpu as pltpu

F32 = jnp.float32
BF16 = jnp.bfloat16

NSA_HEADS = 8
NSA_GROUPS = 2
NSA_REP = NSA_HEADS // NSA_GROUPS
HEAD_DIM = 64
CMP_BLOCK = 32
CMP_STRIDE = 16
CMP_HIDDEN = 256
SLC_BLOCK = 64
SLC_TOPK = 16
SLC_LOCAL = 2
WINDOW = 512
DIFF_HEADS = 4
REL_BUCKETS = 32
REL_MAX_EXACT = 16
REL_MAX_DIST = 128
D_FF = 2816
CONV_WIDTH = 3
LN_EPS = 1e-5
NEG_INF = -1e30
BIG = 1e30
MASK_EXP = 100

LANES = 128
SUBLANES = 8
VMEM_LIMIT = 56 * 1024 * 1024

CMP_TQ = 128
NSA_TQ = 256
DIFF_TQ = 256
ROW_TILE = 512
FF_CHUNK = 256
HALO = 16


def _rel_breakpoints():
    n = np.arange(0, 4 * REL_MAX_DIST)
    large = REL_MAX_EXACT + (np.log(np.maximum(n, 1).astype(np.float32) / REL_MAX_EXACT)
                             / np.float32(math.log(REL_MAX_DIST / REL_MAX_EXACT))
                             * (REL_BUCKETS - REL_MAX_EXACT)).astype(np.int32)
    bucket = np.where(n < REL_MAX_EXACT, n, np.minimum(large, REL_BUCKETS - 1))
    assert np.all(np.diff(bucket) >= 0)
    return [int(np.argmax(bucket >= b)) for b in range(1, REL_BUCKETS)]


REL_BREAKS = _rel_breakpoints()


def _dot(a, b):
    return jnp.dot(a, b, preferred_element_type=F32)


def _dot_nt(a, b):
    return lax.dot_general(a, b, (((1,), (1,)), ((), ())), preferred_element_type=F32)


def _rel_bias(dist, tab_ref, head, shift):
    val = jnp.full(dist.shape, tab_ref[0, head] - shift, F32)
    for b, brk in enumerate(REL_BREAKS, start=1):
        val = jnp.where(dist >= brk, tab_ref[b, head] - shift, val)
    return val


def _gelu(x):
    c = math.sqrt(2.0 / math.pi)
    return 0.5 * x * (1.0 + jnp.tanh(c * (x + 0.044715 * (x * x * x))))


def _layer_norm(z, g, b):
    mu = jnp.mean(z, axis=-1, keepdims=True)
    zc = z - mu
    var = jnp.mean(zc * zc, axis=-1, keepdims=True)
    return zc * lax.rsqrt(var + LN_EPS) * g + b


def _params(*sem):
    return pltpu.CompilerParams(dimension_semantics=sem, vmem_limit_bytes=VMEM_LIMIT)


def _proj_kernel(x_ref, wm_ref, wg_ref, om_ref, og_ref):
    xb = x_ref[...].astype(BF16)
    n = wm_ref.shape[1]
    for c in range(0, n, 2 * LANES):
        w = min(2 * LANES, n - c)
        om_ref[:, c:c + w] = _dot(xb, wm_ref[:, c:c + w]).astype(om_ref.dtype)
    og_ref[...] = _dot(xb, wg_ref[...])


def _proj(x2d, w_main, w_gate):
    T, D = x2d.shape
    n = w_main.shape[1]
    return pl.pallas_call(
        _proj_kernel,
        grid=(T // ROW_TILE,),
        in_specs=[pl.BlockSpec((ROW_TILE, D), lambda i: (i, 0)),
                  pl.BlockSpec((D, n), lambda i: (0, 0)),
                  pl.BlockSpec((D, LANES), lambda i: (0, 0))],
        out_specs=[pl.BlockSpec((ROW_TILE, n), lambda i: (i, 0)),
                   pl.BlockSpec((ROW_TILE, LANES), lambda i: (i, 0))],
        out_shape=[jax.ShapeDtypeStruct((T, n), BF16), jax.ShapeDtypeStruct((T, LANES), F32)],
        compiler_params=_params("arbitrary"),
        name="proj",
    )(x2d, w_main, w_gate)


def _compress_kernel(c_ref, pe_ref, w1_ref, w2_ref, o_ref):
    ncp = c_ref.shape[3]
    acc = jnp.zeros((ncp, LANES), F32)
    for g in range(NSA_GROUPS):
        ch = c_ref[0, 0, g].astype(F32)
        a = _dot((ch + pe_ref[0, 0]).astype(BF16), w1_ref[0, 0])
        b = _dot((ch + pe_ref[0, 1]).astype(BF16), w1_ref[0, 1])
        h = a + pltpu.roll(b, ncp - 1, 0)
        acc = acc + _dot(_gelu(h).astype(BF16), w2_ref[0, g])
    o_ref[0, 0] = acc.astype(o_ref.dtype)


def _compress(chunks, pe, w1, w2):
    _, B, G, ncp, cw = chunks.shape
    return pl.pallas_call(
        _compress_kernel,
        grid=(2, B),
        in_specs=[pl.BlockSpec((1, 1, G, ncp, cw), lambda s, b: (s, b, 0, 0, 0)),
                  pl.BlockSpec((1, 2, 1, cw), lambda s, b: (s, 0, 0, 0)),
                  pl.BlockSpec((1, 2, cw, CMP_HIDDEN), lambda s, b: (s, 0, 0, 0)),
                  pl.BlockSpec((1, G, CMP_HIDDEN, LANES), lambda s, b: (s, 0, 0, 0))],
        out_specs=pl.BlockSpec((1, 1, ncp, LANES), lambda s, b: (s, b, 0, 0)),
        out_shape=jax.ShapeDtypeStruct((2, B, ncp, LANES), BF16),
        compiler_params=_params("arbitrary", "arbitrary"),
        name="compress",
    )(chunks, pe, w1, w2)


def _cmp_kernel(tab_ref, q_ref, kc_ref, vc_ref, gate_ref, matt_ref, ocmp_ref, sel_ref, *, n_slc):
    tq = q_ref.shape[1]
    ncp = kc_ref.shape[2]
    q0 = pl.program_id(1) * tq
    lane = lax.broadcasted_iota(jnp.int32, (tq, LANES), 1)
    t_idx = q0 + lax.broadcasted_iota(jnp.int32, (tq, ncp), 0)
    c_idx = lax.broadcasted_iota(jnp.int32, (tq, ncp), 1)
    dist = t_idx - (c_idx * CMP_STRIDE + (CMP_BLOCK - 1))
    valid = dist >= 0
    gates = jax.nn.sigmoid(gate_ref[0])
    eye = (lax.broadcasted_iota(jnp.int32, (tq, tq), 0)
           == lax.broadcasted_iota(jnp.int32, (tq, tq), 1)).astype(BF16)
    kc = kc_ref[0, 0]
    vc = vc_ref[0, 0]
    jrow_i = lax.broadcasted_iota(jnp.int32, (n_slc, tq), 0)
    cur = jnp.right_shift(q0 + lax.broadcasted_iota(jnp.int32, (n_slc, tq), 1), int(math.log2(SLC_BLOCK)))
    forced = (jrow_i == 0) | ((cur - jrow_i >= 0) & (cur - jrow_i < SLC_LOCAL))
    blk_valid = jrow_i <= cur
    jrow = jrow_i.astype(F32)
    gated = []
    for g in range(NSA_GROUPS):
        lane_g = (lane >= HEAD_DIM * g) & (lane < HEAD_DIM * (g + 1))
        psum = jnp.zeros((tq, ncp), F32)
        outs = []
        for r in range(NSA_REP):
            head = g * NSA_REP + r
            qb = q_ref[0, :, r * LANES:(r + 1) * LANES]
            qb = jnp.where(lane_g, qb, jnp.zeros_like(qb))
            logit = _dot_nt(qb, kc) + _rel_bias(dist, tab_ref, head, 0.0)
            logit = jnp.where(valid, logit, NEG_INF)
            m = jnp.max(logit, axis=-1, keepdims=True)
            e = jnp.exp(logit - m)
            p = jnp.where(valid, e / jnp.sum(e, axis=-1, keepdims=True), 0.0)
            psum = psum + p
            o = _dot(p.astype(BF16), vc)
            outs.append(gates[:, head * 3:head * 3 + 1] * o)
        gated.append(outs)
        hi = psum.astype(BF16)
        lo = (psum - hi.astype(F32)).astype(BF16)
        p_slc = _dot_nt(matt_ref[...], hi) + _dot_nt(matt_ref[...], lo)
        score = jnp.where(forced, BIG, jnp.where(blk_valid, p_slc, NEG_INF))
        sel = jnp.zeros((n_slc, tq), F32)
        for _ in range(min(SLC_TOPK, n_slc)):
            mx = jnp.max(score, axis=0, keepdims=True)
            idx = jnp.min(jnp.where(score == mx, jrow, float(n_slc)), axis=0, keepdims=True)
            hit = jrow == idx
            sel = jnp.where(hit, 1.0, sel)
            score = jnp.where(hit, -3.0e38, score)
        selm1 = (sel - 1.0).astype(BF16)
        if n_slc < LANES:
            selm1 = jnp.concatenate([selm1, jnp.zeros((LANES - n_slc, tq), BF16)], axis=0)
        sel_ref[0, g] = _dot_nt(eye, selm1).astype(sel_ref.dtype)
    for r in range(NSA_REP):
        ocmp_ref[0, :, r * LANES:(r + 1) * LANES] = jnp.where(lane < HEAD_DIM, gated[0][r], gated[1][r])


def _cmp_attention(tab, proj3, cmp_kv, gates3, matt, n_slc):
    B, S, _ = proj3.shape
    ncp = cmp_kv.shape[2]
    tq = CMP_TQ
    return pl.pallas_call(
        functools.partial(_cmp_kernel, n_slc=n_slc),
        grid=(B, S // tq),
        in_specs=[pl.BlockSpec(memory_space=pltpu.SMEM),
                  pl.BlockSpec((1, tq, 4 * LANES), lambda b, i: (b, i, 0)),
                  pl.BlockSpec((1, 1, ncp, LANES), lambda b, i: (0, b, 0, 0)),
                  pl.BlockSpec((1, 1, ncp, LANES), lambda b, i: (1, b, 0, 0)),
                  pl.BlockSpec((1, tq, LANES), lambda b, i: (b, i, 0)),
                  pl.BlockSpec((n_slc, ncp), lambda b, i: (0, 0))],
        out_specs=[pl.BlockSpec((1, tq, 4 * LANES), lambda b, i: (b, i, 0)),
                   pl.BlockSpec((1, NSA_GROUPS, tq, LANES), lambda b, i: (b, 0, i, 0))],
        out_shape=[jax.ShapeDtypeStruct((B, S, 4 * LANES), F32),
                   jax.ShapeDtypeStruct((B, NSA_GROUPS, S, LANES), BF16)],
        compiler_params=_params("arbitrary", "arbitrary"),
        name="cmp",
    )(tab, proj3, cmp_kv, cmp_kv, gates3, matt)


def _two_pass_attention(for_each_tile, score, value, m_ref, l_ref, acc_ref):
    m_ref[...] = jnp.full(m_ref.shape, NEG_INF, F32)

    def max_step(kt, bias):
        s = score(kt, bias)
        mx = s[:, :LANES]
        for c in range(LANES, s.shape[1], LANES):
            mx = jnp.maximum(mx, s[:, c:c + LANES])
        m_ref[...] = jnp.maximum(m_ref[...], mx)

    for_each_tile(max_step)
    m_ref[...] = jnp.broadcast_to(jnp.max(m_ref[...], axis=-1, keepdims=True), m_ref.shape)
    l_ref[...] = jnp.zeros(l_ref.shape, F32)
    acc_ref[...] = jnp.zeros(acc_ref.shape, F32)

    def sum_step(kt, bias):
        s = score(kt, bias)
        m = m_ref[...]
        ps = [jnp.exp(s[:, c:c + LANES] - m) for c in range(0, s.shape[1], LANES)]
        l_ref[...] += functools.reduce(lambda a, b: a + b, ps)
        acc_ref[...] += _dot(jnp.concatenate([p.astype(BF16) for p in ps], axis=1), value(kt))

    for_each_tile(sum_step)
    return acc_ref[...] / jnp.sum(l_ref[...], axis=-1, keepdims=True)


def _nsa_kernel(tab_ref, q_ref, ks_ref, vs_ref, kw_ref, vw_ref, sel_ref, et_ref, gate_ref, ocmp_ref,
                y_ref, tb_ref, far_ref, m_ref, l_ref, acc_ref, part_ref):
    tq = q_ref.shape[1]
    n_win = WINDOW // tq
    qt = pl.program_id(1)

    @pl.when((pl.program_id(0) == 0) & (qt == 0))
    def _():
        ti = lax.broadcasted_iota(jnp.int32, (tq, tq), 0)
        ki = lax.broadcasted_iota(jnp.int32, (tq, tq), 1)
        for g in range(NSA_GROUPS):
            for r in range(NSA_REP):
                head = g * NSA_REP + r
                far_bias = tab_ref[REL_BUCKETS - 1, head]
                rows = slice(r * tq, (r + 1) * tq)
                tb_ref[g, 0, rows, :] = jnp.where(ti >= ki, _rel_bias(ti - ki, tab_ref, head, far_bias), NEG_INF)
                tb_ref[g, 1, rows, :] = _rel_bias(ti - ki + tq, tab_ref, head, far_bias)
        for r in range(NSA_REP):
            far_ref[r * tq:(r + 1) * tq, :] = jnp.where(ti < ki, 0.0, NEG_INF)

    lane = lax.broadcasted_iota(jnp.int32, (tq, LANES), 1)
    gates = jax.nn.sigmoid(gate_ref[0])

    def k_tile(ref, kt):
        return ref[0, pl.ds(pl.multiple_of(kt * tq, tq), tq), :]

    for g in range(NSA_GROUPS):
        lane_g = (lane >= HEAD_DIM * g) & (lane < HEAD_DIM * (g + 1))
        q_parts = []
        for r in range(NSA_REP):
            qb = q_ref[0, :, r * LANES:(r + 1) * LANES]
            q_parts.append(jnp.where(lane_g, qb, jnp.zeros_like(qb)))
        q_g = jnp.concatenate(q_parts, axis=0)
        sel_g = jnp.concatenate([sel_ref[0, g]] * NSA_REP, axis=0)
        qs_g = jnp.concatenate([q_g, sel_g], axis=1)

        def slc_score(kt, bias):
            et = et_ref[pl.ds(pl.multiple_of(kt * tq, tq), tq), :]
            s = _dot_nt(qs_g, jnp.concatenate([k_tile(ks_ref, kt), et], axis=1))
            return s if bias is None else s + bias()

        def slc_tiles(fn):
            def far_body(kt, carry):
                fn(kt, None)
                return carry

            lax.fori_loop(0, jnp.maximum(qt - 1, 0), far_body, 0)
            pl.when(qt >= 1)(lambda: fn(qt - 1, lambda: tb_ref[g, 1]))
            fn(qt, lambda: tb_ref[g, 0])

        part_ref[...] = _two_pass_attention(slc_tiles, slc_score, lambda kt: k_tile(vs_ref, kt),
                                            m_ref, l_ref, acc_ref)

        def win_score(kt, bias):
            s = _dot_nt(q_g, k_tile(kw_ref, kt))
            return s if bias is None else s + bias()

        def win_tiles(fn):
            for j in range(n_win, -1, -1):
                if j == n_win:
                    bias = lambda: far_ref[...]
                elif j == 1:
                    bias = lambda: tb_ref[g, 1]
                elif j == 0:
                    bias = lambda: tb_ref[g, 0]
                else:
                    bias = None
                if j == 0:
                    fn(qt, bias)
                else:
                    pl.when(qt >= j)(functools.partial(fn, qt - j, bias))

        o_win = _two_pass_attention(win_tiles, win_score, lambda kt: k_tile(vw_ref, kt), m_ref, l_ref, acc_ref)
        o_slc = part_ref[...]

        for r in range(NSA_REP):
            head = g * NSA_REP + r
            rows = slice(r * tq, (r + 1) * tq)
            y = (gates[:, head * 3 + 1:head * 3 + 2] * o_slc[rows]
                 + gates[:, head * 3 + 2:head * 3 + 3] * o_win[rows])
            cols = slice(r * LANES, (r + 1) * LANES)
            if g == 0:
                y_ref[0, :, cols] = (ocmp_ref[0, :, cols] + y).astype(y_ref.dtype)
            else:
                y0 = y_ref[0, :, cols]
                y1 = (ocmp_ref[0, :, cols] + y).astype(y_ref.dtype)
                y_ref[0, :, cols] = jnp.where(lane < HEAD_DIM, y0, y1)


def _nsa_attention(tab, proj3, sel, et, gates3, ocmp, col_blocks):
    B, S, _ = proj3.shape
    tq = NSA_TQ
    assert WINDOW % tq == 0 and WINDOW // tq >= 2 and S % tq == 0
    ks_c, vs_c, kw_c, vw_c = col_blocks
    rows = NSA_REP * tq

    def kv_spec(c):
        return pl.BlockSpec((1, S, LANES), lambda b, i: (b, 0, c))

    return pl.pallas_call(
        _nsa_kernel,
        grid=(B, S // tq),
        in_specs=[pl.BlockSpec(memory_space=pltpu.SMEM),
                  pl.BlockSpec((1, tq, 4 * LANES), lambda b, i: (b, i, 0)),
                  kv_spec(ks_c), kv_spec(vs_c), kv_spec(kw_c), kv_spec(vw_c),
                  pl.BlockSpec((1, NSA_GROUPS, tq, LANES), lambda b, i: (b, 0, i, 0)),
                  pl.BlockSpec((S, LANES), lambda b, i: (0, 0)),
                  pl.BlockSpec((1, tq, LANES), lambda b, i: (b, i, 0)),
                  pl.BlockSpec((1, tq, 4 * LANES), lambda b, i: (b, i, 0))],
        out_specs=pl.BlockSpec((1, tq, 4 * LANES), lambda b, i: (b, i, 0)),
        out_shape=jax.ShapeDtypeStruct((B, S, 4 * LANES), BF16),
        scratch_shapes=[pltpu.VMEM((NSA_GROUPS, 2, rows, tq), F32),
                        pltpu.VMEM((rows, tq), F32),
                        pltpu.VMEM((rows, LANES), F32),
                        pltpu.VMEM((rows, LANES), F32),
                        pltpu.VMEM((rows, LANES), F32),
                        pltpu.VMEM((rows, LANES), F32)],
        compiler_params=_params("arbitrary", "arbitrary"),
        name="nsa",
    )(tab, proj3, proj3, proj3, proj3, proj3, sel, et, gates3, ocmp)


def _diff_kernel(tab_ref, q_ref, k_ref, v_ref, lam_ref, g_ref, y_ref, tb_ref, m_ref, l_ref, acc_ref,
                 *, lambda_init):
    tq = q_ref.shape[1]
    h = pl.program_id(1)
    qt = pl.program_id(2)

    @pl.when(qt == 0)
    def _():
        ti = lax.broadcasted_iota(jnp.int32, (tq, tq), 0)
        ki = lax.broadcasted_iota(jnp.int32, (tq, tq), 1)
        head = NSA_HEADS + h
        far_bias = tab_ref[REL_BUCKETS - 1, head]
        tb_ref[0] = jnp.where(ti >= ki, _rel_bias(ti - ki, tab_ref, head, far_bias), NEG_INF)
        tb_ref[1] = _rel_bias(ti - ki + tq, tab_ref, head, far_bias)

    lane = lax.broadcasted_iota(jnp.int32, (tq, LANES), 1)
    qb = q_ref[0]
    zero = jnp.zeros_like(qb)
    q2 = jnp.concatenate([jnp.where(lane < HEAD_DIM, qb, zero), jnp.where(lane >= HEAD_DIM, qb, zero)], axis=0)

    def k_tile(ref, kt):
        return ref[0, pl.ds(pl.multiple_of(kt * tq, tq), tq), :]

    def score(kt, bias):
        s = _dot_nt(q2, k_tile(k_ref, kt))
        if bias is None:
            return s
        tb = bias()
        return s + jnp.concatenate([tb, tb], axis=0)

    def tiles(fn):
        def far_body(kt, carry):
            fn(kt, None)
            return carry

        lax.fori_loop(0, jnp.maximum(qt - 1, 0), far_body, 0)
        pl.when(qt >= 1)(lambda: fn(qt - 1, lambda: tb_ref[1]))
        fn(qt, lambda: tb_ref[0])

    a = _two_pass_attention(tiles, score, lambda kt: k_tile(v_ref, kt), m_ref, l_ref, acc_ref)
    lq1, lk1, lq2, lk2 = lam_ref[0:1, :], lam_ref[1:2, :], lam_ref[2:3, :], lam_ref[3:4, :]
    lam = (jnp.exp(jnp.sum(lq1 * lk1, axis=-1, keepdims=True))
           - jnp.exp(jnp.sum(lq2 * lk2, axis=-1, keepdims=True)) + lambda_init)
    o = a[:tq] - lam * a[tq:]
    o = o * lax.rsqrt(jnp.mean(o * o, axis=-1, keepdims=True) + LN_EPS) * g_ref[...]
    y_ref[0] = (o * (1.0 - lambda_init)).astype(y_ref.dtype)


def _diff_attention(tab, proj3, lam4, subln_g, col_blocks, lambda_init):
    B, S, _ = proj3.shape
    tq = min(DIFF_TQ, S)
    q_c, k_c, v_c = col_blocks
    return pl.pallas_call(
        functools.partial(_diff_kernel, lambda_init=lambda_init),
        grid=(B, DIFF_HEADS, S // tq),
        in_specs=[pl.BlockSpec(memory_space=pltpu.SMEM),
                  pl.BlockSpec((1, tq, LANES), lambda b, h, i: (b, i, q_c + h)),
                  pl.BlockSpec((1, S, LANES), lambda b, h, i: (b, 0, k_c + h)),
                  pl.BlockSpec((1, S, LANES), lambda b, h, i: (b, 0, v_c + h)),
                  pl.BlockSpec((SUBLANES, HEAD_DIM), lambda b, h, i: (0, 0)),
                  pl.BlockSpec((1, LANES), lambda b, h, i: (0, 0))],
        out_specs=pl.BlockSpec((1, tq, LANES), lambda b, h, i: (b, i, h)),
        out_shape=jax.ShapeDtypeStruct((B, S, DIFF_HEADS * LANES), BF16),
        scratch_shapes=[pltpu.VMEM((2, tq, tq), F32),
                        pltpu.VMEM((2 * tq, LANES), F32),
                        pltpu.VMEM((2 * tq, LANES), F32),
                        pltpu.VMEM((2 * tq, LANES), F32)],
        compiler_params=_params("arbitrary", "arbitrary", "arbitrary"),
        name="diff",
    )(tab, proj3, proj3, proj3, lam4, subln_g)


def _merge_kernel(x_ref, yn_ref, yd_ref, wgn_ref, wgd_ref, wbn_ref, wbd_ref, wo_ref, g_ref, b_ref, o_ref,
                  *, alpha):
    x = x_ref[...]
    xb = x.astype(BF16)
    merged = (jax.nn.sigmoid(_dot(xb, wgn_ref[...])) * _dot(yn_ref[...], wbn_ref[...])
              + jax.nn.sigmoid(_dot(xb, wgd_ref[...])) * _dot(yd_ref[...], wbd_ref[...]))
    z = alpha * x + _dot(merged.astype(BF16), wo_ref[...])
    o_ref[...] = _layer_norm(z, g_ref[...], b_ref[...])


def _merge(x2d, y_nsa, y_diff, wgn, wgd, wbn, wbd, wo, ln_g, ln_b, alpha):
    T, D = x2d.shape
    const = lambda shape: pl.BlockSpec(shape, lambda i: (0, 0))
    return pl.pallas_call(
        functools.partial(_merge_kernel, alpha=alpha),
        grid=(T // ROW_TILE,),
        in_specs=[pl.BlockSpec((ROW_TILE, D), lambda i: (i, 0)),
                  pl.BlockSpec((ROW_TILE, y_nsa.shape[1]), lambda i: (i, 0)),
                  pl.BlockSpec((ROW_TILE, y_diff.shape[1]), lambda i: (i, 0)),
                  const(wgn.shape), const(wgd.shape), const(wbn.shape), const(wbd.shape), const(wo.shape),
                  const((1, D)), const((1, D))],
        out_specs=pl.BlockSpec((ROW_TILE, D), lambda i: (i, 0)),
        out_shape=jax.ShapeDtypeStruct((T, D), F32),
        compiler_params=_params("arbitrary"),
        name="merge",
    )(x2d, y_nsa, y_diff, wgn, wgd, wbn, wbd, wo, ln_g, ln_b)


def _ffn_kernel(x_ref, halo_ref, p_ref, wg_ref, wu_ref, cw_ref, cb_ref, wd_ref, g_ref, b_ref, wpg_ref, wpp_ref,
                o_ref, acc_ref, *, alpha, tiles_per_seq):
    tm = x_ref.shape[0]
    x = x_ref[...]
    xb = x.astype(BF16)
    keep = (pl.program_id(0) % tiles_per_seq != 0).astype(F32)
    hb = halo_ref[...].astype(BF16)
    row = lax.broadcasted_iota(jnp.int32, (tm, FF_CHUNK), 0)
    acc_ref[...] = jnp.zeros(acc_ref.shape, F32)

    def body(c, carry):
        gm = _dot(xb, wg_ref[c])
        gh = _dot(hb, wg_ref[c]) * keep
        um = _dot(xb, wu_ref[c])
        h1 = gh[HALO - 1:HALO, :]
        h2 = gh[HALO - 2:HALO - 1, :]
        g1 = jnp.where(row == 0, h1, pltpu.roll(gm, 1, 0))
        g2 = jnp.where(row == 0, h2, jnp.where(row == 1, h1, pltpu.roll(gm, 2, 0)))
        cw = cw_ref[c]
        conv = cb_ref[c] + cw[0:1, :] * g2 + cw[1:2, :] * g1 + cw[2:3, :] * gm
        act = (_gelu(conv) * um).astype(BF16)
        acc_ref[...] += _dot(act, wd_ref[c])
        return carry

    lax.fori_loop(0, wg_ref.shape[0], body, 0)
    x2 = _layer_norm(alpha * x + acc_ref[...], g_ref[...], b_ref[...])
    gate = jax.nn.sigmoid(_dot(x2.astype(BF16), wpg_ref[...]))
    o_ref[...] = x2 + gate * _dot(p_ref[...].astype(BF16), wpp_ref[...])


def _ffn(x1, p2d, wg, wu, cw, cb, wd, ln_g, ln_b, wpg, wpp, alpha, seq):
    T, D = x1.shape
    tm = ROW_TILE
    assert seq % tm == 0
    nc = wg.shape[0]
    hb = tm // HALO
    const3 = lambda shape: pl.BlockSpec(shape, lambda i: (0, 0, 0))
    const2 = lambda shape: pl.BlockSpec(shape, lambda i: (0, 0))
    return pl.pallas_call(
        functools.partial(_ffn_kernel, alpha=alpha, tiles_per_seq=seq // tm),
        grid=(T // tm,),
        in_specs=[pl.BlockSpec((tm, D), lambda i: (i, 0)),
                  pl.BlockSpec((HALO, D), lambda i: (jnp.maximum(i * hb - 1, 0), 0)),
                  pl.BlockSpec((tm, p2d.shape[1]), lambda i: (i, 0)),
                  const3(wg.shape), const3(wu.shape), const3(cw.shape), const3(cb.shape), const3(wd.shape),
                  const2((1, D)), const2((1, D)), const2(wpg.shape), const2(wpp.shape)],
        out_specs=pl.BlockSpec((tm, D), lambda i: (i, 0)),
        out_shape=jax.ShapeDtypeStruct((T, D), F32),
        scratch_shapes=[pltpu.VMEM((tm, D), F32)],
        compiler_params=_params("arbitrary"),
        name="ffn",
    )(x1, x1, p2d, wg, wu, cw, cb, wd, ln_g, ln_b, wpg, wpp)


def _slc_from_cmp_t(ncp, n_slc):
    ratio = SLC_BLOCK // CMP_STRIDE
    span = CMP_BLOCK // CMP_STRIDE
    mat = np.zeros((n_slc, ncp), np.float32)
    for j in range(n_slc):
        for m in range(ratio):
            for n in range(span):
                i = ratio * j + m - n
                if 0 <= i < ncp - 1:
                    mat[j, i] += 1.0
    return mat


def _layer(x, p_l, w_in, pe_k, w1_k, w2_k, pe_v, w1_v, w2_v, lq1, lk1, lq2, lk2, subln_g, w_bn, w_bd, w_out,
           ln1_g, ln1_b, w_ffn_in, conv_w, conv_b, w_down, ln2_g, ln2_b, w_pp, w_pg, tab, lambda_init,
           alpha):
    B, S, D = x.shape
    T = B * S
    ncp = S // CMP_STRIDE
    n_slc = S // SLC_BLOCK
    q_w = NSA_HEADS * HEAD_DIM
    kv_w = NSA_GROUPS * HEAD_DIM
    dqk_w = DIFF_HEADS * 2 * HEAD_DIM
    sizes = (q_w,) + (kv_w,) * 6 + (NSA_HEADS * 3, dqk_w, dqk_w, dqk_w, D, D)
    offs = np.concatenate([[0], np.cumsum(sizes)])
    col = lambda i: w_in[:, int(offs[i]):int(offs[i + 1])]
    scale = HEAD_DIM ** -0.5

    n_idx = np.arange(q_w)
    perm = (NSA_REP * ((n_idx % LANES) // HEAD_DIM) + n_idx // LANES) * HEAD_DIM + n_idx % HEAD_DIM
    w_main = jnp.concatenate([col(0)[:, perm] * scale] + [col(i) for i in range(1, 7)]
                             + [col(8) * scale, col(9), col(10)], axis=1).astype(BF16)
    w_gate = jnp.pad(col(7), ((0, 0), (0, LANES - NSA_HEADS * 3))).astype(BF16)
    x2d = x.reshape(T, D)
    proj, gates = _proj(x2d, w_main, w_gate)
    proj3 = proj.reshape(B, S, proj.shape[1])
    gates3 = gates.reshape(B, S, LANES)
    c_kcmp, c_vcmp, c_kslc, c_vslc, c_kwin, c_vwin = (q_w // LANES + i for i in range(6))
    c_dq = q_w // LANES + 6
    c_dk = c_dq + DIFF_HEADS
    c_dv = c_dk + DIFF_HEADS

    kv_cmp = proj3[:, :, c_kcmp * LANES:(c_vcmp + 1) * LANES]
    chunks = kv_cmp.reshape(B, ncp, CMP_STRIDE, 2, NSA_GROUPS, HEAD_DIM).transpose(3, 0, 4, 1, 2, 5)
    chunks = chunks.reshape(2, B, NSA_GROUPS, ncp, CMP_STRIDE * HEAD_DIM)
    half = CMP_STRIDE * HEAD_DIM
    pe = jnp.stack([pe_k, pe_v]).reshape(2, 2, 1, half)
    w1 = jnp.stack([w1_k, w1_v]).reshape(2, 2, half, CMP_HIDDEN).astype(BF16)
    w2 = jnp.stack([w2_k, w2_v])
    w2p = jnp.stack([jnp.pad(w2, ((0, 0), (0, 0), (g * HEAD_DIM, LANES - (g + 1) * HEAD_DIM)))
                     for g in range(NSA_GROUPS)], axis=1).astype(BF16)
    cmp_kv = _compress(chunks, pe, w1, w2p)

    matt = jnp.asarray(_slc_from_cmp_t(ncp, n_slc), BF16)
    ocmp, sel = _cmp_attention(tab, proj3, cmp_kv, gates3, matt, n_slc)
    et_np = np.zeros((S, LANES), np.float32)
    et_np[np.arange(S), np.arange(S) // SLC_BLOCK] = 2.0 ** MASK_EXP
    y_nsa = _nsa_attention(tab, proj3, sel, jnp.asarray(et_np, BF16), gates3, ocmp,
                           (c_kslc, c_vslc, c_kwin, c_vwin))

    lam4 = jnp.pad(jnp.stack([lq1, lk1, lq2, lk2]), ((0, SUBLANES - 4), (0, 0)))
    y_diff = _diff_attention(tab, proj3, lam4, subln_g.reshape(1, LANES), (c_dq, c_dk, c_dv), lambda_init)

    x1 = _merge(x2d, y_nsa.reshape(T, q_w), y_diff.reshape(T, dqk_w),
                col(11).astype(BF16), col(12).astype(BF16), w_bn[perm].astype(BF16), w_bd.astype(BF16),
                w_out.astype(BF16), ln1_g.reshape(1, D), ln1_b.reshape(1, D), alpha)

    nc = D_FF // FF_CHUNK
    chunked = lambda w: w.reshape(w.shape[0], nc, FF_CHUNK).transpose(1, 0, 2)
    wg = chunked(w_ffn_in[:, :D_FF]).astype(BF16)
    wu = chunked(w_ffn_in[:, D_FF:]).astype(BF16)
    cw = chunked(jnp.pad(conv_w, ((0, SUBLANES - CONV_WIDTH), (0, 0))))
    cb = chunked(conv_b.reshape(1, D_FF))
    wd = w_down.reshape(nc, FF_CHUNK, D).astype(BF16)
    out = _ffn(x1, p_l.reshape(T, p_l.shape[-1]), wg, wu, cw, cb, wd, ln2_g.reshape(1, D), ln2_b.reshape(1, D),
               w_pg.astype(BF16), w_pp.astype(BF16), alpha, S)
    return out.reshape(B, S, D)


def kernel(x, p, w_in, nsa_cmp_pe_k, nsa_cmp_w1_k, nsa_cmp_w2_k, nsa_cmp_pe_v, nsa_cmp_w1_v, nsa_cmp_w2_v, diff_lambda_q1, diff_lambda_k1, diff_lambda_q2, diff_lambda_k2, diff_subln_g, w_branch_nsa, w_branch_diff, w_out, ln1_g, ln1_b, w_ffn_in, ffn_conv_w, ffn_conv_b, w_ffn_down, ln2_g, ln2_b, w_ple_proj, w_ple_gate, rel_bias_table):
    depth = w_in.shape[0]
    alpha = (2.0 * depth) ** 0.25
    for l in range(depth):
        lambda_init = 0.8 - 0.6 * math.exp(-0.3 * l)
        x = _layer(x, p[l], w_in[l], nsa_cmp_pe_k[l], nsa_cmp_w1_k[l], nsa_cmp_w2_k[l], nsa_cmp_pe_v[l],
                   nsa_cmp_w1_v[l], nsa_cmp_w2_v[l], diff_lambda_q1[l], diff_lambda_k1[l], diff_lambda_q2[l],
                   diff_lambda_k2[l], diff_subln_g[l], w_branch_nsa[l], w_branch_diff[l], w_out[l], ln1_g[l],
                   ln1_b[l], w_ffn_in[l], ffn_conv_w[l], ffn_conv_b[l], w_ffn_down[l], ln2_g[l], ln2_b[l],
                   w_ple_proj[l], w_ple_gate[l], rel_bias_table, lambda_init, alpha)
    return x
```

```python
import functools
import math

import jax
import jax.numpy as jnp
import numpy as np
from jax import lax
from jax.experimental import pallas as pl
from jax.experimental.pallas import tpu as pltpu

F32 = jnp.float32
BF16 = jnp.bfloat16

NSA_HEADS = 8
NSA_GROUPS = 2
NSA_REP = NSA_HEADS // NSA_GROUPS
HEAD_DIM = 64
CMP_BLOCK = 32
CMP_STRIDE = 16
CMP_HIDDEN = 256
SLC_BLOCK = 64
SLC_TOPK = 16
SLC_LOCAL = 2
WINDOW = 512
DIFF_HEADS = 4
REL_BUCKETS = 32
REL_MAX_EXACT = 16
REL_MAX_DIST = 128
D_FF = 2816
CONV_WIDTH = 3
LN_EPS = 1e-5
NEG_INF = -1e30
BIG = 1e30
MASK_EXP = 100

LANES = 128
SUBLANES = 8
VMEM_LIMIT = 56 * 1024 * 1024

CMP_TQ = 128
NSA_TQ = 256
DIFF_TQ = 256
FAR_TILES = 4
ROW_TILE = 512
FF_CHUNK = 256
HALO = 16


def _rel_breakpoints():
    n = np.arange(0, 4 * REL_MAX_DIST)
    large = REL_MAX_EXACT + (np.log(np.maximum(n, 1).astype(np.float32) / REL_MAX_EXACT)
                             / np.float32(math.log(REL_MAX_DIST / REL_MAX_EXACT))
                             * (REL_BUCKETS - REL_MAX_EXACT)).astype(np.int32)
    bucket = np.where(n < REL_MAX_EXACT, n, np.minimum(large, REL_BUCKETS - 1))
    assert np.all(np.diff(bucket) >= 0)
    return [int(np.argmax(bucket >= b)) for b in range(1, REL_BUCKETS)]


REL_BREAKS = _rel_breakpoints()


def _dot(a, b):
    return jnp.dot(a, b, preferred_element_type=F32)


def _dot_nt(a, b):
    return lax.dot_general(a, b, (((1,), (1,)), ((), ())), preferred_element_type=F32)


def _rel_bias(dist, tab_ref, head, shift):
    val = jnp.full(dist.shape, tab_ref[0, head] - shift, F32)
    for b, brk in enumerate(REL_BREAKS, start=1):
        val = jnp.where(dist >= brk, tab_ref[b, head] - shift, val)
    return val


def _gelu(x):
    c = math.sqrt(2.0 / math.pi)
    return 0.5 * x * (1.0 + jnp.tanh(c * (x + 0.044715 * (x * x * x))))


def _layer_norm(z, g, b):
    mu = jnp.mean(z, axis=-1, keepdims=True)
    zc = z - mu
    var = jnp.mean(zc * zc, axis=-1, keepdims=True)
    return zc * lax.rsqrt(var + LN_EPS) * g + b


def _params(*sem):
    return pltpu.CompilerParams(dimension_semantics=sem, vmem_limit_bytes=VMEM_LIMIT)


def _proj_kernel(x_ref, wm_ref, wg_ref, om_ref, og_ref):
    xb = x_ref[...].astype(BF16)
    n = wm_ref.shape[1]
    for c in range(0, n, 2 * LANES):
        w = min(2 * LANES, n - c)
        om_ref[:, c:c + w] = _dot(xb, wm_ref[:, c:c + w]).astype(om_ref.dtype)
    og_ref[...] = _dot(xb, wg_ref[...])


def _proj(x2d, w_main, w_gate):
    T, D = x2d.shape
    n = w_main.shape[1]
    return pl.pallas_call(
        _proj_kernel,
        grid=(T // ROW_TILE,),
        in_specs=[pl.BlockSpec((ROW_TILE, D), lambda i: (i, 0)),
                  pl.BlockSpec((D, n), lambda i: (0, 0)),
                  pl.BlockSpec((D, LANES), lambda i: (0, 0))],
        out_specs=[pl.BlockSpec((ROW_TILE, n), lambda i: (i, 0)),
                   pl.BlockSpec((ROW_TILE, LANES), lambda i: (i, 0))],
        out_shape=[jax.ShapeDtypeStruct((T, n), BF16), jax.ShapeDtypeStruct((T, LANES), F32)],
        compiler_params=_params("arbitrary"),
        name="proj",
    )(x2d, w_main, w_gate)


def _compress_kernel(c_ref, pe_ref, w1_ref, w2_ref, o_ref):
    ncp = c_ref.shape[3]
    acc = jnp.zeros((ncp, LANES), F32)
    for g in range(NSA_GROUPS):
        ch = c_ref[0, 0, g].astype(F32)
        a = _dot((ch + pe_ref[0, 0]).astype(BF16), w1_ref[0, 0])
        b = _dot((ch + pe_ref[0, 1]).astype(BF16), w1_ref[0, 1])
        h = a + pltpu.roll(b, ncp - 1, 0)
        acc = acc + _dot(_gelu(h).astype(BF16), w2_ref[0, g])
    o_ref[0, 0] = acc.astype(o_ref.dtype)


def _compress(chunks, pe, w1, w2):
    _, B, G, ncp, cw = chunks.shape
    return pl.pallas_call(
        _compress_kernel,
        grid=(2, B),
        in_specs=[pl.BlockSpec((1, 1, G, ncp, cw), lambda s, b: (s, b, 0, 0, 0)),
                  pl.BlockSpec((1, 2, 1, cw), lambda s, b: (s, 0, 0, 0)),
                  pl.BlockSpec((1, 2, cw, CMP_HIDDEN), lambda s, b: (s, 0, 0, 0)),
                  pl.BlockSpec((1, G, CMP_HIDDEN, LANES), lambda s, b: (s, 0, 0, 0))],
        out_specs=pl.BlockSpec((1, 1, ncp, LANES), lambda s, b: (s, b, 0, 0)),
        out_shape=jax.ShapeDtypeStruct((2, B, ncp, LANES), BF16),
        compiler_params=_params("arbitrary", "arbitrary"),
        name="compress",
    )(chunks, pe, w1, w2)


def _cmp_kernel(tab_ref, q_ref, kc_ref, vc_ref, gate_ref, matt_ref, ocmp_ref, sel_ref, bias_ref, *, n_slc):
    tq = q_ref.shape[1]
    ncp = kc_ref.shape[2]
    q0 = pl.program_id(0) * tq
    lane = lax.broadcasted_iota(jnp.int32, (tq, LANES), 1)
    t_idx = q0 + lax.broadcasted_iota(jnp.int32, (tq, ncp), 0)
    c_idx = lax.broadcasted_iota(jnp.int32, (tq, ncp), 1)
    dist = t_idx - (c_idx * CMP_STRIDE + (CMP_BLOCK - 1))
    valid = dist >= 0

    @pl.when(pl.program_id(1) == 0)
    def _():
        for head in range(NSA_HEADS):
            bias_ref[head] = jnp.where(valid, _rel_bias(dist, tab_ref, head, 0.0), NEG_INF)

    gates = jax.nn.sigmoid(gate_ref[0])
    eye = (lax.broadcasted_iota(jnp.int32, (tq, tq), 0)
           == lax.broadcasted_iota(jnp.int32, (tq, tq), 1)).astype(BF16)
    kc = kc_ref[0, 0]
    vc = vc_ref[0, 0]
    jrow_i = lax.broadcasted_iota(jnp.int32, (n_slc, tq), 0)
    cur = jnp.right_shift(q0 + lax.broadcasted_iota(jnp.int32, (n_slc, tq), 1), int(math.log2(SLC_BLOCK)))
    forced = (jrow_i == 0) | ((cur - jrow_i >= 0) & (cur - jrow_i < SLC_LOCAL))
    blk_valid = jrow_i <= cur
    jrow = jrow_i.astype(F32)
    gated = []
    for g in range(NSA_GROUPS):
        lane_g = (lane >= HEAD_DIM * g) & (lane < HEAD_DIM * (g + 1))
        psum = jnp.zeros((tq, ncp), F32)
        outs = []
        for r in range(NSA_REP):
            head = g * NSA_REP + r
            qb = q_ref[0, :, r * LANES:(r + 1) * LANES]
            qb = jnp.where(lane_g, qb, jnp.zeros_like(qb))
            logit = _dot_nt(qb, kc) + bias_ref[head]
            m = jnp.max(logit, axis=-1, keepdims=True)
            e = jnp.exp(logit - m)
            p = jnp.where(valid, e / jnp.sum(e, axis=-1, keepdims=True), 0.0)
            psum = psum + p
            o = _dot(p.astype(BF16), vc)
            outs.append(gates[:, head * 3:head * 3 + 1] * o)
        gated.append(outs)
        hi = psum.astype(BF16)
        lo = (psum - hi.astype(F32)).astype(BF16)
        p_slc = _dot_nt(matt_ref[...], hi) + _dot_nt(matt_ref[...], lo)
        score = jnp.where(forced, BIG, jnp.where(blk_valid, p_slc, NEG_INF))
        sel = jnp.zeros((n_slc, tq), F32)
        for _ in range(min(SLC_TOPK, n_slc)):
            mx = jnp.max(score, axis=0, keepdims=True)
            idx = jnp.min(jnp.where(score == mx, jrow, float(n_slc)), axis=0, keepdims=True)
            hit = jrow == idx
            sel = jnp.where(hit, 1.0, sel)
            score = jnp.where(hit, -3.0e38, score)
        selm1 = (sel - 1.0).astype(BF16)
        if n_slc < LANES:
            selm1 = jnp.concatenate([selm1, jnp.zeros((LANES - n_slc, tq), BF16)], axis=0)
        sel_ref[0, g] = _dot_nt(eye, selm1).astype(sel_ref.dtype)
    for r in range(NSA_REP):
        ocmp_ref[0, :, r * LANES:(r + 1) * LANES] = jnp.where(lane < HEAD_DIM, gated[0][r], gated[1][r])


def _cmp_attention(tab, proj3, cmp_kv, gates3, matt, n_slc):
    B, S, _ = proj3.shape
    ncp = cmp_kv.shape[2]
    tq = CMP_TQ
    return pl.pallas_call(
        functools.partial(_cmp_kernel, n_slc=n_slc),
        grid=(S // tq, B),
        in_specs=[pl.BlockSpec(memory_space=pltpu.SMEM),
                  pl.BlockSpec((1, tq, 4 * LANES), lambda i, b: (b, i, 0)),
                  pl.BlockSpec((1, 1, ncp, LANES), lambda i, b: (0, b, 0, 0)),
                  pl.BlockSpec((1, 1, ncp, LANES), lambda i, b: (1, b, 0, 0)),
                  pl.BlockSpec((1, tq, LANES), lambda i, b: (b, i, 0)),
                  pl.BlockSpec((n_slc, ncp), lambda i, b: (0, 0))],
        out_specs=[pl.BlockSpec((1, tq, 4 * LANES), lambda i, b: (b, i, 0)),
                   pl.BlockSpec((1, NSA_GROUPS, tq, LANES), lambda i, b: (b, 0, i, 0))],
        out_shape=[jax.ShapeDtypeStruct((B, S, 4 * LANES), F32),
                   jax.ShapeDtypeStruct((B, NSA_GROUPS, S, LANES), BF16)],
        scratch_shapes=[pltpu.VMEM((NSA_HEADS, tq, ncp), F32)],
        compiler_params=_params("arbitrary", "arbitrary"),
        name="cmp",
    )(tab, proj3, cmp_kv, cmp_kv, gates3, matt)


def _two_pass_attention(segments, score, value, m_ref, l_ref, acc_ref):
    m_ref[...] = jnp.full(m_ref.shape, NEG_INF, F32)

    def max_step(k0, width, bias):
        s = score(k0, width, bias)
        mx = s[:, :LANES]
        for c in range(LANES, width, LANES):
            mx = jnp.maximum(mx, s[:, c:c + LANES])
        m_ref[...] = jnp.maximum(m_ref[...], mx)

    segments(max_step)
    m_ref[...] = jnp.broadcast_to(jnp.max(m_ref[...], axis=-1, keepdims=True), m_ref.shape)
    l_ref[...] = jnp.zeros(l_ref.shape, F32)
    acc_ref[...] = jnp.zeros(acc_ref.shape, F32)

    def sum_step(k0, width, bias):
        s = score(k0, width, bias)
        m = m_ref[...]
        ps = [jnp.exp(s[:, c:c + LANES] - m) for c in range(0, width, LANES)]
        l_ref[...] += functools.reduce(lambda a, b: a + b, ps)
        acc_ref[...] += _dot(jnp.concatenate([p.astype(BF16) for p in ps], axis=1), value(k0, width))

    segments(sum_step)
    return acc_ref[...] / jnp.sum(l_ref[...], axis=-1, keepdims=True)


def _causal_segments(fn, qt, tq, near_bias):
    n_far = jnp.maximum(qt - 1, 0)
    step = FAR_TILES * tq

    def far_body(i, carry):
        fn(i * step, step, None)
        return carry

    lax.fori_loop(0, lax.div(n_far, FAR_TILES), far_body, 0)
    for rem in range(1, FAR_TILES):
        pl.when(lax.rem(n_far, FAR_TILES) == rem)(
            functools.partial(fn, (n_far - rem) * tq, rem * tq, None))
    pl.when(qt >= 1)(functools.partial(fn, (qt - 1) * tq, 2 * tq, near_bias(0, 2 * tq)))
    pl.when(qt == 0)(functools.partial(fn, 0, tq, near_bias(tq, 2 * tq)))


def _key_slice(k0, width, tq):
    return pl.ds(k0 if isinstance(k0, int) else pl.multiple_of(k0, tq), width)


def _rows(ref, k0, width, tq):
    return ref[0, _key_slice(k0, width, tq), :]


def _nsa_kernel(tab_ref, q_ref, ks_ref, vs_ref, kw_ref, vw_ref, sel_ref, et_ref, gate_ref, ocmp_ref,
                y_ref, tb_ref, m_ref, l_ref, acc_ref, part_ref):
    tq = q_ref.shape[1]
    n_win = WINDOW // tq
    qt = pl.program_id(1)

    @pl.when((pl.program_id(0) == 0) & (qt == 0))
    def _():
        ti = lax.broadcasted_iota(jnp.int32, (tq, tq), 0)
        ki = lax.broadcasted_iota(jnp.int32, (tq, tq), 1)
        for g in range(NSA_GROUPS):
            for r in range(NSA_REP):
                head = g * NSA_REP + r
                far_bias = tab_ref[REL_BUCKETS - 1, head]
                rows = slice(r * tq, (r + 1) * tq)
                tb_ref[g, rows, 0:tq] = jnp.where(ti < ki, 0.0, NEG_INF)
                for j in range(2, n_win):
                    tb_ref[g, rows, (n_win - j) * tq:(n_win - j + 1) * tq] = jnp.zeros((tq, tq), F32)
                tb_ref[g, rows, (n_win - 1) * tq:n_win * tq] = _rel_bias(ti - ki + tq, tab_ref, head, far_bias)
                tb_ref[g, rows, n_win * tq:(n_win + 1) * tq] = jnp.where(
                    ti >= ki, _rel_bias(ti - ki, tab_ref, head, far_bias), NEG_INF)

    lane = lax.broadcasted_iota(jnp.int32, (tq, LANES), 1)
    gates = jax.nn.sigmoid(gate_ref[0])

    for g in range(NSA_GROUPS):
        lane_g = (lane >= HEAD_DIM * g) & (lane < HEAD_DIM * (g + 1))
        q_parts = []
        for r in range(NSA_REP):
            qb = q_ref[0, :, r * LANES:(r + 1) * LANES]
            q_parts.append(jnp.where(lane_g, qb, jnp.zeros_like(qb)))
        q_g = jnp.concatenate(q_parts, axis=0)
        sel_g = jnp.concatenate([sel_ref[0, g]] * NSA_REP, axis=0)
        qs_g = jnp.concatenate([q_g, sel_g], axis=1)

        def bias_cols(lo, hi):
            return lambda: tb_ref[g, :, lo:hi]

        def slc_score(k0, width, bias):
            et = et_ref[_key_slice(k0, width, tq), :]
            s = _dot_nt(qs_g, jnp.concatenate([_rows(ks_ref, k0, width, tq), et], axis=1))
            return s if bias is None else s + bias()

        def slc_segments(fn):
            _causal_segments(fn, qt, tq, lambda lo, hi: bias_cols((n_win - 1) * tq + lo, (n_win - 1) * tq + hi))

        part_ref[...] = _two_pass_attention(slc_segments, slc_score,
                                            lambda k0, width: _rows(vs_ref, k0, width, tq), m_ref, l_ref, acc_ref)

        def win_score(k0, width, bias):
            return _dot_nt(q_g, _rows(kw_ref, k0, width, tq)) + bias()

        def win_segments(fn):
            for n in range(n_win):
                pl.when(qt == n)(functools.partial(fn, 0, (n + 1) * tq, bias_cols((n_win - n) * tq, (n_win + 1) * tq)))
            pl.when(qt >= n_win)(functools.partial(fn, (qt - n_win) * tq, (n_win + 1) * tq,
                                                   bias_cols(0, (n_win + 1) * tq)))

        o_win = _two_pass_attention(win_segments, win_score,
                                    lambda k0, width: _rows(vw_ref, k0, width, tq), m_ref, l_ref, acc_ref)
        o_slc = part_ref[...]

        for r in range(NSA_REP):
            head = g * NSA_REP + r
            rows = slice(r * tq, (r + 1) * tq)
            y = (gates[:, head * 3 + 1:head * 3 + 2] * o_slc[rows]
                 + gates[:, head * 3 + 2:head * 3 + 3] * o_win[rows])
            cols = slice(r * LANES, (r + 1) * LANES)
            if g == 0:
                y_ref[0, :, cols] = (ocmp_ref[0, :, cols] + y).astype(y_ref.dtype)
            else:
                y0 = y_ref[0, :, cols]
                y1 = (ocmp_ref[0, :, cols] + y).astype(y_ref.dtype)
                y_ref[0, :, cols] = jnp.where(lane < HEAD_DIM, y0, y1)


def _nsa_attention(tab, proj3, sel, et, gates3, ocmp, col_blocks):
    B, S, _ = proj3.shape
    tq = NSA_TQ
    assert WINDOW % tq == 0 and WINDOW // tq >= 2 and S % tq == 0
    n_win = WINDOW // tq
    ks_c, vs_c, kw_c, vw_c = col_blocks
    rows = NSA_REP * tq

    def kv_spec(c):
        return pl.BlockSpec((1, S, LANES), lambda b, i: (b, 0, c))

    return pl.pallas_call(
        _nsa_kernel,
        grid=(B, S // tq),
        in_specs=[pl.BlockSpec(memory_space=pltpu.SMEM),
                  pl.BlockSpec((1, tq, 4 * LANES), lambda b, i: (b, i, 0)),
                  kv_spec(ks_c), kv_spec(vs_c), kv_spec(kw_c), kv_spec(vw_c),
                  pl.BlockSpec((1, NSA_GROUPS, tq, LANES), lambda b, i: (b, 0, i, 0)),
                  pl.BlockSpec((S, LANES), lambda b, i: (0, 0)),
                  pl.BlockSpec((1, tq, LANES), lambda b, i: (b, i, 0)),
                  pl.BlockSpec((1, tq, 4 * LANES), lambda b, i: (b, i, 0))],
        out_specs=pl.BlockSpec((1, tq, 4 * LANES), lambda b, i: (b, i, 0)),
        out_shape=jax.ShapeDtypeStruct((B, S, 4 * LANES), BF16),
        scratch_shapes=[pltpu.VMEM((NSA_GROUPS, rows, (n_win + 1) * tq), F32),
                        pltpu.VMEM((rows, LANES), F32),
                        pltpu.VMEM((rows, LANES), F32),
                        pltpu.VMEM((rows, LANES), F32),
                        pltpu.VMEM((rows, LANES), F32)],
        compiler_params=_params("arbitrary", "arbitrary"),
        name="nsa",
    )(tab, proj3, proj3, proj3, proj3, proj3, sel, et, gates3, ocmp)


def _diff_kernel(tab_ref, q_ref, k_ref, v_ref, lam_ref, g_ref, y_ref, tb_ref, m_ref, l_ref, acc_ref,
                 *, lambda_init):
    tq = q_ref.shape[1]
    h = pl.program_id(1)
    qt = pl.program_id(2)

    @pl.when(qt == 0)
    def _():
        ti = lax.broadcasted_iota(jnp.int32, (tq, tq), 0)
        ki = lax.broadcasted_iota(jnp.int32, (tq, tq), 1)
        head = NSA_HEADS + h
        far_bias = tab_ref[REL_BUCKETS - 1, head]
        tb_ref[:, 0:tq] = _rel_bias(ti - ki + tq, tab_ref, head, far_bias)
        tb_ref[:, tq:2 * tq] = jnp.where(ti >= ki, _rel_bias(ti - ki, tab_ref, head, far_bias), NEG_INF)

    lane = lax.broadcasted_iota(jnp.int32, (tq, LANES), 1)
    qb = q_ref[0]
    zero = jnp.zeros_like(qb)
    q2 = jnp.concatenate([jnp.where(lane < HEAD_DIM, qb, zero), jnp.where(lane >= HEAD_DIM, qb, zero)], axis=0)

    def score(k0, width, bias):
        s = _dot_nt(q2, _rows(k_ref, k0, width, tq))
        if bias is None:
            return s
        tb = bias()
        return s + jnp.concatenate([tb, tb], axis=0)

    def segments(fn):
        _causal_segments(fn, qt, tq, lambda lo, hi: (lambda: tb_ref[:, lo:hi]))

    a = _two_pass_attention(segments, score, lambda k0, width: _rows(v_ref, k0, width, tq), m_ref, l_ref, acc_ref)
    lq1, lk1, lq2, lk2 = lam_ref[0:1, :], lam_ref[1:2, :], lam_ref[2:3, :], lam_ref[3:4, :]
    lam = (jnp.exp(jnp.sum(lq1 * lk1, axis=-1, keepdims=True))
           - jnp.exp(jnp.sum(lq2 * lk2, axis=-1, keepdims=True)) + lambda_init)
    o = a[:tq] - lam * a[tq:]
    o = o * lax.rsqrt(jnp.mean(o * o, axis=-1, keepdims=True) + LN_EPS) * g_ref[...]
    y_ref[0] = (o * (1.0 - lambda_init)).astype(y_ref.dtype)


def _diff_attention(tab, proj3, lam4, subln_g, col_blocks, lambda_init):
    B, S, _ = proj3.shape
    tq = min(DIFF_TQ, S)
    q_c, k_c, v_c = col_blocks
    return pl.pallas_call(
        functools.partial(_diff_kernel, lambda_init=lambda_init),
        grid=(B, DIFF_HEADS, S // tq),
        in_specs=[pl.BlockSpec(memory_space=pltpu.SMEM),
                  pl.BlockSpec((1, tq, LANES), lambda b, h, i: (b, i, q_c + h)),
                  pl.BlockSpec((1, S, LANES), lambda b, h, i: (b, 0, k_c + h)),
                  pl.BlockSpec((1, S, LANES), lambda b, h, i: (b, 0, v_c + h)),
                  pl.BlockSpec((SUBLANES, HEAD_DIM), lambda b, h, i: (0, 0)),
                  pl.BlockSpec((1, LANES), lambda b, h, i: (0, 0))],
        out_specs=pl.BlockSpec((1, tq, LANES), lambda b, h, i: (b, i, h)),
        out_shape=jax.ShapeDtypeStruct((B, S, DIFF_HEADS * LANES), BF16),
        scratch_shapes=[pltpu.VMEM((tq, 2 * tq), F32),
                        pltpu.VMEM((2 * tq, LANES), F32),
                        pltpu.VMEM((2 * tq, LANES), F32),
                        pltpu.VMEM((2 * tq, LANES), F32)],
        compiler_params=_params("arbitrary", "arbitrary", "arbitrary"),
        name="diff",
    )(tab, proj3, proj3, proj3, lam4, subln_g)


def _merge_kernel(x_ref, yn_ref, yd_ref, wgn_ref, wgd_ref, wbn_ref, wbd_ref, wo_ref, g_ref, b_ref, o_ref,
                  *, alpha):
    x = x_ref[...]
    xb = x.astype(BF16)
    merged = (jax.nn.sigmoid(_dot(xb, wgn_ref[...])) * _dot(yn_ref[...], wbn_ref[...])
              + jax.nn.sigmoid(_dot(xb, wgd_ref[...])) * _dot(yd_ref[...], wbd_ref[...]))
    z = alpha * x + _dot(merged.astype(BF16), wo_ref[...])
    o_ref[...] = _layer_norm(z, g_ref[...], b_ref[...])


def _merge(x2d, y_nsa, y_diff, wgn, wgd, wbn, wbd, wo, ln_g, ln_b, alpha):
    T, D = x2d.shape
    const = lambda shape: pl.BlockSpec(shape, lambda i: (0, 0))
    return pl.pallas_call(
        functools.partial(_merge_kernel, alpha=alpha),
        grid=(T // ROW_TILE,),
        in_specs=[pl.BlockSpec((ROW_TILE, D), lambda i: (i, 0)),
                  pl.BlockSpec((ROW_TILE, y_nsa.shape[1]), lambda i: (i, 0)),
                  pl.BlockSpec((ROW_TILE, y_diff.shape[1]), lambda i: (i, 0)),
                  const(wgn.shape), const(wgd.shape), const(wbn.shape), const(wbd.shape), const(wo.shape),
                  const((1, D)), const((1, D))],
        out_specs=pl.BlockSpec((ROW_TILE, D), lambda i: (i, 0)),
        out_shape=jax.ShapeDtypeStruct((T, D), F32),
        compiler_params=_params("arbitrary"),
        name="merge",
    )(x2d, y_nsa, y_diff, wgn, wgd, wbn, wbd, wo, ln_g, ln_b)


def _ffn_kernel(x_ref, halo_ref, p_ref, w_ref, cw_ref, cb_ref, wd_ref, g_ref, b_ref, wpg_ref, wpp_ref,
                o_ref, acc_ref, *, alpha, tiles_per_seq):
    tm = x_ref.shape[0]
    d_ff = wd_ref.shape[0]
    x = x_ref[...]
    xb = x.astype(BF16)
    keep = (pl.program_id(0) % tiles_per_seq != 0).astype(F32)
    hb = halo_ref[...].astype(BF16)
    row = lax.broadcasted_iota(jnp.int32, (tm, FF_CHUNK), 0)
    for c in range(0, d_ff, FF_CHUNK):
        cols = slice(c, c + FF_CHUNK)
        wg = w_ref[:, cols]
        gm = _dot(xb, wg)
        gh = _dot(hb, wg) * keep
        um = _dot(xb, w_ref[:, d_ff + c:d_ff + c + FF_CHUNK])
        h1 = gh[HALO - 1:HALO, :]
        h2 = gh[HALO - 2:HALO - 1, :]
        g1 = jnp.where(row == 0, h1, pltpu.roll(gm, 1, 0))
        g2 = jnp.where(row == 0, h2, jnp.where(row == 1, h1, pltpu.roll(gm, 2, 0)))
        conv = cb_ref[:, cols] + cw_ref[0:1, cols] * g2 + cw_ref[1:2, cols] * g1 + cw_ref[2:3, cols] * gm
        act = (_gelu(conv) * um).astype(BF16)
        down = _dot(act, wd_ref[cols, :])
        if c == 0:
            acc_ref[...] = down
        else:
            acc_ref[...] += down
    x2 = _layer_norm(alpha * x + acc_ref[...], g_ref[...], b_ref[...])
    gate = jax.nn.sigmoid(_dot(x2.astype(BF16), wpg_ref[...]))
    o_ref[...] = x2 + gate * _dot(p_ref[...].astype(BF16), wpp_ref[...])


def _ffn(x1, p2d, w_in, cw, cb, wd, ln_g, ln_b, wpg, wpp, alpha, seq):
    T, D = x1.shape
    tm = ROW_TILE
    assert seq % tm == 0 and wd.shape[0] % FF_CHUNK == 0
    hb = tm // HALO
    const = lambda shape: pl.BlockSpec(shape, lambda i: (0, 0))
    return pl.pallas_call(
        functools.partial(_ffn_kernel, alpha=alpha, tiles_per_seq=seq // tm),
        grid=(T // tm,),
        in_specs=[pl.BlockSpec((tm, D), lambda i: (i, 0)),
                  pl.BlockSpec((HALO, D), lambda i: (jnp.maximum(i * hb - 1, 0), 0)),
                  pl.BlockSpec((tm, p2d.shape[1]), lambda i: (i, 0)),
                  const(w_in.shape), const(cw.shape), const(cb.shape), const(wd.shape),
                  const((1, D)), const((1, D)), const(wpg.shape), const(wpp.shape)],
        out_specs=pl.BlockSpec((tm, D), lambda i: (i, 0)),
        out_shape=jax.ShapeDtypeStruct((T, D), F32),
        scratch_shapes=[pltpu.VMEM((tm, D), F32)],
        compiler_params=_params("arbitrary"),
        name="ffn",
    )(x1, x1, p2d, w_in, cw, cb, wd, ln_g, ln_b, wpg, wpp)


def _slc_from_cmp_t(ncp, n_slc):
    ratio = SLC_BLOCK // CMP_STRIDE
    span = CMP_BLOCK // CMP_STRIDE
    mat = np.zeros((n_slc, ncp), np.float32)
    for j in range(n_slc):
        for m in range(ratio):
            for n in range(span):
                i = ratio * j + m - n
                if 0 <= i < ncp - 1:
                    mat[j, i] += 1.0
    return mat


def _layer(x, p_l, w_in, pe_k, w1_k, w2_k, pe_v, w1_v, w2_v, lq1, lk1, lq2, lk2, subln_g, w_bn, w_bd, w_out,
           ln1_g, ln1_b, w_ffn_in, conv_w, conv_b, w_down, ln2_g, ln2_b, w_pp, w_pg, tab, lambda_init,
           alpha):
    B, S, D = x.shape
    T = B * S
    ncp = S // CMP_STRIDE
    n_slc = S // SLC_BLOCK
    q_w = NSA_HEADS * HEAD_DIM
    kv_w = NSA_GROUPS * HEAD_DIM
    dqk_w = DIFF_HEADS * 2 * HEAD_DIM
    sizes = (q_w,) + (kv_w,) * 6 + (NSA_HEADS * 3, dqk_w, dqk_w, dqk_w, D, D)
    offs = np.concatenate([[0], np.cumsum(sizes)])
    col = lambda i: w_in[:, int(offs[i]):int(offs[i + 1])]
    scale = HEAD_DIM ** -0.5

    n_idx = np.arange(q_w)
    perm = (NSA_REP * ((n_idx % LANES) // HEAD_DIM) + n_idx // LANES) * HEAD_DIM + n_idx % HEAD_DIM
    w_main = jnp.concatenate([col(0)[:, perm] * scale] + [col(i) for i in range(1, 7)]
                             + [col(8) * scale, col(9), col(10)], axis=1).astype(BF16)
    w_gate = jnp.pad(col(7), ((0, 0), (0, LANES - NSA_HEADS * 3))).astype(BF16)
    x2d = x.reshape(T, D)
    proj, gates = _proj(x2d, w_main, w_gate)
    proj3 = proj.reshape(B, S, proj.shape[1])
    gates3 = gates.reshape(B, S, LANES)
    c_kcmp, c_vcmp, c_kslc, c_vslc, c_kwin, c_vwin = (q_w // LANES + i for i in range(6))
    c_dq = q_w // LANES + 6
    c_dk = c_dq + DIFF_HEADS
    c_dv = c_dk + DIFF_HEADS

    kv_cmp = proj3[:, :, c_kcmp * LANES:(c_vcmp + 1) * LANES]
    chunks = kv_cmp.reshape(B, ncp, CMP_STRIDE, 2, NSA_GROUPS, HEAD_DIM).transpose(3, 0, 4, 1, 2, 5)
    chunks = chunks.reshape(2, B, NSA_GROUPS, ncp, CMP_STRIDE * HEAD_DIM)
    half = CMP_STRIDE * HEAD_DIM
    pe = jnp.stack([pe_k, pe_v]).reshape(2, 2, 1, half)
    w1 = jnp.stack([w1_k, w1_v]).reshape(2, 2, half, CMP_HIDDEN).astype(BF16)
    w2 = jnp.stack([w2_k, w2_v])
    w2p = jnp.stack([jnp.pad(w2, ((0, 0), (0, 0), (g * HEAD_DIM, LANES - (g + 1) * HEAD_DIM)))
                     for g in range(NSA_GROUPS)], axis=1).astype(BF16)
    cmp_kv = _compress(chunks, pe, w1, w2p)

    matt = jnp.asarray(_slc_from_cmp_t(ncp, n_slc), BF16)
    ocmp, sel = _cmp_attention(tab, proj3, cmp_kv, gates3, matt, n_slc)
    et_np = np.zeros((S, LANES), np.float32)
    et_np[np.arange(S), np.arange(S) // SLC_BLOCK] = 2.0 ** MASK_EXP
    y_nsa = _nsa_attention(tab, proj3, sel, jnp.asarray(et_np, BF16), gates3, ocmp,
                           (c_kslc, c_vslc, c_kwin, c_vwin))

    lam4 = jnp.pad(jnp.stack([lq1, lk1, lq2, lk2]), ((0, SUBLANES - 4), (0, 0)))
    y_diff = _diff_attention(tab, proj3, lam4, subln_g.reshape(1, LANES), (c_dq, c_dk, c_dv), lambda_init)

    x1 = _merge(x2d, y_nsa.reshape(T, q_w), y_diff.reshape(T, dqk_w),
                col(11).astype(BF16), col(12).astype(BF16), w_bn[perm].astype(BF16), w_bd.astype(BF16),
                w_out.astype(BF16), ln1_g.reshape(1, D), ln1_b.reshape(1, D), alpha)

    cw = jnp.pad(conv_w, ((0, SUBLANES - CONV_WIDTH), (0, 0)))
    out = _ffn(x1, p_l.reshape(T, p_l.shape[-1]), w_ffn_in.astype(BF16), cw, conv_b.reshape(1, -1),
               w_down.astype(BF16), ln2_g.reshape(1, D), ln2_b.reshape(1, D),
               w_pg.astype(BF16), w_pp.astype(BF16), alpha, S)
    return out.reshape(B, S, D)


def kernel(x, p, w_in, nsa_cmp_pe_k, nsa_cmp_w1_k, nsa_cmp_w2_k, nsa_cmp_pe_v, nsa_cmp_w1_v, nsa_cmp_w2_v, diff_lambda_q1, diff_lambda_k1, diff_lambda_q2, diff_lambda_k2, diff_subln_g, w_branch_nsa, w_branch_diff, w_out, ln1_g, ln1_b, w_ffn_in, ffn_conv_w, ffn_conv_b, w_ffn_down, ln2_g, ln2_b, w_ple_proj, w_ple_gate, rel_bias_table):
    depth = w_in.shape[0]
    alpha = (2.0 * depth) ** 0.25
    for l in range(depth):
        lambda_init = 0.8 - 0.6 * math.exp(-0.3 * l)
        x = _layer(x, p[l], w_in[l], nsa_cmp_pe_k[l], nsa_cmp_w1_k[l], nsa_cmp_w2_k[l], nsa_cmp_pe_v[l],
                   nsa_cmp_w1_v[l], nsa_cmp_w2_v[l], diff_lambda_q1[l], diff_lambda_k1[l], diff_lambda_q2[l],
                   diff_lambda_k2[l], diff_subln_g[l], w_branch_nsa[l], w_branch_diff[l], w_out[l], ln1_g[l],
                   ln1_b[l], w_ffn_in[l], ffn_conv_w[l], ffn_conv_b[l], w_ffn_down[l], ln2_g[l], ln2_b[l],
                   w_ple_proj[l], w_ple_gate[l], rel_bias_table, lambda_init, alpha)
    return x
```

```python
import functools
import math

import jax
import jax.numpy as jnp
import numpy as np
from jax import lax
from jax.experimental import pallas as pl
from jax.experimental.pallas import tpu as pltpu

F32 = jnp.float32
BF16 = jnp.bfloat16

NSA_HEADS = 8
NSA_GROUPS = 2
NSA_REP = NSA_HEADS // NSA_GROUPS
HEAD_DIM = 64
CMP_BLOCK = 32
CMP_STRIDE = 16
CMP_HIDDEN = 256
SLC_BLOCK = 64
SLC_TOPK = 16
SLC_LOCAL = 2
WINDOW = 512
DIFF_HEADS = 4
REL_BUCKETS = 32
REL_MAX_EXACT = 16
REL_MAX_DIST = 128
D_FF = 2816
CONV_WIDTH = 3
LN_EPS = 1e-5
NEG_INF = -1e30
BIG = 1e30
MASK_EXP = 100

LANES = 128
SUBLANES = 8
VMEM_LIMIT = 56 * 1024 * 1024

CMP_TQ = 256
NSA_TQ = 256
DIFF_TQ = 512
FAR_KEYS = 1024
ROW_TILE = 512
FFN_TILE = 512
FF_CHUNK = 256
HALO = 16


def _rel_breakpoints():
    n = np.arange(0, 4 * REL_MAX_DIST)
    large = REL_MAX_EXACT + (np.log(np.maximum(n, 1).astype(np.float32) / REL_MAX_EXACT)
                             / np.float32(math.log(REL_MAX_DIST / REL_MAX_EXACT))
                             * (REL_BUCKETS - REL_MAX_EXACT)).astype(np.int32)
    bucket = np.where(n < REL_MAX_EXACT, n, np.minimum(large, REL_BUCKETS - 1))
    assert np.all(np.diff(bucket) >= 0)
    return [int(np.argmax(bucket >= b)) for b in range(1, REL_BUCKETS)]


REL_BREAKS = _rel_breakpoints()


def _dot(a, b):
    return jnp.dot(a, b, preferred_element_type=F32)


def _dot_nt(a, b):
    return lax.dot_general(a, b, (((1,), (1,)), ((), ())), preferred_element_type=F32)


def _rel_bias(dist, tab_ref, head, shift):
    val = jnp.full(dist.shape, tab_ref[0, head] - shift, F32)
    for b, brk in enumerate(REL_BREAKS, start=1):
        val = jnp.where(dist >= brk, tab_ref[b, head] - shift, val)
    return val


def _gelu(x):
    c = math.sqrt(2.0 / math.pi)
    return 0.5 * x * (1.0 + jnp.tanh(c * (x + 0.044715 * (x * x * x))))


def _layer_norm(z, g, b):
    mu = jnp.mean(z, axis=-1, keepdims=True)
    zc = z - mu
    var = jnp.mean(zc * zc, axis=-1, keepdims=True)
    return zc * lax.rsqrt(var + LN_EPS) * g + b


def _params(*sem):
    return pltpu.CompilerParams(dimension_semantics=sem, vmem_limit_bytes=VMEM_LIMIT)


def _proj_kernel(x_ref, wm_ref, wc_ref, wg_ref, om_ref, ok_ref, ov_ref, og_ref):
    xb = x_ref[...].astype(BF16)
    n = wm_ref.shape[1]
    for c in range(0, n, 2 * LANES):
        w = min(2 * LANES, n - c)
        om_ref[:, c:c + w] = _dot(xb, wm_ref[:, c:c + w]).astype(om_ref.dtype)
    kv = _dot(xb, wc_ref[...])
    ok_ref[...] = kv[:, :LANES].astype(ok_ref.dtype)
    ov_ref[...] = kv[:, LANES:].astype(ov_ref.dtype)
    og_ref[...] = _dot(xb, wg_ref[...])


def _proj(x2d, w_main, w_cmp, w_gate):
    T, D = x2d.shape
    n = w_main.shape[1]
    row = lambda width: pl.BlockSpec((ROW_TILE, width), lambda i: (i, 0))
    const = lambda width: pl.BlockSpec((D, width), lambda i: (0, 0))
    return pl.pallas_call(
        _proj_kernel,
        grid=(T // ROW_TILE,),
        in_specs=[row(D), const(n), const(2 * LANES), const(LANES)],
        out_specs=[row(n), row(LANES), row(LANES), row(LANES)],
        out_shape=[jax.ShapeDtypeStruct((T, n), BF16), jax.ShapeDtypeStruct((T, LANES), BF16),
                   jax.ShapeDtypeStruct((T, LANES), BF16), jax.ShapeDtypeStruct((T, LANES), F32)],
        compiler_params=_params("arbitrary"),
        name="proj",
    )(x2d, w_main, w_cmp, w_gate)


def _compress_kernel(xk_ref, xv_ref, pe_ref, w1_ref, w2_ref, o_ref):
    ncp, cw = xk_ref.shape[1], xk_ref.shape[2]
    lane = lax.broadcasted_iota(jnp.int32, (ncp, cw), 1)
    group = jnp.bitwise_and(jnp.right_shift(lane, int(math.log2(HEAD_DIM))), NSA_GROUPS - 1)
    for s, x_ref in enumerate((xk_ref, xv_ref)):
        x = x_ref[0].astype(F32)
        xa = x + pe_ref[s, 0]
        xb = x + pe_ref[s, 1]
        acc = jnp.zeros((ncp, LANES), F32)
        for g in range(NSA_GROUPS):
            a = _dot(jnp.where(group == g, xa, 0.0).astype(BF16), w1_ref[s, 0])
            b = _dot(jnp.where(group == g, xb, 0.0).astype(BF16), w1_ref[s, 1])
            h = a + pltpu.roll(b, ncp - 1, 0)
            acc = acc + _dot(_gelu(h).astype(BF16), w2_ref[s, g])
        o_ref[s, 0] = acc.astype(o_ref.dtype)


def _compress(xk, xv, pe, w1, w2):
    B, ncp, cw = xk.shape
    x_spec = pl.BlockSpec((1, ncp, cw), lambda b: (b, 0, 0))
    const = lambda shape: pl.BlockSpec(shape, lambda b: (0, 0, 0, 0))
    return pl.pallas_call(
        _compress_kernel,
        grid=(B,),
        in_specs=[x_spec, x_spec, const(pe.shape), const(w1.shape), const(w2.shape)],
        out_specs=pl.BlockSpec((2, 1, ncp, LANES), lambda b: (0, b, 0, 0)),
        out_shape=jax.ShapeDtypeStruct((2, B, ncp, LANES), BF16),
        compiler_params=_params("arbitrary"),
        name="compress",
    )(xk, xv, pe, w1, w2)


def _cmp_kernel(tab_ref, q_ref, kc_ref, vc_ref, gate_ref, matt_ref, ocmp_ref, sel_ref, bias_ref, *, n_slc):
    tq = q_ref.shape[1]
    ncp = kc_ref.shape[2]
    q0 = pl.program_id(1) * tq
    lane = lax.broadcasted_iota(jnp.int32, (tq, LANES), 1)
    t_idx = q0 + lax.broadcasted_iota(jnp.int32, (tq, ncp), 0)
    c_idx = lax.broadcasted_iota(jnp.int32, (tq, ncp), 1)
    dist = t_idx - (c_idx * CMP_STRIDE + (CMP_BLOCK - 1))
    valid = dist >= 0
    near = jnp.clip(dist, 0, LANES - 1)

    @pl.when((pl.program_id(0) == 0) & (pl.program_id(1) == 0))
    def _():
        for head in range(NSA_HEADS):
            bias_ref[head] = _rel_bias(lane, tab_ref, head, 0.0)

    def masked_bias(head):
        table = bias_ref[head]
        cols = [jnp.take_along_axis(table, near[:, c:c + LANES], axis=1) for c in range(0, ncp, LANES)]
        return jnp.where(valid, jnp.concatenate(cols, axis=1), NEG_INF)

    gates = jax.nn.sigmoid(gate_ref[0])
    eye = (lax.broadcasted_iota(jnp.int32, (tq, tq), 0)
           == lax.broadcasted_iota(jnp.int32, (tq, tq), 1)).astype(BF16)
    kc = kc_ref[0, 0]
    vc = vc_ref[0, 0]
    jrow_i = lax.broadcasted_iota(jnp.int32, (n_slc, tq), 0)
    cur = jnp.right_shift(q0 + lax.broadcasted_iota(jnp.int32, (n_slc, tq), 1), int(math.log2(SLC_BLOCK)))
    forced = (jrow_i == 0) | ((cur - jrow_i >= 0) & (cur - jrow_i < SLC_LOCAL))
    blk_valid = jrow_i <= cur
    jrow = jrow_i.astype(F32)
    gated = []
    for g in range(NSA_GROUPS):
        lane_g = (lane >= HEAD_DIM * g) & (lane < HEAD_DIM * (g + 1))
        psum = jnp.zeros((tq, ncp), F32)
        outs = []
        for r in range(NSA_REP):
            head = g * NSA_REP + r
            qb = q_ref[0, :, r * LANES:(r + 1) * LANES]
            qb = jnp.where(lane_g, qb, jnp.zeros_like(qb))
            logit = _dot_nt(qb, kc) + masked_bias(head)
            m = jnp.max(logit, axis=-1, keepdims=True)
            e = jnp.exp(logit - m)
            p = jnp.where(valid, e * (1.0 / jnp.sum(e, axis=-1, keepdims=True)), 0.0)
            psum = psum + p
            o = _dot(p.astype(BF16), vc)
            outs.append(gates[:, head * 3:head * 3 + 1] * o)
        gated.append(outs)
        hi = psum.astype(BF16)
        lo = (psum - hi.astype(F32)).astype(BF16)
        p_slc = _dot_nt(matt_ref[...], hi) + _dot_nt(matt_ref[...], lo)
        score = jnp.where(forced, BIG, jnp.where(blk_valid, p_slc, NEG_INF))
        sel = jnp.zeros((n_slc, tq), F32)
        for _ in range(min(SLC_TOPK, n_slc)):
            mx = jnp.max(score, axis=0, keepdims=True)
            idx = jnp.min(jnp.where(score == mx, jrow, float(n_slc)), axis=0, keepdims=True)
            hit = jrow == idx
            sel = jnp.where(hit, 1.0, sel)
            score = jnp.where(hit, -3.0e38, score)
        selm1 = (sel - 1.0).astype(BF16)
        if n_slc < LANES:
            selm1 = jnp.concatenate([selm1, jnp.zeros((LANES - n_slc, tq), BF16)], axis=0)
        sel_ref[0, g] = _dot_nt(eye, selm1).astype(sel_ref.dtype)
    for r in range(NSA_REP):
        ocmp_ref[0, :, r * LANES:(r + 1) * LANES] = jnp.where(lane < HEAD_DIM, gated[0][r], gated[1][r])


def _cmp_attention(tab, proj3, cmp_kv, gates3, matt, n_slc):
    B, S, _ = proj3.shape
    ncp = cmp_kv.shape[2]
    tq = CMP_TQ
    return pl.pallas_call(
        functools.partial(_cmp_kernel, n_slc=n_slc),
        grid=(B, S // tq),
        in_specs=[pl.BlockSpec(memory_space=pltpu.SMEM),
                  pl.BlockSpec((1, tq, 4 * LANES), lambda b, i: (b, i, 0)),
                  pl.BlockSpec((1, 1, ncp, LANES), lambda b, i: (0, b, 0, 0)),
                  pl.BlockSpec((1, 1, ncp, LANES), lambda b, i: (1, b, 0, 0)),
                  pl.BlockSpec((1, tq, LANES), lambda b, i: (b, i, 0)),
                  pl.BlockSpec((n_slc, ncp), lambda b, i: (0, 0))],
        out_specs=[pl.BlockSpec((1, tq, 4 * LANES), lambda b, i: (b, i, 0)),
                   pl.BlockSpec((1, NSA_GROUPS, tq, LANES), lambda b, i: (b, 0, i, 0))],
        out_shape=[jax.ShapeDtypeStruct((B, S, 4 * LANES), F32),
                   jax.ShapeDtypeStruct((B, NSA_GROUPS, S, LANES), BF16)],
        scratch_shapes=[pltpu.VMEM((NSA_HEADS, tq, LANES), F32)],
        compiler_params=_params("arbitrary", "arbitrary"),
        name="cmp",
    )(tab, proj3, cmp_kv, cmp_kv, gates3, matt)


def _two_pass_attention(segments, score, value, m_ref, l_ref, acc_ref):
    m_ref[...] = jnp.full(m_ref.shape, NEG_INF, F32)

    def max_step(k0, width, bias):
        s = score(k0, width, bias)
        mx = s[:, :LANES]
        for c in range(LANES, width, LANES):
            mx = jnp.maximum(mx, s[:, c:c + LANES])
        m_ref[...] = jnp.maximum(m_ref[...], mx)

    segments(max_step)
    m_ref[...] = jnp.broadcast_to(jnp.max(m_ref[...], axis=-1, keepdims=True), m_ref.shape)
    l_ref[...] = jnp.zeros(l_ref.shape, F32)
    acc_ref[...] = jnp.zeros(acc_ref.shape, F32)

    def sum_step(k0, width, bias):
        s = score(k0, width, bias)
        m = m_ref[...]
        ps = [jnp.exp(s[:, c:c + LANES] - m) for c in range(0, width, LANES)]
        l_ref[...] += functools.reduce(lambda a, b: a + b, ps)
        acc_ref[...] += _dot(jnp.concatenate([p.astype(BF16) for p in ps], axis=1), value(k0, width))

    segments(sum_step)
    return acc_ref[...] / jnp.sum(l_ref[...], axis=-1, keepdims=True)


def _causal_segments(fn, qt, tq, near_bias):
    n_far = jnp.maximum(qt - 1, 0)
    far_tiles = max(FAR_KEYS // tq, 1)
    step = far_tiles * tq

    def far_body(i, carry):
        fn(i * step, step, None)
        return carry

    lax.fori_loop(0, lax.div(n_far, far_tiles), far_body, 0)
    for rem in range(1, far_tiles):
        pl.when(lax.rem(n_far, far_tiles) == rem)(
            functools.partial(fn, (n_far - rem) * tq, rem * tq, None))
    pl.when(qt >= 1)(functools.partial(fn, (qt - 1) * tq, 2 * tq, near_bias(0, 2 * tq)))
    pl.when(qt == 0)(functools.partial(fn, 0, tq, near_bias(tq, 2 * tq)))


def _key_slice(k0, width, tq):
    return pl.ds(k0 if isinstance(k0, int) else pl.multiple_of(k0, tq), width)


def _rows(ref, k0, width, tq):
    return ref[0, _key_slice(k0, width, tq), :]


def _nsa_kernel(tab_ref, q_ref, ks_ref, vs_ref, kw_ref, vw_ref, sel_ref, et_ref, gate_ref, ocmp_ref,
                y_ref, tb_ref, m_ref, l_ref, acc_ref, part_ref):
    tq = q_ref.shape[1]
    n_win = WINDOW // tq
    qt = pl.program_id(1)

    @pl.when((pl.program_id(0) == 0) & (qt == 0))
    def _():
        ti = lax.broadcasted_iota(jnp.int32, (tq, tq), 0)
        ki = lax.broadcasted_iota(jnp.int32, (tq, tq), 1)
        for g in range(NSA_GROUPS):
            for r in range(NSA_REP):
                head = g * NSA_REP + r
                far_bias = tab_ref[REL_BUCKETS - 1, head]
                rows = slice(r * tq, (r + 1) * tq)
                tb_ref[g, rows, 0:tq] = jnp.where(ti < ki, 0.0, NEG_INF)
                for j in range(2, n_win):
                    tb_ref[g, rows, (n_win - j) * tq:(n_win - j + 1) * tq] = jnp.zeros((tq, tq), F32)
                tb_ref[g, rows, (n_win - 1) * tq:n_win * tq] = _rel_bias(ti - ki + tq, tab_ref, head, far_bias)
                tb_ref[g, rows, n_win * tq:(n_win + 1) * tq] = jnp.where(
                    ti >= ki, _rel_bias(ti - ki, tab_ref, head, far_bias), NEG_INF)

    lane = lax.broadcasted_iota(jnp.int32, (tq, LANES), 1)
    gates = jax.nn.sigmoid(gate_ref[0])

    for g in range(NSA_GROUPS):
        lane_g = (lane >= HEAD_DIM * g) & (lane < HEAD_DIM * (g + 1))
        q_parts = []
        for r in range(NSA_REP):
            qb = q_ref[0, :, r * LANES:(r + 1) * LANES]
            q_parts.append(jnp.where(lane_g, qb, jnp.zeros_like(qb)))
        q_g = jnp.concatenate(q_parts, axis=0)
        sel_g = jnp.concatenate([sel_ref[0, g]] * NSA_REP, axis=0)
        qs_g = jnp.concatenate([q_g, sel_g], axis=1)

        def bias_cols(lo, hi):
            return lambda: tb_ref[g, :, lo:hi]

        def slc_score(k0, width, bias):
            et = et_ref[_key_slice(k0, width, tq), :]
            s = _dot_nt(qs_g, jnp.concatenate([_rows(ks_ref, k0, width, tq), et], axis=1))
            return s if bias is None else s + bias()

        def slc_segments(fn):
            _causal_segments(fn, qt, tq, lambda lo, hi: bias_cols((n_win - 1) * tq + lo, (n_win - 1) * tq + hi))

        part_ref[...] = _two_pass_attention(slc_segments, slc_score,
                                            lambda k0, width: _rows(vs_ref, k0, width, tq), m_ref, l_ref, acc_ref)

        def win_score(k0, width, bias):
            return _dot_nt(q_g, _rows(kw_ref, k0, width, tq)) + bias()

        def win_segments(fn):
            for n in range(n_win):
                pl.when(qt == n)(functools.partial(fn, 0, (n + 1) * tq, bias_cols((n_win - n) * tq, (n_win + 1) * tq)))
            pl.when(qt >= n_win)(functools.partial(fn, (qt - n_win) * tq, (n_win + 1) * tq,
                                                   bias_cols(0, (n_win + 1) * tq)))

        o_win = _two_pass_attention(win_segments, win_score,
                                    lambda k0, width: _rows(vw_ref, k0, width, tq), m_ref, l_ref, acc_ref)
        o_slc = part_ref[...]

        for r in range(NSA_REP):
            head = g * NSA_REP + r
            rows = slice(r * tq, (r + 1) * tq)
            y = (gates[:, head * 3 + 1:head * 3 + 2] * o_slc[rows]
                 + gates[:, head * 3 + 2:head * 3 + 3] * o_win[rows])
            cols = slice(r * LANES, (r + 1) * LANES)
            if g == 0:
                y_ref[0, :, cols] = (ocmp_ref[0, :, cols] + y).astype(y_ref.dtype)
            else:
                y0 = y_ref[0, :, cols]
                y1 = (ocmp_ref[0, :, cols] + y).astype(y_ref.dtype)
                y_ref[0, :, cols] = jnp.where(lane < HEAD_DIM, y0, y1)


def _nsa_attention(tab, proj3, sel, et, gates3, ocmp, col_blocks):
    B, S, _ = proj3.shape
    tq = NSA_TQ
    assert WINDOW % tq == 0 and WINDOW // tq >= 2 and S % tq == 0
    n_win = WINDOW // tq
    ks_c, vs_c, kw_c, vw_c = col_blocks
    rows = NSA_REP * tq

    def kv_spec(c):
        return pl.BlockSpec((1, S, LANES), lambda b, i: (b, 0, c))

    return pl.pallas_call(
        _nsa_kernel,
        grid=(B, S // tq),
        in_specs=[pl.BlockSpec(memory_space=pltpu.SMEM),
                  pl.BlockSpec((1, tq, 4 * LANES), lambda b, i: (b, i, 0)),
                  kv_spec(ks_c), kv_spec(vs_c), kv_spec(kw_c), kv_spec(vw_c),
                  pl.BlockSpec((1, NSA_GROUPS, tq, LANES), lambda b, i: (b, 0, i, 0)),
                  pl.BlockSpec((S, LANES), lambda b, i: (0, 0)),
                  pl.BlockSpec((1, tq, LANES), lambda b, i: (b, i, 0)),
                  pl.BlockSpec((1, tq, 4 * LANES), lambda b, i: (b, i, 0))],
        out_specs=pl.BlockSpec((1, tq, 4 * LANES), lambda b, i: (b, i, 0)),
        out_shape=jax.ShapeDtypeStruct((B, S, 4 * LANES), BF16),
        scratch_shapes=[pltpu.VMEM((NSA_GROUPS, rows, (n_win + 1) * tq), F32),
                        pltpu.VMEM((rows, LANES), F32),
                        pltpu.VMEM((rows, LANES), F32),
                        pltpu.VMEM((rows, LANES), F32),
                        pltpu.VMEM((rows, LANES), F32)],
        compiler_params=_params("arbitrary", "arbitrary"),
        name="nsa",
    )(tab, proj3, proj3, proj3, proj3, proj3, sel, et, gates3, ocmp)


def _diff_kernel(tab_ref, q_ref, k_ref, v_ref, lam_ref, g_ref, y_ref, tb_ref, m_ref, l_ref, acc_ref,
                 *, lambda_init):
    tq = q_ref.shape[1]
    h = pl.program_id(1)
    qt = pl.program_id(2)

    @pl.when(qt == 0)
    def _():
        ti = lax.broadcasted_iota(jnp.int32, (tq, tq), 0)
        ki = lax.broadcasted_iota(jnp.int32, (tq, tq), 1)
        head = NSA_HEADS + h
        far_bias = tab_ref[REL_BUCKETS - 1, head]
        tb_ref[:, 0:tq] = _rel_bias(ti - ki + tq, tab_ref, head, far_bias)
        tb_ref[:, tq:2 * tq] = jnp.where(ti >= ki, _rel_bias(ti - ki, tab_ref, head, far_bias), NEG_INF)

    lane = lax.broadcasted_iota(jnp.int32, (tq, LANES), 1)
    qb = q_ref[0]
    zero = jnp.zeros_like(qb)
    q2 = jnp.concatenate([jnp.where(lane < HEAD_DIM, qb, zero), jnp.where(lane >= HEAD_DIM, qb, zero)], axis=0)

    def score(k0, width, bias):
        s = _dot_nt(q2, _rows(k_ref, k0, width, tq))
        if bias is None:
            return s
        tb = bias()
        return s + jnp.concatenate([tb, tb], axis=0)

    def segments(fn):
        _causal_segments(fn, qt, tq, lambda lo, hi: (lambda: tb_ref[:, lo:hi]))

    a = _two_pass_attention(segments, score, lambda k0, width: _rows(v_ref, k0, width, tq), m_ref, l_ref, acc_ref)
    lq1, lk1, lq2, lk2 = lam_ref[0:1, :], lam_ref[1:2, :], lam_ref[2:3, :], lam_ref[3:4, :]
    lam = (jnp.exp(jnp.sum(lq1 * lk1, axis=-1, keepdims=True))
           - jnp.exp(jnp.sum(lq2 * lk2, axis=-1, keepdims=True)) + lambda_init)
    o = a[:tq] - lam * a[tq:]
    o = o * lax.rsqrt(jnp.mean(o * o, axis=-1, keepdims=True) + LN_EPS) * g_ref[...]
    y_ref[0] = (o * (1.0 - lambda_init)).astype(y_ref.dtype)


def _diff_attention(tab, proj3, lam4, subln_g, col_blocks, lambda_init):
    B, S, _ = proj3.shape
    tq = min(DIFF_TQ, S)
    q_c, k_c, v_c = col_blocks
    return pl.pallas_call(
        functools.partial(_diff_kernel, lambda_init=lambda_init),
        grid=(B, DIFF_HEADS, S // tq),
        in_specs=[pl.BlockSpec(memory_space=pltpu.SMEM),
                  pl.BlockSpec((1, tq, LANES), lambda b, h, i: (b, i, q_c + h)),
                  pl.BlockSpec((1, S, LANES), lambda b, h, i: (b, 0, k_c + h)),
                  pl.BlockSpec((1, S, LANES), lambda b, h, i: (b, 0, v_c + h)),
                  pl.BlockSpec((SUBLANES, HEAD_DIM), lambda b, h, i: (0, 0)),
                  pl.BlockSpec((1, LANES), lambda b, h, i: (0, 0))],
        out_specs=pl.BlockSpec((1, tq, LANES), lambda b, h, i: (b, i, h)),
        out_shape=jax.ShapeDtypeStruct((B, S, DIFF_HEADS * LANES), BF16),
        scratch_shapes=[pltpu.VMEM((tq, 2 * tq), F32),
                        pltpu.VMEM((2 * tq, LANES), F32),
                        pltpu.VMEM((2 * tq, LANES), F32),
                        pltpu.VMEM((2 * tq, LANES), F32)],
        compiler_params=_params("arbitrary", "arbitrary", "arbitrary"),
        name="diff",
    )(tab, proj3, proj3, proj3, lam4, subln_g)


def _merge_kernel(x_ref, yn_ref, yd_ref, wgn_ref, wgd_ref, wbn_ref, wbd_ref, wo_ref, g_ref, b_ref, o_ref,
                  *, alpha):
    x = x_ref[...]
    xb = x.astype(BF16)
    merged = (jax.nn.sigmoid(_dot(xb, wgn_ref[...])) * _dot(yn_ref[...], wbn_ref[...])
              + jax.nn.sigmoid(_dot(xb, wgd_ref[...])) * _dot(yd_ref[...], wbd_ref[...]))
    z = alpha * x + _dot(merged.astype(BF16), wo_ref[...])
    o_ref[...] = _layer_norm(z, g_ref[...], b_ref[...])


def _merge(x2d, y_nsa, y_diff, wgn, wgd, wbn, wbd, wo, ln_g, ln_b, alpha):
    T, D = x2d.shape
    const = lambda shape: pl.BlockSpec(shape, lambda i: (0, 0))
    return pl.pallas_call(
        functools.partial(_merge_kernel, alpha=alpha),
        grid=(T // ROW_TILE,),
        in_specs=[pl.BlockSpec((ROW_TILE, D), lambda i: (i, 0)),
                  pl.BlockSpec((ROW_TILE, y_nsa.shape[1]), lambda i: (i, 0)),
                  pl.BlockSpec((ROW_TILE, y_diff.shape[1]), lambda i: (i, 0)),
                  const(wgn.shape), const(wgd.shape), const(wbn.shape), const(wbd.shape), const(wo.shape),
                  const((1, D)), const((1, D))],
        out_specs=pl.BlockSpec((ROW_TILE, D), lambda i: (i, 0)),
        out_shape=jax.ShapeDtypeStruct((T, D), F32),
        compiler_params=_params("arbitrary"),
        name="merge",
    )(x2d, y_nsa, y_diff, wgn, wgd, wbn, wbd, wo, ln_g, ln_b)


def _ffn_kernel(x_ref, halo_ref, p_ref, w_ref, cw_ref, cb_ref, wd_ref, g_ref, b_ref, wpg_ref, wpp_ref,
                o_ref, acc_ref, *, alpha, tiles_per_seq):
    tm = x_ref.shape[0]
    d_ff = wd_ref.shape[0]
    x = x_ref[...]
    xb = x.astype(BF16)
    keep = (pl.program_id(0) % tiles_per_seq != 0).astype(F32)
    hb = halo_ref[...].astype(BF16)
    row = lax.broadcasted_iota(jnp.int32, (tm, FF_CHUNK), 0)
    for c in range(0, d_ff, FF_CHUNK):
        cols = slice(c, c + FF_CHUNK)
        wg = w_ref[:, cols]
        gm = _dot(xb, wg)
        gh = _dot(hb, wg) * keep
        um = _dot(xb, w_ref[:, d_ff + c:d_ff + c + FF_CHUNK])
        h1 = gh[HALO - 1:HALO, :]
        h2 = gh[HALO - 2:HALO - 1, :]
        g1 = jnp.where(row == 0, h1, pltpu.roll(gm, 1, 0))
        g2 = jnp.where(row == 0, h2, jnp.where(row == 1, h1, pltpu.roll(gm, 2, 0)))
        conv = cb_ref[:, cols] + cw_ref[0:1, cols] * g2 + cw_ref[1:2, cols] * g1 + cw_ref[2:3, cols] * gm
        act = (_gelu(conv) * um).astype(BF16)
        down = _dot(act, wd_ref[cols, :])
        if c == 0:
            acc_ref[...] = down
        else:
            acc_ref[...] += down
    x2 = _layer_norm(alpha * x + acc_ref[...], g_ref[...], b_ref[...])
    gate = jax.nn.sigmoid(_dot(x2.astype(BF16), wpg_ref[...]))
    o_ref[...] = x2 + gate * _dot(p_ref[...].astype(BF16), wpp_ref[...])


def _ffn(x1, p2d, w_in, cw, cb, wd, ln_g, ln_b, wpg, wpp, alpha, seq):
    T, D = x1.shape
    tm = FFN_TILE
    assert seq % tm == 0 and wd.shape[0] % FF_CHUNK == 0
    hb = tm // HALO
    const = lambda shape: pl.BlockSpec(shape, lambda i: (0, 0))
    return pl.pallas_call(
        functools.partial(_ffn_kernel, alpha=alpha, tiles_per_seq=seq // tm),
        grid=(T // tm,),
        in_specs=[pl.BlockSpec((tm, D), lambda i: (i, 0)),
                  pl.BlockSpec((HALO, D), lambda i: (jnp.maximum(i * hb - 1, 0), 0)),
                  pl.BlockSpec((tm, p2d.shape[1]), lambda i: (i, 0)),
                  const(w_in.shape), const(cw.shape), const(cb.shape), const(wd.shape),
                  const((1, D)), const((1, D)), const(wpg.shape), const(wpp.shape)],
        out_specs=pl.BlockSpec((tm, D), lambda i: (i, 0)),
        out_shape=jax.ShapeDtypeStruct((T, D), F32),
        scratch_shapes=[pltpu.VMEM((tm, D), F32)],
        compiler_params=_params("arbitrary"),
        name="ffn",
    )(x1, x1, p2d, w_in, cw, cb, wd, ln_g, ln_b, wpg, wpp)


def _slc_from_cmp_t(ncp, n_slc):
    ratio = SLC_BLOCK // CMP_STRIDE
    span = CMP_BLOCK // CMP_STRIDE
    mat = np.zeros((n_slc, ncp), np.float32)
    for j in range(n_slc):
        for m in range(ratio):
            for n in range(span):
                i = ratio * j + m - n
                if 0 <= i < ncp - 1:
                    mat[j, i] += 1.0
    return mat


def _layer(x, p_l, w_in, pe_k, w1_k, w2_k, pe_v, w1_v, w2_v, lq1, lk1, lq2, lk2, subln_g, w_bn, w_bd, w_out,
           ln1_g, ln1_b, w_ffn_in, conv_w, conv_b, w_down, ln2_g, ln2_b, w_pp, w_pg, tab, lambda_init,
           alpha):
    B, S, D = x.shape
    T = B * S
    ncp = S // CMP_STRIDE
    n_slc = S // SLC_BLOCK
    q_w = NSA_HEADS * HEAD_DIM
    kv_w = NSA_GROUPS * HEAD_DIM
    dqk_w = DIFF_HEADS * 2 * HEAD_DIM
    sizes = (q_w,) + (kv_w,) * 6 + (NSA_HEADS * 3, dqk_w, dqk_w, dqk_w, D, D)
    offs = np.concatenate([[0], np.cumsum(sizes)])
    col = lambda i: w_in[:, int(offs[i]):int(offs[i + 1])]
    scale = HEAD_DIM ** -0.5

    n_idx = np.arange(q_w)
    perm = (NSA_REP * ((n_idx % LANES) // HEAD_DIM) + n_idx // LANES) * HEAD_DIM + n_idx % HEAD_DIM
    w_main = jnp.concatenate([col(0)[:, perm] * scale] + [col(i) for i in range(3, 7)]
                             + [col(8) * scale, col(9), col(10)], axis=1).astype(BF16)
    w_cmp = jnp.concatenate([col(1), col(2)], axis=1).astype(BF16)
    w_gate = jnp.pad(col(7), ((0, 0), (0, LANES - NSA_HEADS * 3))).astype(BF16)
    x2d = x.reshape(T, D)
    proj, k_cmp, v_cmp, gates = _proj(x2d, w_main, w_cmp, w_gate)
    proj3 = proj.reshape(B, S, proj.shape[1])
    gates3 = gates.reshape(B, S, LANES)
    c_kslc, c_vslc, c_kwin, c_vwin = (q_w // LANES + i for i in range(4))
    c_dq = q_w // LANES + 4
    c_dk = c_dq + DIFF_HEADS
    c_dv = c_dk + DIFF_HEADS

    cw = CMP_STRIDE * kv_w
    rep = lambda a: jnp.broadcast_to(a.reshape(2, 2, CMP_STRIDE, 1, HEAD_DIM, -1),
                                     (2, 2, CMP_STRIDE, NSA_GROUPS, HEAD_DIM, a.shape[-1]))
    pe = rep(jnp.stack([pe_k, pe_v])[..., None]).reshape(2, 2, 1, cw)
    w1 = rep(jnp.stack([w1_k, w1_v])).reshape(2, 2, cw, CMP_HIDDEN).astype(BF16)
    w2 = jnp.stack([w2_k, w2_v])
    w2p = jnp.stack([jnp.pad(w2, ((0, 0), (0, 0), (g * HEAD_DIM, LANES - (g + 1) * HEAD_DIM)))
                     for g in range(NSA_GROUPS)], axis=1).astype(BF16)
    cmp_kv = _compress(k_cmp.reshape(B, ncp, cw), v_cmp.reshape(B, ncp, cw), pe, w1, w2p)

    matt = jnp.asarray(_slc_from_cmp_t(ncp, n_slc), BF16)
    ocmp, sel = _cmp_attention(tab, proj3, cmp_kv, gates3, matt, n_slc)
    et_np = np.zeros((S, LANES), np.float32)
    et_np[np.arange(S), np.arange(S) // SLC_BLOCK] = 2.0 ** MASK_EXP
    y_nsa = _nsa_attention(tab, proj3, sel, jnp.asarray(et_np, BF16), gates3, ocmp,
                           (c_kslc, c_vslc, c_kwin, c_vwin))

    lam4 = jnp.pad(jnp.stack([lq1, lk1, lq2, lk2]), ((0, SUBLANES - 4), (0, 0)))
    y_diff = _diff_attention(tab, proj3, lam4, subln_g.reshape(1, LANES), (c_dq, c_dk, c_dv), lambda_init)

    x1 = _merge(x2d, y_nsa.reshape(T, q_w), y_diff.reshape(T, dqk_w),
                col(11).astype(BF16), col(12).astype(BF16), w_bn[perm].astype(BF16), w_bd.astype(BF16),
                w_out.astype(BF16), ln1_g.reshape(1, D), ln1_b.reshape(1, D), alpha)

    cw = jnp.pad(conv_w, ((0, SUBLANES - CONV_WIDTH), (0, 0)))
    out = _ffn(x1, p_l.reshape(T, p_l.shape[-1]), w_ffn_in.astype(BF16), cw, conv_b.reshape(1, -1),
               w_down.astype(BF16), ln2_g.reshape(1, D), ln2_b.reshape(1, D),
               w_pg.astype(BF16), w_pp.astype(BF16), alpha, S)
    return out.reshape(B, S, D)


def kernel(x, p, w_in, nsa_cmp_pe_k, nsa_cmp_w1_k, nsa_cmp_w2_k, nsa_cmp_pe_v, nsa_cmp_w1_v, nsa_cmp_w2_v, diff_lambda_q1, diff_lambda_k1, diff_lambda_q2, diff_lambda_k2, diff_subln_g, w_branch_nsa, w_branch_diff, w_out, ln1_g, ln1_b, w_ffn_in, ffn_conv_w, ffn_conv_b, w_ffn_down, ln2_g, ln2_b, w_ple_proj, w_ple_gate, rel_bias_table):
    depth = w_in.shape[0]
    alpha = (2.0 * depth) ** 0.25
    for l in range(depth):
        lambda_init = 0.8 - 0.6 * math.exp(-0.3 * l)
        x = _layer(x, p[l], w_in[l], nsa_cmp_pe_k[l], nsa_cmp_w1_k[l], nsa_cmp_w2_k[l], nsa_cmp_pe_v[l],
                   nsa_cmp_w1_v[l], nsa_cmp_w2_v[l], diff_lambda_q1[l], diff_lambda_k1[l], diff_lambda_q2[l],
                   diff_lambda_k2[l], diff_subln_g[l], w_branch_nsa[l], w_branch_diff[l], w_out[l], ln1_g[l],
                   ln1_b[l], w_ffn_in[l], ffn_conv_w[l], ffn_conv_b[l], w_ffn_down[l], ln2_g[l], ln2_b[l],
                   w_ple_proj[l], w_ple_gate[l], rel_bias_table, lambda_init, alpha)
    return x
```

```python
import functools
import math

import jax
import jax.numpy as jnp
import numpy as np
from jax import lax
from jax.experimental import pallas as pl
from jax.experimental.pallas import tpu as pltpu

F32 = jnp.float32
BF16 = jnp.bfloat16

NSA_HEADS = 8
NSA_GROUPS = 2
NSA_REP = NSA_HEADS // NSA_GROUPS
HEAD_DIM = 64
CMP_BLOCK = 32
CMP_STRIDE = 16
CMP_HIDDEN = 256
SLC_BLOCK = 64
SLC_TOPK = 16
SLC_LOCAL = 2
WINDOW = 512
DIFF_HEADS = 4
REL_BUCKETS = 32
REL_MAX_EXACT = 16
REL_MAX_DIST = 128
D_FF = 2816
CONV_WIDTH = 3
LN_EPS = 1e-5
NEG_INF = -1e30
BIG = 1e30
MASK_EXP = 100

LANES = 128
SUBLANES = 8
VMEM_LIMIT = 56 * 1024 * 1024

CMP_TQ = 256
NSA_TQ = 256
DIFF_TQ = 512
NSA_FAR_KEYS = 512
DIFF_FAR_KEYS = 1024
ROW_TILE = 512
FFN_TILE = 512
FF_CHUNK = 256
HALO = 16


def _rel_breakpoints():
    n = np.arange(0, 4 * REL_MAX_DIST)
    large = REL_MAX_EXACT + (np.log(np.maximum(n, 1).astype(np.float32) / REL_MAX_EXACT)
                             / np.float32(math.log(REL_MAX_DIST / REL_MAX_EXACT))
                             * (REL_BUCKETS - REL_MAX_EXACT)).astype(np.int32)
    bucket = np.where(n < REL_MAX_EXACT, n, np.minimum(large, REL_BUCKETS - 1))
    assert np.all(np.diff(bucket) >= 0)
    return [int(np.argmax(bucket >= b)) for b in range(1, REL_BUCKETS)]


REL_BREAKS = _rel_breakpoints()


def _dot(a, b):
    return jnp.dot(a, b, preferred_element_type=F32)


def _dot_nt(a, b):
    return lax.dot_general(a, b, (((1,), (1,)), ((), ())), preferred_element_type=F32)


def _rel_bias(dist, tab_ref, head, shift):
    val = jnp.full(dist.shape, tab_ref[0, head] - shift, F32)
    for b, brk in enumerate(REL_BREAKS, start=1):
        val = jnp.where(dist >= brk, tab_ref[b, head] - shift, val)
    return val


def _gelu(x):
    c = math.sqrt(2.0 / math.pi)
    half = 0.5 * x
    return half + half * jnp.tanh(x * (c + (c * 0.044715) * (x * x)))


def _layer_norm(z, g, b):
    mu = jnp.mean(z, axis=-1, keepdims=True)
    zc = z - mu
    var = jnp.mean(zc * zc, axis=-1, keepdims=True)
    return zc * lax.rsqrt(var + LN_EPS) * g + b


def _params(*sem):
    return pltpu.CompilerParams(dimension_semantics=sem, vmem_limit_bytes=VMEM_LIMIT)


def _proj_kernel(x_ref, wm_ref, wc_ref, wg_ref, om_ref, ok_ref, ov_ref, og_ref):
    xb = x_ref[...].astype(BF16)
    n = wm_ref.shape[1]
    for c in range(0, n, 2 * LANES):
        w = min(2 * LANES, n - c)
        om_ref[:, c:c + w] = _dot(xb, wm_ref[:, c:c + w]).astype(om_ref.dtype)
    kv = _dot(xb, wc_ref[...])
    ok_ref[...] = kv[:, :LANES].astype(ok_ref.dtype)
    ov_ref[...] = kv[:, LANES:].astype(ov_ref.dtype)
    og_ref[...] = _dot(xb, wg_ref[...])


def _proj(x2d, w_main, w_cmp, w_gate):
    T, D = x2d.shape
    n = w_main.shape[1]
    row = lambda width: pl.BlockSpec((ROW_TILE, width), lambda i: (i, 0))
    const = lambda width: pl.BlockSpec((D, width), lambda i: (0, 0))
    return pl.pallas_call(
        _proj_kernel,
        grid=(T // ROW_TILE,),
        in_specs=[row(D), const(n), const(2 * LANES), const(LANES)],
        out_specs=[row(n), row(LANES), row(LANES), row(LANES)],
        out_shape=[jax.ShapeDtypeStruct((T, n), BF16), jax.ShapeDtypeStruct((T, LANES), BF16),
                   jax.ShapeDtypeStruct((T, LANES), BF16), jax.ShapeDtypeStruct((T, LANES), F32)],
        compiler_params=_params("arbitrary"),
        name="proj",
    )(x2d, w_main, w_cmp, w_gate)


def _compress_kernel(xk_ref, xv_ref, pe_ref, w1_ref, w2_ref, o_ref):
    ncp, cw = xk_ref.shape[1], xk_ref.shape[2]
    lane = lax.broadcasted_iota(jnp.int32, (ncp, cw), 1)
    group = jnp.bitwise_and(jnp.right_shift(lane, int(math.log2(HEAD_DIM))), NSA_GROUPS - 1)
    for s, x_ref in enumerate((xk_ref, xv_ref)):
        x = x_ref[0].astype(F32)
        xa = x + pe_ref[s, 0]
        xb = x + pe_ref[s, 1]
        acc = jnp.zeros((ncp, LANES), F32)
        for g in range(NSA_GROUPS):
            a = _dot(jnp.where(group == g, xa, 0.0).astype(BF16), w1_ref[s, 0])
            b = _dot(jnp.where(group == g, xb, 0.0).astype(BF16), w1_ref[s, 1])
            h = a + pltpu.roll(b, ncp - 1, 0)
            acc = acc + _dot(_gelu(h).astype(BF16), w2_ref[s, g])
        o_ref[s, 0] = acc.astype(o_ref.dtype)


def _compress(xk, xv, pe, w1, w2):
    B, ncp, cw = xk.shape
    x_spec = pl.BlockSpec((1, ncp, cw), lambda b: (b, 0, 0))
    const = lambda shape: pl.BlockSpec(shape, lambda b: (0, 0, 0, 0))
    return pl.pallas_call(
        _compress_kernel,
        grid=(B,),
        in_specs=[x_spec, x_spec, const(pe.shape), const(w1.shape), const(w2.shape)],
        out_specs=pl.BlockSpec((2, 1, ncp, LANES), lambda b: (0, b, 0, 0)),
        out_shape=jax.ShapeDtypeStruct((2, B, ncp, LANES), BF16),
        compiler_params=_params("arbitrary"),
        name="compress",
    )(xk, xv, pe, w1, w2)


def _cmp_kernel(tab_ref, q_ref, kc_ref, vc_ref, gate_ref, matt_ref, ocmp_ref, sel_ref, bias_ref, *, n_slc):
    tq = q_ref.shape[1]
    ncp = kc_ref.shape[2]
    q0 = pl.program_id(1) * tq
    lane = lax.broadcasted_iota(jnp.int32, (tq, LANES), 1)
    t_idx = q0 + lax.broadcasted_iota(jnp.int32, (tq, ncp), 0)
    c_idx = lax.broadcasted_iota(jnp.int32, (tq, ncp), 1)
    dist = t_idx - (c_idx * CMP_STRIDE + (CMP_BLOCK - 1))
    valid = dist >= 0
    near = jnp.clip(dist, 0, LANES - 1)

    @pl.when((pl.program_id(0) == 0) & (pl.program_id(1) == 0))
    def _():
        for head in range(NSA_HEADS):
            bias_ref[head] = _rel_bias(lane, tab_ref, head, 0.0)

    def masked_bias(head):
        table = bias_ref[head]
        cols = [jnp.take_along_axis(table, near[:, c:c + LANES], axis=1) for c in range(0, ncp, LANES)]
        return jnp.where(valid, jnp.concatenate(cols, axis=1), NEG_INF)

    gates = jax.nn.sigmoid(gate_ref[0])
    eye = (lax.broadcasted_iota(jnp.int32, (tq, tq), 0)
           == lax.broadcasted_iota(jnp.int32, (tq, tq), 1)).astype(BF16)
    kc = kc_ref[0, 0]
    vc = vc_ref[0, 0]
    jrow_i = lax.broadcasted_iota(jnp.int32, (n_slc, tq), 0)
    cur = jnp.right_shift(q0 + lax.broadcasted_iota(jnp.int32, (n_slc, tq), 1), int(math.log2(SLC_BLOCK)))
    forced = (jrow_i == 0) | ((cur - jrow_i >= 0) & (cur - jrow_i < SLC_LOCAL))
    blk_valid = jrow_i <= cur
    jrow = jrow_i.astype(F32)
    gated = []
    for g in range(NSA_GROUPS):
        lane_g = (lane >= HEAD_DIM * g) & (lane < HEAD_DIM * (g + 1))
        psum = jnp.zeros((tq, ncp), F32)
        outs = []
        for r in range(NSA_REP):
            head = g * NSA_REP + r
            qb = q_ref[0, :, r * LANES:(r + 1) * LANES]
            qb = jnp.where(lane_g, qb, jnp.zeros_like(qb))
            logit = _dot_nt(qb, kc) + masked_bias(head)
            m = jnp.max(logit, axis=-1, keepdims=True)
            e = jnp.exp(logit - m)
            p = jnp.where(valid, e * (1.0 / jnp.sum(e, axis=-1, keepdims=True)), 0.0)
            psum = psum + p
            o = _dot(p.astype(BF16), vc)
            outs.append(gates[:, head * 3:head * 3 + 1] * o)
        gated.append(outs)
        hi = psum.astype(BF16)
        lo = (psum - hi.astype(F32)).astype(BF16)
        p_slc = _dot_nt(matt_ref[...], hi) + _dot_nt(matt_ref[...], lo)
        score = jnp.where(forced, BIG, jnp.where(blk_valid, p_slc, NEG_INF))
        sel = jnp.zeros((n_slc, tq), F32)
        for _ in range(min(SLC_TOPK, n_slc)):
            mx = jnp.max(score, axis=0, keepdims=True)
            idx = jnp.min(jnp.where(score == mx, jrow, float(n_slc)), axis=0, keepdims=True)
            hit = jrow == idx
            sel = jnp.where(hit, 1.0, sel)
            score = jnp.where(hit, -3.0e38, score)
        selm1 = (sel - 1.0).astype(BF16)
        if n_slc < LANES:
            selm1 = jnp.concatenate([selm1, jnp.zeros((LANES - n_slc, tq), BF16)], axis=0)
        sel_ref[0, g] = _dot_nt(eye, selm1).astype(sel_ref.dtype)
    for r in range(NSA_REP):
        ocmp_ref[0, :, r * LANES:(r + 1) * LANES] = jnp.where(lane < HEAD_DIM, gated[0][r], gated[1][r])


def _cmp_attention(tab, proj3, cmp_kv, gates3, matt, n_slc):
    B, S, _ = proj3.shape
    ncp = cmp_kv.shape[2]
    tq = CMP_TQ
    return pl.pallas_call(
        functools.partial(_cmp_kernel, n_slc=n_slc),
        grid=(B, S // tq),
        in_specs=[pl.BlockSpec(memory_space=pltpu.SMEM),
                  pl.BlockSpec((1, tq, 4 * LANES), lambda b, i: (b, i, 0)),
                  pl.BlockSpec((1, 1, ncp, LANES), lambda b, i: (0, b, 0, 0)),
                  pl.BlockSpec((1, 1, ncp, LANES), lambda b, i: (1, b, 0, 0)),
                  pl.BlockSpec((1, tq, LANES), lambda b, i: (b, i, 0)),
                  pl.BlockSpec((n_slc, ncp), lambda b, i: (0, 0))],
        out_specs=[pl.BlockSpec((1, tq, 4 * LANES), lambda b, i: (b, i, 0)),
                   pl.BlockSpec((1, NSA_GROUPS, tq, LANES), lambda b, i: (b, 0, i, 0))],
        out_shape=[jax.ShapeDtypeStruct((B, S, 4 * LANES), F32),
                   jax.ShapeDtypeStruct((B, NSA_GROUPS, S, LANES), BF16)],
        scratch_shapes=[pltpu.VMEM((NSA_HEADS, tq, LANES), F32)],
        compiler_params=_params("arbitrary", "arbitrary"),
        name="cmp",
    )(tab, proj3, cmp_kv, cmp_kv, gates3, matt)


def _two_pass_attention(segments, score, value, m_ref, l_ref, acc_ref):
    def columns(k0, width, bias):
        s = score(k0, width)
        cols = []
        for c in range(0, width, LANES):
            b = None if bias is None else bias(c)
            cols.append(s[:, c:c + LANES] if b is None else s[:, c:c + LANES] + b)
        return cols

    m_ref[...] = jnp.full(m_ref.shape, NEG_INF, F32)

    def max_step(k0, width, bias):
        m_ref[...] = jnp.maximum(m_ref[...], functools.reduce(jnp.maximum, columns(k0, width, bias)))

    segments(max_step)
    m_ref[...] = jnp.broadcast_to(jnp.max(m_ref[...], axis=-1, keepdims=True), m_ref.shape)
    l_ref[...] = jnp.zeros(l_ref.shape, F32)
    acc_ref[...] = jnp.zeros(acc_ref.shape, F32)

    def sum_step(k0, width, bias):
        m = m_ref[...]
        ps = [jnp.exp(col - m) for col in columns(k0, width, bias)]
        l_ref[...] += functools.reduce(lambda a, b: a + b, ps)
        acc_ref[...] += _dot(jnp.concatenate([p.astype(BF16) for p in ps], axis=1), value(k0, width))

    segments(sum_step)
    return acc_ref[...] / jnp.sum(l_ref[...], axis=-1, keepdims=True)


def _causal_segments(fn, qt, tq, far_keys, near_bias):
    n_far = jnp.maximum(qt - 1, 0)
    far_tiles = far_keys // tq
    step = far_tiles * tq

    def far_body(i, carry):
        fn(i * step, step, None)
        return carry

    lax.fori_loop(0, lax.div(n_far, far_tiles), far_body, 0)
    for rem in range(far_tiles):
        lead = rem * tq
        bias = (lambda c, lead=lead: None if c < lead else near_bias(c - lead))
        pl.when((qt >= 1) & (lax.rem(n_far, far_tiles) == rem))(
            functools.partial(fn, (qt - 1 - rem) * tq, lead + 2 * tq, bias))
    pl.when(qt == 0)(functools.partial(fn, 0, tq, lambda c: near_bias(tq + c)))


def _key_slice(k0, width, tq):
    return pl.ds(k0 if isinstance(k0, int) else pl.multiple_of(k0, tq), width)


def _rows(ref, k0, width, tq):
    return ref[0, _key_slice(k0, width, tq), :]


def _nsa_kernel(tab_ref, q_ref, ks_ref, vs_ref, kw_ref, vw_ref, sel_ref, et_ref, gate_ref, ocmp_ref,
                y_ref, tb_ref, m_ref, l_ref, acc_ref, part_ref):
    tq = q_ref.shape[1]
    n_win = WINDOW // tq
    qt = pl.program_id(1)

    @pl.when((pl.program_id(0) == 0) & (qt == 0))
    def _():
        ti = lax.broadcasted_iota(jnp.int32, (tq, tq), 0)
        ki = lax.broadcasted_iota(jnp.int32, (tq, tq), 1)
        for g in range(NSA_GROUPS):
            for r in range(NSA_REP):
                head = g * NSA_REP + r
                far_bias = tab_ref[REL_BUCKETS - 1, head]
                rows = slice(head * tq, (head + 1) * tq)
                tb_ref[rows, 0:tq] = jnp.where(ti < ki, 0.0, NEG_INF)
                for j in range(2, n_win):
                    tb_ref[rows, (n_win - j) * tq:(n_win - j + 1) * tq] = jnp.zeros((tq, tq), F32)
                tb_ref[rows, (n_win - 1) * tq:n_win * tq] = _rel_bias(ti - ki + tq, tab_ref, head, far_bias)
                tb_ref[rows, n_win * tq:(n_win + 1) * tq] = jnp.where(
                    ti >= ki, _rel_bias(ti - ki, tab_ref, head, far_bias), NEG_INF)

    lane = lax.broadcasted_iota(jnp.int32, (tq, LANES), 1)
    gates = jax.nn.sigmoid(gate_ref[0])

    q_parts, sel_parts = [], []
    for g in range(NSA_GROUPS):
        lane_g = (lane >= HEAD_DIM * g) & (lane < HEAD_DIM * (g + 1))
        for r in range(NSA_REP):
            qb = q_ref[0, :, r * LANES:(r + 1) * LANES]
            q_parts.append(jnp.where(lane_g, qb, jnp.zeros_like(qb)))
            sel_parts.append(sel_ref[0, g])
    q_all = jnp.concatenate(q_parts, axis=0)
    qs_all = jnp.concatenate([q_all, jnp.concatenate(sel_parts, axis=0)], axis=1)

    def bias_from(col0):
        return lambda c: tb_ref[:, col0 + c:col0 + c + LANES]

    def slc_score(k0, width):
        et = et_ref[_key_slice(k0, width, tq), :]
        return _dot_nt(qs_all, jnp.concatenate([_rows(ks_ref, k0, width, tq), et], axis=1))

    def slc_segments(fn):
        _causal_segments(fn, qt, tq, NSA_FAR_KEYS, bias_from((n_win - 1) * tq))

    part_ref[...] = _two_pass_attention(slc_segments, slc_score,
                                        lambda k0, width: _rows(vs_ref, k0, width, tq), m_ref, l_ref, acc_ref)

    def win_score(k0, width):
        return _dot_nt(q_all, _rows(kw_ref, k0, width, tq))

    def win_segments(fn):
        for n in range(n_win):
            pl.when(qt == n)(functools.partial(fn, 0, (n + 1) * tq, bias_from((n_win - n) * tq)))
        pl.when(qt >= n_win)(functools.partial(fn, (qt - n_win) * tq, (n_win + 1) * tq, bias_from(0)))

    o_win = _two_pass_attention(win_segments, win_score,
                                lambda k0, width: _rows(vw_ref, k0, width, tq), m_ref, l_ref, acc_ref)
    o_slc = part_ref[...]

    for r in range(NSA_REP):
        ys = []
        for g in range(NSA_GROUPS):
            head = g * NSA_REP + r
            rows = slice(head * tq, (head + 1) * tq)
            ys.append(gates[:, head * 3 + 1:head * 3 + 2] * o_slc[rows]
                      + gates[:, head * 3 + 2:head * 3 + 3] * o_win[rows])
        cols = slice(r * LANES, (r + 1) * LANES)
        y_ref[0, :, cols] = (ocmp_ref[0, :, cols] + jnp.where(lane < HEAD_DIM, ys[0], ys[1])).astype(y_ref.dtype)


def _nsa_attention(tab, proj3, sel, et, gates3, ocmp, col_blocks):
    B, S, _ = proj3.shape
    tq = NSA_TQ
    assert WINDOW % tq == 0 and WINDOW // tq >= 2 and S % tq == 0
    n_win = WINDOW // tq
    ks_c, vs_c, kw_c, vw_c = col_blocks
    rows = NSA_HEADS * tq

    def kv_spec(c):
        return pl.BlockSpec((1, S, LANES), lambda b, i: (b, 0, c))

    return pl.pallas_call(
        _nsa_kernel,
        grid=(B, S // tq),
        in_specs=[pl.BlockSpec(memory_space=pltpu.SMEM),
                  pl.BlockSpec((1, tq, 4 * LANES), lambda b, i: (b, i, 0)),
                  kv_spec(ks_c), kv_spec(vs_c), kv_spec(kw_c), kv_spec(vw_c),
                  pl.BlockSpec((1, NSA_GROUPS, tq, LANES), lambda b, i: (b, 0, i, 0)),
                  pl.BlockSpec((S, LANES), lambda b, i: (0, 0)),
                  pl.BlockSpec((1, tq, LANES), lambda b, i: (b, i, 0)),
                  pl.BlockSpec((1, tq, 4 * LANES), lambda b, i: (b, i, 0))],
        out_specs=pl.BlockSpec((1, tq, 4 * LANES), lambda b, i: (b, i, 0)),
        out_shape=jax.ShapeDtypeStruct((B, S, 4 * LANES), BF16),
        scratch_shapes=[pltpu.VMEM((rows, (n_win + 1) * tq), F32),
                        pltpu.VMEM((rows, LANES), F32),
                        pltpu.VMEM((rows, LANES), F32),
                        pltpu.VMEM((rows, LANES), F32),
                        pltpu.VMEM((rows, LANES), F32)],
        compiler_params=_params("arbitrary", "arbitrary"),
        name="nsa",
    )(tab, proj3, proj3, proj3, proj3, proj3, sel, et, gates3, ocmp)


def _diff_kernel(tab_ref, q_ref, k_ref, v_ref, lam_ref, g_ref, y_ref, tb_ref, m_ref, l_ref, acc_ref,
                 *, lambda_init):
    tq = q_ref.shape[1]
    h = pl.program_id(1)
    qt = pl.program_id(2)

    @pl.when(qt == 0)
    def _():
        ti = lax.broadcasted_iota(jnp.int32, (tq, tq), 0)
        ki = lax.broadcasted_iota(jnp.int32, (tq, tq), 1)
        head = NSA_HEADS + h
        far_bias = tab_ref[REL_BUCKETS - 1, head]
        tb_ref[:, 0:tq] = _rel_bias(ti - ki + tq, tab_ref, head, far_bias)
        tb_ref[:, tq:2 * tq] = jnp.where(ti >= ki, _rel_bias(ti - ki, tab_ref, head, far_bias), NEG_INF)

    lane = lax.broadcasted_iota(jnp.int32, (tq, LANES), 1)
    qb = q_ref[0]
    zero = jnp.zeros_like(qb)
    q2 = jnp.concatenate([jnp.where(lane < HEAD_DIM, qb, zero), jnp.where(lane >= HEAD_DIM, qb, zero)], axis=0)

    def score(k0, width):
        return _dot_nt(q2, _rows(k_ref, k0, width, tq))

    def near_bias(c):
        tb = tb_ref[:, c:c + LANES]
        return jnp.concatenate([tb, tb], axis=0)

    def segments(fn):
        _causal_segments(fn, qt, tq, DIFF_FAR_KEYS, near_bias)

    a = _two_pass_attention(segments, score, lambda k0, width: _rows(v_ref, k0, width, tq), m_ref, l_ref, acc_ref)
    lq1, lk1, lq2, lk2 = lam_ref[0:1, :], lam_ref[1:2, :], lam_ref[2:3, :], lam_ref[3:4, :]
    lam = (jnp.exp(jnp.sum(lq1 * lk1, axis=-1, keepdims=True))
           - jnp.exp(jnp.sum(lq2 * lk2, axis=-1, keepdims=True)) + lambda_init)
    o = a[:tq] - lam * a[tq:]
    o = o * lax.rsqrt(jnp.mean(o * o, axis=-1, keepdims=True) + LN_EPS) * g_ref[...]
    y_ref[0] = (o * (1.0 - lambda_init)).astype(y_ref.dtype)


def _diff_attention(tab, proj3, lam4, subln_g, col_blocks, lambda_init):
    B, S, _ = proj3.shape
    tq = min(DIFF_TQ, S)
    q_c, k_c, v_c = col_blocks
    return pl.pallas_call(
        functools.partial(_diff_kernel, lambda_init=lambda_init),
        grid=(B, DIFF_HEADS, S // tq),
        in_specs=[pl.BlockSpec(memory_space=pltpu.SMEM),
                  pl.BlockSpec((1, tq, LANES), lambda b, h, i: (b, i, q_c + h)),
                  pl.BlockSpec((1, S, LANES), lambda b, h, i: (b, 0, k_c + h)),
                  pl.BlockSpec((1, S, LANES), lambda b, h, i: (b, 0, v_c + h)),
                  pl.BlockSpec((SUBLANES, HEAD_DIM), lambda b, h, i: (0, 0)),
                  pl.BlockSpec((1, LANES), lambda b, h, i: (0, 0))],
        out_specs=pl.BlockSpec((1, tq, LANES), lambda b, h, i: (b, i, h)),
        out_shape=jax.ShapeDtypeStruct((B, S, DIFF_HEADS * LANES), BF16),
        scratch_shapes=[pltpu.VMEM((tq, 2 * tq), F32),
                        pltpu.VMEM((2 * tq, LANES), F32),
                        pltpu.VMEM((2 * tq, LANES), F32),
                        pltpu.VMEM((2 * tq, LANES), F32)],
        compiler_params=_params("arbitrary", "arbitrary", "arbitrary"),
        name="diff",
    )(tab, proj3, proj3, proj3, lam4, subln_g)


def _merge_kernel(x_ref, yn_ref, yd_ref, wgn_ref, wgd_ref, wbn_ref, wbd_ref, wo_ref, g_ref, b_ref, o_ref,
                  *, alpha):
    x = x_ref[...]
    xb = x.astype(BF16)
    merged = (jax.nn.sigmoid(_dot(xb, wgn_ref[...])) * _dot(yn_ref[...], wbn_ref[...])
              + jax.nn.sigmoid(_dot(xb, wgd_ref[...])) * _dot(yd_ref[...], wbd_ref[...]))
    z = alpha * x + _dot(merged.astype(BF16), wo_ref[...])
    o_ref[...] = _layer_norm(z, g_ref[...], b_ref[...])


def _merge(x2d, y_nsa, y_diff, wgn, wgd, wbn, wbd, wo, ln_g, ln_b, alpha):
    T, D = x2d.shape
    const = lambda shape: pl.BlockSpec(shape, lambda i: (0, 0))
    return pl.pallas_call(
        functools.partial(_merge_kernel, alpha=alpha),
        grid=(T // ROW_TILE,),
        in_specs=[pl.BlockSpec((ROW_TILE, D), lambda i: (i, 0)),
                  pl.BlockSpec((ROW_TILE, y_nsa.shape[1]), lambda i: (i, 0)),
                  pl.BlockSpec((ROW_TILE, y_diff.shape[1]), lambda i: (i, 0)),
                  const(wgn.shape), const(wgd.shape), const(wbn.shape), const(wbd.shape), const(wo.shape),
                  const((1, D)), const((1, D))],
        out_specs=pl.BlockSpec((ROW_TILE, D), lambda i: (i, 0)),
        out_shape=jax.ShapeDtypeStruct((T, D), F32),
        compiler_params=_params("arbitrary"),
        name="merge",
    )(x2d, y_nsa, y_diff, wgn, wgd, wbn, wbd, wo, ln_g, ln_b)


def _ffn_kernel(x_ref, halo_ref, p_ref, w_ref, cw_ref, cb_ref, wd_ref, g_ref, b_ref, wpg_ref, wpp_ref,
                o_ref, acc_ref, *, alpha, tiles_per_seq):
    tm = x_ref.shape[0]
    d_ff = wd_ref.shape[0]
    x = x_ref[...]
    xb = x.astype(BF16)
    keep = (pl.program_id(0) % tiles_per_seq != 0).astype(F32)
    hb = halo_ref[...].astype(BF16)
    for c in range(0, d_ff, FF_CHUNK):
        cols = slice(c, c + FF_CHUNK)
        wg = w_ref[:, cols]
        gm = _dot(xb, wg)
        gh = _dot(hb, wg) * keep
        um = _dot(xb, w_ref[:, d_ff + c:d_ff + c + FF_CHUNK])
        gext = jnp.concatenate([gh, gm], axis=0)
        g1 = pltpu.roll(gext, 1, 0)[HALO:]
        g2 = pltpu.roll(gext, 2, 0)[HALO:]
        conv =cb_ref[:, cols] + cw_ref[0:1, cols] * g2 + cw_ref[1:2, cols] * g1 + cw_ref[2:3, cols] * gm
        act = (_gelu(conv) * um).astype(BF16)
        down = _dot(act, wd_ref[cols, :])
        if c == 0:
            acc_ref[...] = down
        else:
            acc_ref[...] += down
    x2 = _layer_norm(alpha * x + acc_ref[...], g_ref[...], b_ref[...])
    gate = jax.nn.sigmoid(_dot(x2.astype(BF16), wpg_ref[...]))
    o_ref[...] = x2 + gate * _dot(p_ref[...].astype(BF16), wpp_ref[...])


def _ffn(x1, p2d, w_in, cw, cb, wd, ln_g, ln_b, wpg, wpp, alpha, seq):
    T, D = x1.shape
    tm = FFN_TILE
    assert seq % tm == 0 and wd.shape[0] % FF_CHUNK == 0
    hb = tm // HALO
    const = lambda shape: pl.BlockSpec(shape, lambda i: (0, 0))
    return pl.pallas_call(
        functools.partial(_ffn_kernel, alpha=alpha, tiles_per_seq=seq // tm),
        grid=(T // tm,),
        in_specs=[pl.BlockSpec((tm, D), lambda i: (i, 0)),
                  pl.BlockSpec((HALO, D), lambda i: (jnp.maximum(i * hb - 1, 0), 0)),
                  pl.BlockSpec((tm, p2d.shape[1]), lambda i: (i, 0)),
                  const(w_in.shape), const(cw.shape), const(cb.shape), const(wd.shape),
                  const((1, D)), const((1, D)), const(wpg.shape), const(wpp.shape)],
        out_specs=pl.BlockSpec((tm, D), lambda i: (i, 0)),
        out_shape=jax.ShapeDtypeStruct((T, D), F32),
        scratch_shapes=[pltpu.VMEM((tm, D), F32)],
        compiler_params=_params("arbitrary"),
        name="ffn",
    )(x1, x1, p2d, w_in, cw, cb, wd, ln_g, ln_b, wpg, wpp)


def _slc_from_cmp_t(ncp, n_slc):
    ratio = SLC_BLOCK // CMP_STRIDE
    span = CMP_BLOCK // CMP_STRIDE
    mat = np.zeros((n_slc, ncp), np.float32)
    for j in range(n_slc):
        for m in range(ratio):
            for n in range(span):
                i = ratio * j + m - n
                if 0 <= i < ncp - 1:
                    mat[j, i] += 1.0
    return mat


def _layer(x, p_l, w_in, pe_k, w1_k, w2_k, pe_v, w1_v, w2_v, lq1, lk1, lq2, lk2, subln_g, w_bn, w_bd, w_out,
           ln1_g, ln1_b, w_ffn_in, conv_w, conv_b, w_down, ln2_g, ln2_b, w_pp, w_pg, tab, lambda_init,
           alpha):
    B, S, D = x.shape
    T = B * S
    ncp = S // CMP_STRIDE
    n_slc = S // SLC_BLOCK
    q_w = NSA_HEADS * HEAD_DIM
    kv_w = NSA_GROUPS * HEAD_DIM
    dqk_w = DIFF_HEADS * 2 * HEAD_DIM
    sizes = (q_w,) + (kv_w,) * 6 + (NSA_HEADS * 3, dqk_w, dqk_w, dqk_w, D, D)
    offs = np.concatenate([[0], np.cumsum(sizes)])
    col = lambda i: w_in[:, int(offs[i]):int(offs[i + 1])]
    scale = HEAD_DIM ** -0.5

    n_idx = np.arange(q_w)
    perm = (NSA_REP * ((n_idx % LANES) // HEAD_DIM) + n_idx // LANES) * HEAD_DIM + n_idx % HEAD_DIM
    w_main = jnp.concatenate([col(0)[:, perm] * scale] + [col(i) for i in range(3, 7)]
                             + [col(8) * scale, col(9), col(10)], axis=1).astype(BF16)
    w_cmp = jnp.concatenate([col(1), col(2)], axis=1).astype(BF16)
    w_gate = jnp.pad(col(7), ((0, 0), (0, LANES - NSA_HEADS * 3))).astype(BF16)
    x2d = x.reshape(T, D)
    proj, k_cmp, v_cmp, gates = _proj(x2d, w_main, w_cmp, w_gate)
    proj3 = proj.reshape(B, S, proj.shape[1])
    gates3 = gates.reshape(B, S, LANES)
    c_kslc, c_vslc, c_kwin, c_vwin = (q_w // LANES + i for i in range(4))
    c_dq = q_w // LANES + 4
    c_dk = c_dq + DIFF_HEADS
    c_dv = c_dk + DIFF_HEADS

    cw = CMP_STRIDE * kv_w
    rep = lambda a: jnp.broadcast_to(a.reshape(2, 2, CMP_STRIDE, 1, HEAD_DIM, -1),
                                     (2, 2, CMP_STRIDE, NSA_GROUPS, HEAD_DIM, a.shape[-1]))
    pe = rep(jnp.stack([pe_k, pe_v])[..., None]).reshape(2, 2, 1, cw)
    w1 = rep(jnp.stack([w1_k, w1_v])).reshape(2, 2, cw, CMP_HIDDEN).astype(BF16)
    w2 = jnp.stack([w2_k, w2_v])
    w2p = jnp.stack([jnp.pad(w2, ((0, 0), (0, 0), (g * HEAD_DIM, LANES - (g + 1) * HEAD_DIM)))
                     for g in range(NSA_GROUPS)], axis=1).astype(BF16)
    cmp_kv = _compress(k_cmp.reshape(B, ncp, cw), v_cmp.reshape(B, ncp, cw), pe, w1, w2p)

    matt = jnp.asarray(_slc_from_cmp_t(ncp, n_slc), BF16)
    ocmp, sel = _cmp_attention(tab, proj3, cmp_kv, gates3, matt, n_slc)
    et_np = np.zeros((S, LANES), np.float32)
    et_np[np.arange(S), np.arange(S) // SLC_BLOCK] = 2.0 ** MASK_EXP
    y_nsa = _nsa_attention(tab, proj3, sel, jnp.asarray(et_np, BF16), gates3, ocmp,
                           (c_kslc, c_vslc, c_kwin, c_vwin))

    lam4 = jnp.pad(jnp.stack([lq1, lk1, lq2, lk2]), ((0, SUBLANES - 4), (0, 0)))
    y_diff = _diff_attention(tab, proj3, lam4, subln_g.reshape(1, LANES), (c_dq, c_dk, c_dv), lambda_init)

    x1 = _merge(x2d, y_nsa.reshape(T, q_w), y_diff.reshape(T, dqk_w),
                col(11).astype(BF16), col(12).astype(BF16), w_bn[perm].astype(BF16), w_bd.astype(BF16),
                w_out.astype(BF16), ln1_g.reshape(1, D), ln1_b.reshape(1, D), alpha)

    cw = jnp.pad(conv_w, ((0, SUBLANES - CONV_WIDTH), (0, 0)))
    out = _ffn(x1, p_l.reshape(T, p_l.shape[-1]), w_ffn_in.astype(BF16), cw, conv_b.reshape(1, -1),
               w_down.astype(BF16), ln2_g.reshape(1, D), ln2_b.reshape(1, D),
               w_pg.astype(BF16), w_pp.astype(BF16), alpha, S)
    return out.reshape(B, S, D)


def kernel(x, p, w_in, nsa_cmp_pe_k, nsa_cmp_w1_k, nsa_cmp_w2_k, nsa_cmp_pe_v, nsa_cmp_w1_v, nsa_cmp_w2_v, diff_lambda_q1, diff_lambda_k1, diff_lambda_q2, diff_lambda_k2, diff_subln_g, w_branch_nsa, w_branch_diff, w_out, ln1_g, ln1_b, w_ffn_in, ffn_conv_w, ffn_conv_b, w_ffn_down, ln2_g, ln2_b, w_ple_proj, w_ple_gate, rel_bias_table):
    depth = w_in.shape[0]
    alpha = (2.0 * depth) ** 0.25
    for l in range(depth):
        lambda_init = 0.8 - 0.6 * math.exp(-0.3 * l)
        x = _layer(x, p[l], w_in[l], nsa_cmp_pe_k[l], nsa_cmp_w1_k[l], nsa_cmp_w2_k[l], nsa_cmp_pe_v[l],
                   nsa_cmp_w1_v[l], nsa_cmp_w2_v[l], diff_lambda_q1[l], diff_lambda_k1[l], diff_lambda_q2[l],
                   diff_lambda_k2[l], diff_subln_g[l], w_branch_nsa[l], w_branch_diff[l], w_out[l], ln1_g[l],
                   ln1_b[l], w_ffn_in[l], ffn_conv_w[l], ffn_conv_b[l], w_ffn_down[l], ln2_g[l], ln2_b[l],
                   w_ple_proj[l], w_ple_gate[l], rel_bias_table, lambda_init, alpha)
    return x
```

```python
import functools
import math

import jax
import jax.numpy as jnp
import numpy as np
from jax import lax
from jax.experimental import pallas as pl
from jax.experimental.pallas import tpu as pltpu

F32 = jnp.float32
BF16 = jnp.bfloat16

NSA_HEADS = 8
NSA_GROUPS = 2
NSA_REP = NSA_HEADS // NSA_GROUPS
HEAD_DIM = 64
CMP_BLOCK = 32
CMP_STRIDE = 16
CMP_HIDDEN = 256
SLC_BLOCK = 64
SLC_TOPK = 16
SLC_LOCAL = 2
WINDOW = 512
DIFF_HEADS = 4
REL_BUCKETS = 32
REL_MAX_EXACT = 16
REL_MAX_DIST = 128
D_FF = 2816
CONV_WIDTH = 3
LN_EPS = 1e-5
NEG_INF = -1e30
BIG = 1e30
MASK_EXP = 100

LANES = 128
SUBLANES = 8
VMEM_LIMIT = 56 * 1024 * 1024

CMP_TQ = 256
NSA_TQ = 256
DIFF_TQ = 512
NSA_FAR_KEYS = 512
DIFF_FAR_KEYS = 1024
NSA_ONLINE_KEYS = 512
DIFF_ONLINE_KEYS = 1024
ROW_TILE = 512
FFN_TILE = 512
FF_CHUNK = 256
HALO = 16


def _rel_breakpoints():
    n = np.arange(0, 4 * REL_MAX_DIST)
    large = REL_MAX_EXACT + (np.log(np.maximum(n, 1).astype(np.float32) / REL_MAX_EXACT)
                             / np.float32(math.log(REL_MAX_DIST / REL_MAX_EXACT))
                             * (REL_BUCKETS - REL_MAX_EXACT)).astype(np.int32)
    bucket = np.where(n < REL_MAX_EXACT, n, np.minimum(large, REL_BUCKETS - 1))
    assert np.all(np.diff(bucket) >= 0)
    return [int(np.argmax(bucket >= b)) for b in range(1, REL_BUCKETS)]


REL_BREAKS = _rel_breakpoints()


def _dot(a, b):
    return jnp.dot(a, b, preferred_element_type=F32)


def _dot_nt(a, b):
    return lax.dot_general(a, b, (((1,), (1,)), ((), ())), preferred_element_type=F32)


def _rel_bias(dist, tab_ref, head, shift):
    val = jnp.full(dist.shape, tab_ref[0, head] - shift, F32)
    for b, brk in enumerate(REL_BREAKS, start=1):
        val = jnp.where(dist >= brk, tab_ref[b, head] - shift, val)
    return val


def _gelu(x):
    c = math.sqrt(2.0 / math.pi)
    half = 0.5 * x
    return half + half * jnp.tanh(x * (c + (c * 0.044715) * (x * x)))


def _layer_norm(z, g, b):
    mu = jnp.mean(z, axis=-1, keepdims=True)
    zc = z - mu
    var = jnp.mean(zc * zc, axis=-1, keepdims=True)
    return zc * lax.rsqrt(var + LN_EPS) * g + b


def _params(*sem):
    return pltpu.CompilerParams(dimension_semantics=sem, vmem_limit_bytes=VMEM_LIMIT)


def _proj_kernel(x_ref, wm_ref, wc_ref, wg_ref, om_ref, ok_ref, ov_ref, og_ref):
    xb = x_ref[...].astype(BF16)
    n = wm_ref.shape[1]
    for c in range(0, n, 2 * LANES):
        w = min(2 * LANES, n - c)
        om_ref[:, c:c + w] = _dot(xb, wm_ref[:, c:c + w]).astype(om_ref.dtype)
    kv = _dot(xb, wc_ref[...])
    ok_ref[...] = kv[:, :LANES].astype(ok_ref.dtype)
    ov_ref[...] = kv[:, LANES:].astype(ov_ref.dtype)
    og_ref[...] = _dot(xb, wg_ref[...])


def _proj(x2d, w_main, w_cmp, w_gate):
    T, D = x2d.shape
    n = w_main.shape[1]
    row = lambda width: pl.BlockSpec((ROW_TILE, width), lambda i: (i, 0))
    const = lambda width: pl.BlockSpec((D, width), lambda i: (0, 0))
    return pl.pallas_call(
        _proj_kernel,
        grid=(T // ROW_TILE,),
        in_specs=[row(D), const(n), const(2 * LANES), const(LANES)],
        out_specs=[row(n), row(LANES), row(LANES), row(LANES)],
        out_shape=[jax.ShapeDtypeStruct((T, n), BF16), jax.ShapeDtypeStruct((T, LANES), BF16),
                   jax.ShapeDtypeStruct((T, LANES), BF16), jax.ShapeDtypeStruct((T, LANES), F32)],
        compiler_params=_params("arbitrary"),
        name="proj",
    )(x2d, w_main, w_cmp, w_gate)


def _compress_kernel(xk_ref, xv_ref, pe_ref, w1_ref, w2_ref, o_ref):
    ncp, cw = xk_ref.shape[1], xk_ref.shape[2]
    lane = lax.broadcasted_iota(jnp.int32, (ncp, cw), 1)
    group = jnp.bitwise_and(jnp.right_shift(lane, int(math.log2(HEAD_DIM))), NSA_GROUPS - 1)
    for s, x_ref in enumerate((xk_ref, xv_ref)):
        x = x_ref[0].astype(F32)
        xa = x + pe_ref[s, 0]
        xb = x + pe_ref[s, 1]
        acc = jnp.zeros((ncp, LANES), F32)
        for g in range(NSA_GROUPS):
            a = _dot(jnp.where(group == g, xa, 0.0).astype(BF16), w1_ref[s, 0])
            b = _dot(jnp.where(group == g, xb, 0.0).astype(BF16), w1_ref[s, 1])
            h = a + pltpu.roll(b, ncp - 1, 0)
            acc = acc + _dot(_gelu(h).astype(BF16), w2_ref[s, g])
        o_ref[s, 0] = acc.astype(o_ref.dtype)


def _compress(xk, xv, pe, w1, w2):
    B, ncp, cw = xk.shape
    x_spec = pl.BlockSpec((1, ncp, cw), lambda b: (b, 0, 0))
    const = lambda shape: pl.BlockSpec(shape, lambda b: (0, 0, 0, 0))
    return pl.pallas_call(
        _compress_kernel,
        grid=(B,),
        in_specs=[x_spec, x_spec, const(pe.shape), const(w1.shape), const(w2.shape)],
        out_specs=pl.BlockSpec((2, 1, ncp, LANES), lambda b: (0, b, 0, 0)),
        out_shape=jax.ShapeDtypeStruct((2, B, ncp, LANES), BF16),
        compiler_params=_params("arbitrary"),
        name="compress",
    )(xk, xv, pe, w1, w2)


def _cmp_kernel(tab_ref, q_ref, kc_ref, vc_ref, gate_ref, matt_ref, ocmp_ref, sel_ref, bias_ref, *, n_slc):
    tq = q_ref.shape[1]
    ncp = kc_ref.shape[2]
    q0 = pl.program_id(1) * tq
    lane = lax.broadcasted_iota(jnp.int32, (tq, LANES), 1)
    t_idx = q0 + lax.broadcasted_iota(jnp.int32, (tq, ncp), 0)
    c_idx = lax.broadcasted_iota(jnp.int32, (tq, ncp), 1)
    dist = t_idx - (c_idx * CMP_STRIDE + (CMP_BLOCK - 1))
    valid = dist >= 0
    near = jnp.clip(dist, 0, LANES - 1)

    @pl.when((pl.program_id(0) == 0) & (pl.program_id(1) == 0))
    def _():
        for head in range(NSA_HEADS):
            bias_ref[head] = _rel_bias(lane, tab_ref, head, 0.0)

    def masked_bias(head):
        table = bias_ref[head]
        cols = [jnp.take_along_axis(table, near[:, c:c + LANES], axis=1) for c in range(0, ncp, LANES)]
        return jnp.where(valid, jnp.concatenate(cols, axis=1), NEG_INF)

    gates = jax.nn.sigmoid(gate_ref[0])
    eye = (lax.broadcasted_iota(jnp.int32, (tq, tq), 0)
           == lax.broadcasted_iota(jnp.int32, (tq, tq), 1)).astype(BF16)
    kc = kc_ref[0, 0]
    vc = vc_ref[0, 0]
    jrow_i = lax.broadcasted_iota(jnp.int32, (n_slc, tq), 0)
    cur = jnp.right_shift(q0 + lax.broadcasted_iota(jnp.int32, (n_slc, tq), 1), int(math.log2(SLC_BLOCK)))
    forced = (jrow_i == 0) | ((cur - jrow_i >= 0) & (cur - jrow_i < SLC_LOCAL))
    blk_valid = jrow_i <= cur
    jrow = jrow_i.astype(F32)
    gated = []
    for g in range(NSA_GROUPS):
        lane_g = (lane >= HEAD_DIM * g) & (lane < HEAD_DIM * (g + 1))
        psum = jnp.zeros((tq, ncp), F32)
        outs = []
        for r in range(NSA_REP):
            head = g * NSA_REP + r
            qb = q_ref[0, :, r * LANES:(r + 1) * LANES]
            qb = jnp.where(lane_g, qb, jnp.zeros_like(qb))
            logit = _dot_nt(qb, kc) + masked_bias(head)
            m = jnp.max(logit, axis=-1, keepdims=True)
            e = jnp.exp(logit - m)
            p = jnp.where(valid, e * (1.0 / jnp.sum(e, axis=-1, keepdims=True)), 0.0)
            psum = psum + p
            o = _dot(p.astype(BF16), vc)
            outs.append(gates[:, head * 3:head * 3 + 1] * o)
        gated.append(outs)
        hi = psum.astype(BF16)
        lo = (psum - hi.astype(F32)).astype(BF16)
        p_slc = _dot_nt(matt_ref[...], hi) + _dot_nt(matt_ref[...], lo)
        score = jnp.where(forced, BIG, jnp.where(blk_valid, p_slc, NEG_INF))
        sel = jnp.zeros((n_slc, tq), F32)
        for _ in range(min(SLC_TOPK, n_slc)):
            mx = jnp.max(score, axis=0, keepdims=True)
            idx = jnp.min(jnp.where(score == mx, jrow, float(n_slc)), axis=0, keepdims=True)
            hit = jrow == idx
            sel = jnp.where(hit, 1.0, sel)
            score = jnp.where(hit, -3.0e38, score)
        selm1 = (sel - 1.0).astype(BF16)
        if n_slc < LANES:
            selm1 = jnp.concatenate([selm1, jnp.zeros((LANES - n_slc, tq), BF16)], axis=0)
        sel_ref[0, g] = _dot_nt(eye, selm1).astype(sel_ref.dtype)
    for r in range(NSA_REP):
        ocmp_ref[0, :, r * LANES:(r + 1) * LANES] = jnp.where(lane < HEAD_DIM, gated[0][r], gated[1][r])


def _cmp_attention(tab, proj3, cmp_kv, gates3, matt, n_slc):
    B, S, _ = proj3.shape
    ncp = cmp_kv.shape[2]
    tq = CMP_TQ
    return pl.pallas_call(
        functools.partial(_cmp_kernel, n_slc=n_slc),
        grid=(B, S // tq),
        in_specs=[pl.BlockSpec(memory_space=pltpu.SMEM),
                  pl.BlockSpec((1, tq, 4 * LANES), lambda b, i: (b, i, 0)),
                  pl.BlockSpec((1, 1, ncp, LANES), lambda b, i: (0, b, 0, 0)),
                  pl.BlockSpec((1, 1, ncp, LANES), lambda b, i: (1, b, 0, 0)),
                  pl.BlockSpec((1, tq, LANES), lambda b, i: (b, i, 0)),
                  pl.BlockSpec((n_slc, ncp), lambda b, i: (0, 0))],
        out_specs=[pl.BlockSpec((1, tq, 4 * LANES), lambda b, i: (b, i, 0)),
                   pl.BlockSpec((1, NSA_GROUPS, tq, LANES), lambda b, i: (b, 0, i, 0))],
        out_shape=[jax.ShapeDtypeStruct((B, S, 4 * LANES), F32),
                   jax.ShapeDtypeStruct((B, NSA_GROUPS, S, LANES), BF16)],
        scratch_shapes=[pltpu.VMEM((NSA_HEADS, tq, LANES), F32)],
        compiler_params=_params("arbitrary", "arbitrary"),
        name="cmp",
    )(tab, proj3, cmp_kv, cmp_kv, gates3, matt)


def _online_attention(segments, score, value, m_ref, l_ref, acc_ref, sub_keys):
    m_ref[...] = jnp.full(m_ref.shape, NEG_INF, F32)
    l_ref[...] = jnp.zeros(l_ref.shape, F32)
    acc_ref[...] = jnp.zeros(acc_ref.shape, F32)

    def step(k0, width, bias):
        for j in range(0, width, sub_keys):
            w = min(sub_keys, width - j)
            s = score(k0 + j, w)
            cols = []
            for c in range(0, w, LANES):
                b = None if bias is None else bias(j + c)
                cols.append(s[:, c:c + LANES] if b is None else s[:, c:c + LANES] + b)
            m_old = m_ref[...]
            m_new = jnp.maximum(m_old, jnp.max(functools.reduce(jnp.maximum, cols), axis=-1, keepdims=True))
            alpha = jnp.exp(m_old - m_new)
            ps = [jnp.exp(col - m_new) for col in cols]
            l_ref[...] = alpha * l_ref[...] + functools.reduce(lambda a, b: a + b, ps)
            acc_ref[...] = alpha * acc_ref[...] + _dot(
                jnp.concatenate([p.astype(BF16) for p in ps], axis=1), value(k0 + j, w))
            m_ref[...] = m_new

    segments(step)
    return acc_ref[...] / jnp.sum(l_ref[...], axis=-1, keepdims=True)


def _causal_segments(fn, qt, tq, far_keys, near_bias):
    n_far = jnp.maximum(qt - 1, 0)
    far_tiles = far_keys // tq
    step = far_tiles * tq

    def far_body(i, carry):
        fn(i * step, step, None)
        return carry

    lax.fori_loop(0, lax.div(n_far, far_tiles), far_body, 0)
    for rem in range(far_tiles):
        lead = rem * tq
        bias = (lambda c, lead=lead: None if c < lead else near_bias(c - lead))
        pl.when((qt >= 1) & (lax.rem(n_far, far_tiles) == rem))(
            functools.partial(fn, (qt - 1 - rem) * tq, lead + 2 * tq, bias))
    pl.when(qt == 0)(functools.partial(fn, 0, tq, lambda c: near_bias(tq + c)))


def _key_slice(k0, width, tq):
    return pl.ds(k0 if isinstance(k0, int) else pl.multiple_of(k0, tq), width)


def _rows(ref, k0, width, tq):
    return ref[0, _key_slice(k0, width, tq), :]


def _nsa_kernel(tab_ref, q_ref, ks_ref, vs_ref, kw_ref, vw_ref, sel_ref, et_ref, gate_ref, ocmp_ref,
                y_ref, tb_ref, m_ref, l_ref, acc_ref, part_ref):
    tq = q_ref.shape[1]
    n_win = WINDOW // tq
    qt = pl.program_id(1)

    @pl.when((pl.program_id(0) == 0) & (qt == 0))
    def _():
        ti = lax.broadcasted_iota(jnp.int32, (tq, tq), 0)
        ki = lax.broadcasted_iota(jnp.int32, (tq, tq), 1)
        for g in range(NSA_GROUPS):
            for r in range(NSA_REP):
                head = g * NSA_REP + r
                far_bias = tab_ref[REL_BUCKETS - 1, head]
                rows = slice(head * tq, (head + 1) * tq)
                tb_ref[rows, 0:tq] = jnp.where(ti < ki, 0.0, NEG_INF)
                for j in range(2, n_win):
                    tb_ref[rows, (n_win - j) * tq:(n_win - j + 1) * tq] = jnp.zeros((tq, tq), F32)
                tb_ref[rows, (n_win - 1) * tq:n_win * tq] = _rel_bias(ti - ki + tq, tab_ref, head, far_bias)
                tb_ref[rows, n_win * tq:(n_win + 1) * tq] = jnp.where(
                    ti >= ki, _rel_bias(ti - ki, tab_ref, head, far_bias), NEG_INF)

    lane = lax.broadcasted_iota(jnp.int32, (tq, LANES), 1)
    gates = jax.nn.sigmoid(gate_ref[0])

    q_parts, sel_parts = [], []
    for g in range(NSA_GROUPS):
        lane_g = (lane >= HEAD_DIM * g) & (lane < HEAD_DIM * (g + 1))
        for r in range(NSA_REP):
            qb = q_ref[0, :, r * LANES:(r + 1) * LANES]
            q_parts.append(jnp.where(lane_g, qb, jnp.zeros_like(qb)))
            sel_parts.append(sel_ref[0, g])
    q_all = jnp.concatenate(q_parts, axis=0)
    qs_all = jnp.concatenate([q_all, jnp.concatenate(sel_parts, axis=0)], axis=1)

    def bias_from(col0):
        return lambda c: tb_ref[:, col0 + c:col0 + c + LANES]

    def slc_score(k0, width):
        et = et_ref[_key_slice(k0, width, tq), :]
        return _dot_nt(qs_all, jnp.concatenate([_rows(ks_ref, k0, width, tq), et], axis=1))

    def slc_segments(fn):
        _causal_segments(fn, qt, tq, NSA_FAR_KEYS, bias_from((n_win - 1) * tq))

    part_ref[...] = _online_attention(slc_segments, slc_score, lambda k0, width: _rows(vs_ref, k0, width, tq),
                                      m_ref, l_ref, acc_ref, NSA_ONLINE_KEYS)

    def win_score(k0, width):
        return _dot_nt(q_all, _rows(kw_ref, k0, width, tq))

    def win_segments(fn):
        for n in range(n_win):
            pl.when(qt == n)(functools.partial(fn, 0, (n + 1) * tq, bias_from((n_win - n) * tq)))
        pl.when(qt >= n_win)(functools.partial(fn, (qt - n_win) * tq, (n_win + 1) * tq, bias_from(0)))

    o_win = _online_attention(win_segments, win_score, lambda k0, width: _rows(vw_ref, k0, width, tq),
                              m_ref, l_ref, acc_ref, NSA_ONLINE_KEYS)
    o_slc = part_ref[...]

    for r in range(NSA_REP):
        ys = []
        for g in range(NSA_GROUPS):
            head = g * NSA_REP + r
            rows = slice(head * tq, (head + 1) * tq)
            ys.append(gates[:, head * 3 + 1:head * 3 + 2] * o_slc[rows]
                      + gates[:, head * 3 + 2:head * 3 + 3] * o_win[rows])
        cols = slice(r * LANES, (r + 1) * LANES)
        y_ref[0, :, cols] = (ocmp_ref[0, :, cols] + jnp.where(lane < HEAD_DIM, ys[0], ys[1])).astype(y_ref.dtype)


def _nsa_attention(tab, proj3, sel, et, gates3, ocmp, col_blocks):
    B, S, _ = proj3.shape
    tq = NSA_TQ
    assert WINDOW % tq == 0 and WINDOW // tq >= 2 and S % tq == 0
    n_win = WINDOW // tq
    ks_c, vs_c, kw_c, vw_c = col_blocks
    rows = NSA_HEADS * tq

    def kv_spec(c):
        return pl.BlockSpec((1, S, LANES), lambda b, i: (b, 0, c))

    return pl.pallas_call(
        _nsa_kernel,
        grid=(B, S // tq),
        in_specs=[pl.BlockSpec(memory_space=pltpu.SMEM),
                  pl.BlockSpec((1, tq, 4 * LANES), lambda b, i: (b, i, 0)),
                  kv_spec(ks_c), kv_spec(vs_c), kv_spec(kw_c), kv_spec(vw_c),
                  pl.BlockSpec((1, NSA_GROUPS, tq, LANES), lambda b, i: (b, 0, i, 0)),
                  pl.BlockSpec((S, LANES), lambda b, i: (0, 0)),
                  pl.BlockSpec((1, tq, LANES), lambda b, i: (b, i, 0)),
                  pl.BlockSpec((1, tq, 4 * LANES), lambda b, i: (b, i, 0))],
        out_specs=pl.BlockSpec((1, tq, 4 * LANES), lambda b, i: (b, i, 0)),
        out_shape=jax.ShapeDtypeStruct((B, S, 4 * LANES), BF16),
        scratch_shapes=[pltpu.VMEM((rows, (n_win + 1) * tq), F32),
                        pltpu.VMEM((rows, LANES), F32),
                        pltpu.VMEM((rows, LANES), F32),
                        pltpu.VMEM((rows, LANES), F32),
                        pltpu.VMEM((rows, LANES), F32)],
        compiler_params=_params("arbitrary", "arbitrary"),
        name="nsa",
    )(tab, proj3, proj3, proj3, proj3, proj3, sel, et, gates3, ocmp)


def _diff_kernel(tab_ref, q_ref, k_ref, v_ref, lam_ref, g_ref, y_ref, tb_ref, m_ref, l_ref, acc_ref,
                 *, lambda_init):
    tq = q_ref.shape[1]
    h = pl.program_id(1)
    qt = pl.program_id(2)

    @pl.when(qt == 0)
    def _():
        ti = lax.broadcasted_iota(jnp.int32, (tq, tq), 0)
        ki = lax.broadcasted_iota(jnp.int32, (tq, tq), 1)
        head = NSA_HEADS + h
        far_bias = tab_ref[REL_BUCKETS - 1, head]
        tb_ref[:, 0:tq] = _rel_bias(ti - ki + tq, tab_ref, head, far_bias)
        tb_ref[:, tq:2 * tq] = jnp.where(ti >= ki, _rel_bias(ti - ki, tab_ref, head, far_bias), NEG_INF)

    lane = lax.broadcasted_iota(jnp.int32, (tq, LANES), 1)
    qb = q_ref[0]
    zero = jnp.zeros_like(qb)
    q2 = jnp.concatenate([jnp.where(lane < HEAD_DIM, qb, zero), jnp.where(lane >= HEAD_DIM, qb, zero)], axis=0)

    def score(k0, width):
        return _dot_nt(q2, _rows(k_ref, k0, width, tq))

    def near_bias(c):
        tb = tb_ref[:, c:c + LANES]
        return jnp.concatenate([tb, tb], axis=0)

    def segments(fn):
        _causal_segments(fn, qt, tq, DIFF_FAR_KEYS, near_bias)

    a = _online_attention(segments, score, lambda k0, width: _rows(v_ref, k0, width, tq), m_ref, l_ref, acc_ref,
                          DIFF_ONLINE_KEYS)
    lq1, lk1, lq2, lk2 = lam_ref[0:1, :], lam_ref[1:2, :], lam_ref[2:3, :], lam_ref[3:4, :]
    lam = (jnp.exp(jnp.sum(lq1 * lk1, axis=-1, keepdims=True))
           - jnp.exp(jnp.sum(lq2 * lk2, axis=-1, keepdims=True)) + lambda_init)
    o = a[:tq] - lam * a[tq:]
    o = o * lax.rsqrt(jnp.mean(o * o, axis=-1, keepdims=True) + LN_EPS) * g_ref[...]
    y_ref[0] = (o * (1.0 - lambda_init)).astype(y_ref.dtype)


def _diff_attention(tab, proj3, lam4, subln_g, col_blocks, lambda_init):
    B, S, _ = proj3.shape
    tq = min(DIFF_TQ, S)
    q_c, k_c, v_c = col_blocks
    return pl.pallas_call(
        functools.partial(_diff_kernel, lambda_init=lambda_init),
        grid=(B, DIFF_HEADS, S // tq),
        in_specs=[pl.BlockSpec(memory_space=pltpu.SMEM),
                  pl.BlockSpec((1, tq, LANES), lambda b, h, i: (b, i, q_c + h)),
                  pl.BlockSpec((1, S, LANES), lambda b, h, i: (b, 0, k_c + h)),
                  pl.BlockSpec((1, S, LANES), lambda b, h, i: (b, 0, v_c + h)),
                  pl.BlockSpec((SUBLANES, HEAD_DIM), lambda b, h, i: (0, 0)),
                  pl.BlockSpec((1, LANES), lambda b, h, i: (0, 0))],
        out_specs=pl.BlockSpec((1, tq, LANES), lambda b, h, i: (b, i, h)),
        out_shape=jax.ShapeDtypeStruct((B, S, DIFF_HEADS * LANES), BF16),
        scratch_shapes=[pltpu.VMEM((tq, 2 * tq), F32),
                        pltpu.VMEM((2 * tq, LANES), F32),
                        pltpu.VMEM((2 * tq, LANES), F32),
                        pltpu.VMEM((2 * tq, LANES), F32)],
        compiler_params=_params("arbitrary", "arbitrary", "arbitrary"),
        name="diff",
    )(tab, proj3, proj3, proj3, lam4, subln_g)


def _merge_kernel(x_ref, yn_ref, yd_ref, wgn_ref, wgd_ref, wbn_ref, wbd_ref, wo_ref, g_ref, b_ref, o_ref,
                  *, alpha):
    x = x_ref[...]
    xb = x.astype(BF16)
    merged = (jax.nn.sigmoid(_dot(xb, wgn_ref[...])) * _dot(yn_ref[...], wbn_ref[...])
              + jax.nn.sigmoid(_dot(xb, wgd_ref[...])) * _dot(yd_ref[...], wbd_ref[...]))
    z = alpha * x + _dot(merged.astype(BF16), wo_ref[...])
    o_ref[...] = _layer_norm(z, g_ref[...], b_ref[...])


def _merge(x2d, y_nsa, y_diff, wgn, wgd, wbn, wbd, wo, ln_g, ln_b, alpha):
    T, D = x2d.shape
    const = lambda shape: pl.BlockSpec(shape, lambda i: (0, 0))
    return pl.pallas_call(
        functools.partial(_merge_kernel, alpha=alpha),
        grid=(T // ROW_TILE,),
        in_specs=[pl.BlockSpec((ROW_TILE, D), lambda i: (i, 0)),
                  pl.BlockSpec((ROW_TILE, y_nsa.shape[1]), lambda i: (i, 0)),
                  pl.BlockSpec((ROW_TILE, y_diff.shape[1]), lambda i: (i, 0)),
                  const(wgn.shape), const(wgd.shape), const(wbn.shape), const(wbd.shape), const(wo.shape),
                  const((1, D)), const((1, D))],
        out_specs=pl.BlockSpec((ROW_TILE, D), lambda i: (i, 0)),
        out_shape=jax.ShapeDtypeStruct((T, D), F32),
        compiler_params=_params("arbitrary"),
        name="merge",
    )(x2d, y_nsa, y_diff, wgn, wgd, wbn, wbd, wo, ln_g, ln_b)


def _ffn_kernel(x_ref, halo_ref, p_ref, w_ref, cw_ref, cb_ref, wd_ref, g_ref, b_ref, wpg_ref, wpp_ref,
                o_ref, acc_ref, *, alpha, tiles_per_seq):
    tm = x_ref.shape[0]
    d_ff = wd_ref.shape[0]
    x = x_ref[...]
    xb = x.astype(BF16)
    keep = (pl.program_id(0) % tiles_per_seq != 0).astype(F32)
    hb = halo_ref[...].astype(BF16)
    for c in range(0, d_ff, FF_CHUNK):
        cols = slice(c, c + FF_CHUNK)
        wg = w_ref[:, cols]
        gm = _dot(xb, wg)
        gh = _dot(hb, wg) * keep
        um = _dot(xb, w_ref[:, d_ff + c:d_ff + c + FF_CHUNK])
        gext = jnp.concatenate([gh, gm], axis=0)
        g1 = pltpu.roll(gext, 1, 0)[HALO:]
        g2 = pltpu.roll(gext, 2, 0)[HALO:]
        conv = cb_ref[:, cols] + cw_ref[0:1, cols] * g2 + cw_ref[1:2, cols] * g1 + cw_ref[2:3, cols] * gm
        acc_ref[:, cols] = (_gelu(conv) * um).astype(BF16)
    x2 = _layer_norm(alpha * x + _dot(acc_ref[...], wd_ref[...]), g_ref[...], b_ref[...])
    gate = jax.nn.sigmoid(_dot(x2.astype(BF16), wpg_ref[...]))
    o_ref[...] = x2 + gate * _dot(p_ref[...].astype(BF16), wpp_ref[...])


def _ffn(x1, p2d, w_in, cw, cb, wd, ln_g, ln_b, wpg, wpp, alpha, seq):
    T, D = x1.shape
    tm = FFN_TILE
    assert seq % tm == 0 and wd.shape[0] % FF_CHUNK == 0
    hb = tm // HALO
    const = lambda shape: pl.BlockSpec(shape, lambda i: (0, 0))
    return pl.pallas_call(
        functools.partial(_ffn_kernel, alpha=alpha, tiles_per_seq=seq // tm),
        grid=(T // tm,),
        in_specs=[pl.BlockSpec((tm, D), lambda i: (i, 0)),
                  pl.BlockSpec((HALO, D), lambda i: (jnp.maximum(i * hb - 1, 0), 0)),
                  pl.BlockSpec((tm, p2d.shape[1]), lambda i: (i, 0)),
                  const(w_in.shape), const(cw.shape), const(cb.shape), const(wd.shape),
                  const((1, D)), const((1, D)), const(wpg.shape), const(wpp.shape)],
        out_specs=pl.BlockSpec((tm, D), lambda i: (i, 0)),
        out_shape=jax.ShapeDtypeStruct((T, D), F32),
        scratch_shapes=[pltpu.VMEM((tm, wd.shape[0]), BF16)],
        compiler_params=_params("arbitrary"),
        name="ffn",
    )(x1, x1, p2d, w_in, cw, cb, wd, ln_g, ln_b, wpg, wpp)


def _slc_from_cmp_t(ncp, n_slc):
    ratio = SLC_BLOCK // CMP_STRIDE
    span = CMP_BLOCK // CMP_STRIDE
    mat = np.zeros((n_slc, ncp), np.float32)
    for j in range(n_slc):
        for m in range(ratio):
            for n in range(span):
                i = ratio * j + m - n
                if 0 <= i < ncp - 1:
                    mat[j, i] += 1.0
    return mat


def _layer(x, p_l, w_in, pe_k, w1_k, w2_k, pe_v, w1_v, w2_v, lq1, lk1, lq2, lk2, subln_g, w_bn, w_bd, w_out,
           ln1_g, ln1_b, w_ffn_in, conv_w, conv_b, w_down, ln2_g, ln2_b, w_pp, w_pg, tab, lambda_init,
           alpha):
    B, S, D = x.shape
    T = B * S
    ncp = S // CMP_STRIDE
    n_slc = S // SLC_BLOCK
    q_w = NSA_HEADS * HEAD_DIM
    kv_w = NSA_GROUPS * HEAD_DIM
    dqk_w = DIFF_HEADS * 2 * HEAD_DIM
    sizes = (q_w,) + (kv_w,) * 6 + (NSA_HEADS * 3, dqk_w, dqk_w, dqk_w, D, D)
    offs = np.concatenate([[0], np.cumsum(sizes)])
    col = lambda i: w_in[:, int(offs[i]):int(offs[i + 1])]
    scale = HEAD_DIM ** -0.5

    n_idx = np.arange(q_w)
    perm = (NSA_REP * ((n_idx % LANES) // HEAD_DIM) + n_idx // LANES) * HEAD_DIM + n_idx % HEAD_DIM
    w_main = jnp.concatenate([col(0)[:, perm] * scale] + [col(i) for i in range(3, 7)]
                             + [col(8) * scale, col(9), col(10)], axis=1).astype(BF16)
    w_cmp = jnp.concatenate([col(1), col(2)], axis=1).astype(BF16)
    w_gate = jnp.pad(col(7), ((0, 0), (0, LANES - NSA_HEADS * 3))).astype(BF16)
    x2d = x.reshape(T, D)
    proj, k_cmp, v_cmp, gates = _proj(x2d, w_main, w_cmp, w_gate)
    proj3 = proj.reshape(B, S, proj.shape[1])
    gates3 = gates.reshape(B, S, LANES)
    c_kslc, c_vslc, c_kwin, c_vwin = (q_w // LANES + i for i in range(4))
    c_dq = q_w // LANES + 4
    c_dk = c_dq + DIFF_HEADS
    c_dv = c_dk + DIFF_HEADS

    cw = CMP_STRIDE * kv_w
    rep = lambda a: jnp.broadcast_to(a.reshape(2, 2, CMP_STRIDE, 1, HEAD_DIM, -1),
                                     (2, 2, CMP_STRIDE, NSA_GROUPS, HEAD_DIM, a.shape[-1]))
    pe = rep(jnp.stack([pe_k, pe_v])[..., None]).reshape(2, 2, 1, cw)
    w1 = rep(jnp.stack([w1_k, w1_v])).reshape(2, 2, cw, CMP_HIDDEN).astype(BF16)
    w2 = jnp.stack([w2_k, w2_v])
    w2p = jnp.stack([jnp.pad(w2, ((0, 0), (0, 0), (g * HEAD_DIM, LANES - (g + 1) * HEAD_DIM)))
                     for g in range(NSA_GROUPS)], axis=1).astype(BF16)
    cmp_kv = _compress(k_cmp.reshape(B, ncp, cw), v_cmp.reshape(B, ncp, cw), pe, w1, w2p)

    matt = jnp.asarray(_slc_from_cmp_t(ncp, n_slc), BF16)
    ocmp, sel = _cmp_attention(tab, proj3, cmp_kv, gates3, matt, n_slc)
    et_np = np.zeros((S, LANES), np.float32)
    et_np[np.arange(S), np.arange(S) // SLC_BLOCK] = 2.0 ** MASK_EXP
    y_nsa = _nsa_attention(tab, proj3, sel, jnp.asarray(et_np, BF16), gates3, ocmp,
                           (c_kslc, c_vslc, c_kwin, c_vwin))

    lam4 = jnp.pad(jnp.stack([lq1, lk1, lq2, lk2]), ((0, SUBLANES - 4), (0, 0)))
    y_diff = _diff_attention(tab, proj3, lam4, subln_g.reshape(1, LANES), (c_dq, c_dk, c_dv), lambda_init)

    x1 = _merge(x2d, y_nsa.reshape(T, q_w), y_diff.reshape(T, dqk_w),
                col(11).astype(BF16), col(12).astype(BF16), w_bn[perm].astype(BF16), w_bd.astype(BF16),
                w_out.astype(BF16), ln1_g.reshape(1, D), ln1_b.reshape(1, D), alpha)

    cw = jnp.pad(conv_w, ((0, SUBLANES - CONV_WIDTH), (0, 0)))
    out = _ffn(x1, p_l.reshape(T, p_l.shape[-1]), w_ffn_in.astype(BF16), cw, conv_b.reshape(1, -1),
               w_down.astype(BF16), ln2_g.reshape(1, D), ln2_b.reshape(1, D),
               w_pg.astype(BF16), w_pp.astype(BF16), alpha, S)
    return out.reshape(B, S, D)


def kernel(x, p, w_in, nsa_cmp_pe_k, nsa_cmp_w1_k, nsa_cmp_w2_k, nsa_cmp_pe_v, nsa_cmp_w1_v, nsa_cmp_w2_v, diff_lambda_q1, diff_lambda_k1, diff_lambda_q2, diff_lambda_k2, diff_subln_g, w_branch_nsa, w_branch_diff, w_out, ln1_g, ln1_b, w_ffn_in, ffn_conv_w, ffn_conv_b, w_ffn_down, ln2_g, ln2_b, w_ple_proj, w_ple_gate, rel_bias_table):
    depth = w_in.shape[0]
    alpha = (2.0 * depth) ** 0.25
    for l in range(depth):
        lambda_init = 0.8 - 0.6 * math.exp(-0.3 * l)
        x = _layer(x, p[l], w_in[l], nsa_cmp_pe_k[l], nsa_cmp_w1_k[l], nsa_cmp_w2_k[l], nsa_cmp_pe_v[l],
                   nsa_cmp_w1_v[l], nsa_cmp_w2_v[l], diff_lambda_q1[l], diff_lambda_k1[l], diff_lambda_q2[l],
                   diff_lambda_k2[l], diff_subln_g[l], w_branch_nsa[l], w_branch_diff[l], w_out[l], ln1_g[l],
                   ln1_b[l], w_ffn_in[l], ffn_conv_w[l], ffn_conv_b[l], w_ffn_down[l], ln2_g[l], ln2_b[l],
                   w_ple_proj[l], w_ple_gate[l], rel_bias_table, lambda_init, alpha)
    return x
```

```python
import functools
import math

import jax
import jax.numpy as jnp
import numpy as np
from jax import lax
from jax.experimental import pallas as pl
from jax.experimental.pallas import tpu as pltpu

F32 = jnp.float32
BF16 = jnp.bfloat16

NSA_HEADS = 8
NSA_GROUPS = 2
NSA_REP = NSA_HEADS // NSA_GROUPS
HEAD_DIM = 64
CMP_BLOCK = 32
CMP_STRIDE = 16
CMP_HIDDEN = 256
SLC_BLOCK = 64
SLC_TOPK = 16
SLC_LOCAL = 2
WINDOW = 512
DIFF_HEADS = 4
REL_BUCKETS = 32
REL_MAX_EXACT = 16
REL_MAX_DIST = 128
D_FF = 2816
CONV_WIDTH = 3
LN_EPS = 1e-5
NEG_INF = -1e30
BIG = 1e30
MASK_EXP = 100

LANES = 128
SUBLANES = 8
VMEM_LIMIT = 56 * 1024 * 1024

CMP_TQ = 256
NSA_TQ = 256
DIFF_TQ = 512
NSA_FAR_KEYS = 512
DIFF_FAR_KEYS = 1024
NSA_ONLINE_KEYS = 512
DIFF_ONLINE_KEYS = 1024
ROW_TILE = 512
FFN_TILE = 512
FF_CHUNK = 256
HALO = 16


def _rel_breakpoints():
    n = np.arange(0, 4 * REL_MAX_DIST)
    large = REL_MAX_EXACT + (np.log(np.maximum(n, 1).astype(np.float32) / REL_MAX_EXACT)
                             / np.float32(math.log(REL_MAX_DIST / REL_MAX_EXACT))
                             * (REL_BUCKETS - REL_MAX_EXACT)).astype(np.int32)
    bucket = np.where(n < REL_MAX_EXACT, n, np.minimum(large, REL_BUCKETS - 1))
    assert np.all(np.diff(bucket) >= 0)
    return [int(np.argmax(bucket >= b)) for b in range(1, REL_BUCKETS)]


REL_BREAKS = _rel_breakpoints()


def _dot(a, b):
    return jnp.dot(a, b, preferred_element_type=F32)


def _dot_nt(a, b):
    return lax.dot_general(a, b, (((1,), (1,)), ((), ())), preferred_element_type=F32)


def _rel_bias(dist, tab_ref, head, shift):
    val = jnp.full(dist.shape, tab_ref[0, head] - shift, F32)
    for b, brk in enumerate(REL_BREAKS, start=1):
        val = jnp.where(dist >= brk, tab_ref[b, head] - shift, val)
    return val


def _gelu(x):
    c = math.sqrt(2.0 / math.pi)
    half = 0.5 * x
    return half + half * jnp.tanh(x * (c + (c * 0.044715) * (x * x)))


def _layer_norm(z, g, b):
    mu = jnp.mean(z, axis=-1, keepdims=True)
    zc = z - mu
    var = jnp.mean(zc * zc, axis=-1, keepdims=True)
    return zc * lax.rsqrt(var + LN_EPS) * g + b


def _params(*sem):
    return pltpu.CompilerParams(dimension_semantics=sem, vmem_limit_bytes=VMEM_LIMIT)


def _proj_kernel(x_ref, wm_ref, wc_ref, wg_ref, om_ref, ok_ref, ov_ref, og_ref):
    xb = x_ref[...].astype(BF16)
    n = wm_ref.shape[1]
    for c in range(0, n, 2 * LANES):
        w = min(2 * LANES, n - c)
        om_ref[:, c:c + w] = _dot(xb, wm_ref[:, c:c + w]).astype(om_ref.dtype)
    kv = _dot(xb, wc_ref[...])
    ok_ref[...] = kv[:, :LANES].astype(ok_ref.dtype)
    ov_ref[...] = kv[:, LANES:].astype(ov_ref.dtype)
    og_ref[...] = _dot(xb, wg_ref[...])


def _proj(x2d, w_main, w_cmp, w_gate):
    T, D = x2d.shape
    n = w_main.shape[1]
    row = lambda width: pl.BlockSpec((ROW_TILE, width), lambda i: (i, 0))
    const = lambda width: pl.BlockSpec((D, width), lambda i: (0, 0))
    return pl.pallas_call(
        _proj_kernel,
        grid=(T // ROW_TILE,),
        in_specs=[row(D), const(n), const(2 * LANES), const(LANES)],
        out_specs=[row(n), row(LANES), row(LANES), row(LANES)],
        out_shape=[jax.ShapeDtypeStruct((T, n), BF16), jax.ShapeDtypeStruct((T, LANES), BF16),
                   jax.ShapeDtypeStruct((T, LANES), BF16), jax.ShapeDtypeStruct((T, LANES), F32)],
        compiler_params=_params("arbitrary"),
        name="proj",
    )(x2d, w_main, w_cmp, w_gate)


def _compress_kernel(xk_ref, xv_ref, pe_ref, w1_ref, w2_ref, o_ref):
    ncp, cw = xk_ref.shape[1], xk_ref.shape[2]
    lane = lax.broadcasted_iota(jnp.int32, (ncp, cw), 1)
    group = jnp.bitwise_and(jnp.right_shift(lane, int(math.log2(HEAD_DIM))), NSA_GROUPS - 1)
    for s, x_ref in enumerate((xk_ref, xv_ref)):
        x = x_ref[0].astype(F32)
        xa = x + pe_ref[s, 0]
        xb = x + pe_ref[s, 1]
        acc = jnp.zeros((ncp, LANES), F32)
        for g in range(NSA_GROUPS):
            a = _dot(jnp.where(group == g, xa, 0.0).astype(BF16), w1_ref[s, 0])
            b = _dot(jnp.where(group == g, xb, 0.0).astype(BF16), w1_ref[s, 1])
            h = a + pltpu.roll(b, ncp - 1, 0)
            acc = acc + _dot(_gelu(h).astype(BF16), w2_ref[s, g])
        o_ref[s, 0] = acc.astype(o_ref.dtype)


def _compress(xk, xv, pe, w1, w2):
    B, ncp, cw = xk.shape
    x_spec = pl.BlockSpec((1, ncp, cw), lambda b: (b, 0, 0))
    const = lambda shape: pl.BlockSpec(shape, lambda b: (0, 0, 0, 0))
    return pl.pallas_call(
        _compress_kernel,
        grid=(B,),
        in_specs=[x_spec, x_spec, const(pe.shape), const(w1.shape), const(w2.shape)],
        out_specs=pl.BlockSpec((2, 1, ncp, LANES), lambda b: (0, b, 0, 0)),
        out_shape=jax.ShapeDtypeStruct((2, B, ncp, LANES), BF16),
        compiler_params=_params("arbitrary"),
        name="compress",
    )(xk, xv, pe, w1, w2)


def _cmp_kernel(tab_ref, q_ref, kc_ref, vc_ref, gate_ref, matt_ref, ocmp_ref, sel_ref, bias_ref, ball_ref,
                *, n_slc):
    tq = q_ref.shape[1]
    ncp = kc_ref.shape[2]
    q0 = pl.program_id(0) * tq
    lane = lax.broadcasted_iota(jnp.int32, (tq, LANES), 1)

    @pl.when((pl.program_id(0) == 0) & (pl.program_id(1) == 0))
    def _():
        for head in range(NSA_HEADS):
            bias_ref[head] = _rel_bias(lane, tab_ref, head, 0.0)

    @pl.when(pl.program_id(1) == 0)
    def _():
        t_idx = q0 + lax.broadcasted_iota(jnp.int32, (tq, ncp), 0)
        c_idx = lax.broadcasted_iota(jnp.int32, (tq, ncp), 1)
        dist = t_idx - (c_idx * CMP_STRIDE + (CMP_BLOCK - 1))
        near = jnp.clip(dist, 0, LANES - 1)
        for head in range(NSA_HEADS):
            table = bias_ref[head]
            cols = [jnp.take_along_axis(table, near[:, c:c + LANES], axis=1) for c in range(0, ncp, LANES)]
            ball_ref[head * tq:(head + 1) * tq, :] = jnp.where(dist >= 0, jnp.concatenate(cols, axis=1), NEG_INF)

    gates = jax.nn.sigmoid(gate_ref[0])
    eye = (lax.broadcasted_iota(jnp.int32, (tq, tq), 0)
           == lax.broadcasted_iota(jnp.int32, (tq, tq), 1)).astype(BF16)
    kc = kc_ref[0, 0]
    vc = vc_ref[0, 0]
    jrow_i = lax.broadcasted_iota(jnp.int32, (n_slc, tq), 0)
    cur = jnp.right_shift(q0 + lax.broadcasted_iota(jnp.int32, (n_slc, tq), 1), int(math.log2(SLC_BLOCK)))
    forced = (jrow_i == 0) | ((cur - jrow_i >= 0) & (cur - jrow_i < SLC_LOCAL))
    blk_valid = jrow_i <= cur
    jrow = jrow_i.astype(F32)
    q_parts = []
    for g in range(NSA_GROUPS):
        lane_g = (lane >= HEAD_DIM * g) & (lane < HEAD_DIM * (g + 1))
        for r in range(NSA_REP):
            qb = q_ref[0, :, r * LANES:(r + 1) * LANES]
            q_parts.append(jnp.where(lane_g, qb, jnp.zeros_like(qb)))
    logit = _dot_nt(jnp.concatenate(q_parts, axis=0), kc) + ball_ref[...]
    m = jnp.max(logit, axis=-1, keepdims=True)
    e = jnp.exp(logit - m)
    any_valid = (q0 + lax.broadcasted_iota(jnp.int32, (tq, 1), 0) >= CMP_BLOCK - 1).astype(F32)
    p_all = e * (jnp.concatenate([any_valid] * NSA_HEADS, axis=0) / jnp.sum(e, axis=-1, keepdims=True))
    o_all = _dot(p_all.astype(BF16), vc)
    gated = [[gates[:, h * 3:h * 3 + 1] * o_all[h * tq:(h + 1) * tq]
              for h in range(g * NSA_REP, (g + 1) * NSA_REP)] for g in range(NSA_GROUPS)]
    for g in range(NSA_GROUPS):
        psum = functools.reduce(lambda a, b: a + b,
                                [p_all[h * tq:(h + 1) * tq] for h in range(g * NSA_REP, (g + 1) * NSA_REP)])
        hi = psum.astype(BF16)
        lo = (psum - hi.astype(F32)).astype(BF16)
        p_slc = _dot_nt(matt_ref[...], hi) + _dot_nt(matt_ref[...], lo)
        score = jnp.where(forced, BIG, jnp.where(blk_valid, p_slc, NEG_INF))
        sel = jnp.zeros((n_slc, tq), F32)
        for _ in range(min(SLC_TOPK, n_slc)):
            mx = jnp.max(score, axis=0, keepdims=True)
            idx = jnp.min(jnp.where(score == mx, jrow, float(n_slc)), axis=0, keepdims=True)
            hit = jrow == idx
            sel = jnp.where(hit, 1.0, sel)
            score = jnp.where(hit, -3.0e38, score)
        selm1 = (sel - 1.0).astype(BF16)
        if n_slc < LANES:
            selm1 = jnp.concatenate([selm1, jnp.zeros((LANES - n_slc, tq), BF16)], axis=0)
        sel_ref[0, g] = _dot_nt(eye, selm1).astype(sel_ref.dtype)
    for r in range(NSA_REP):
        ocmp_ref[0, :, r * LANES:(r + 1) * LANES] = jnp.where(lane < HEAD_DIM, gated[0][r], gated[1][r])


def _cmp_attention(tab, proj3, cmp_kv, gates3, matt, n_slc):
    B, S, _ = proj3.shape
    ncp = cmp_kv.shape[2]
    tq = CMP_TQ
    return pl.pallas_call(
        functools.partial(_cmp_kernel, n_slc=n_slc),
        grid=(S // tq, B),
        in_specs=[pl.BlockSpec(memory_space=pltpu.SMEM),
                  pl.BlockSpec((1, tq, 4 * LANES), lambda i, b: (b, i, 0)),
                  pl.BlockSpec((1, 1, ncp, LANES), lambda i, b: (0, b, 0, 0)),
                  pl.BlockSpec((1, 1, ncp, LANES), lambda i, b: (1, b, 0, 0)),
                  pl.BlockSpec((1, tq, LANES), lambda i, b: (b, i, 0)),
                  pl.BlockSpec((n_slc, ncp), lambda i, b: (0, 0))],
        out_specs=[pl.BlockSpec((1, tq, 4 * LANES), lambda i, b: (b, i, 0)),
                   pl.BlockSpec((1, NSA_GROUPS, tq, LANES), lambda i, b: (b, 0, i, 0))],
        out_shape=[jax.ShapeDtypeStruct((B, S, 4 * LANES), F32),
                   jax.ShapeDtypeStruct((B, NSA_GROUPS, S, LANES), BF16)],
        scratch_shapes=[pltpu.VMEM((NSA_HEADS, tq, LANES), F32), pltpu.VMEM((NSA_HEADS * tq, ncp), F32)],
        compiler_params=_params("arbitrary", "arbitrary"),
        name="cmp",
    )(tab, proj3, cmp_kv, cmp_kv, gates3, matt)


def _online_attention(segments, score, value, m_ref, l_ref, acc_ref, sub_keys):
    m_ref[...] = jnp.full(m_ref.shape, NEG_INF, F32)
    l_ref[...] = jnp.zeros(l_ref.shape, F32)
    acc_ref[...] = jnp.zeros(acc_ref.shape, F32)

    def step(k0, width, bias):
        for j in range(0, width, sub_keys):
            w = min(sub_keys, width - j)
            s = score(k0 + j, w)
            cols = []
            for c in range(0, w, LANES):
                b = None if bias is None else bias(j + c)
                cols.append(s[:, c:c + LANES] if b is None else s[:, c:c + LANES] + b)
            m_old = m_ref[...]
            m_new = jnp.maximum(m_old, jnp.max(functools.reduce(jnp.maximum, cols), axis=-1, keepdims=True))
            alpha = jnp.exp(m_old - m_new)
            ps = [jnp.exp(col - m_new) for col in cols]
            l_ref[...] = alpha * l_ref[...] + functools.reduce(lambda a, b: a + b, ps)
            acc_ref[...] = alpha * acc_ref[...] + _dot(
                jnp.concatenate([p.astype(BF16) for p in ps], axis=1), value(k0 + j, w))
            m_ref[...] = m_new

    segments(step)
    return acc_ref[...] / jnp.sum(l_ref[...], axis=-1, keepdims=True)


def _causal_segments(fn, qt, tq, far_keys, near_bias):
    n_far = jnp.maximum(qt - 1, 0)
    far_tiles = far_keys // tq
    step = far_tiles * tq

    def far_body(i, carry):
        fn(i * step, step, None)
        return carry

    lax.fori_loop(0, lax.div(n_far, far_tiles), far_body, 0)
    for rem in range(far_tiles):
        lead = rem * tq
        bias = (lambda c, lead=lead: None if c < lead else near_bias(c - lead))
        pl.when((qt >= 1) & (lax.rem(n_far, far_tiles) == rem))(
            functools.partial(fn, (qt - 1 - rem) * tq, lead + 2 * tq, bias))
    pl.when(qt == 0)(functools.partial(fn, 0, tq, lambda c: near_bias(tq + c)))


def _key_slice(k0, width, tq):
    return pl.ds(k0 if isinstance(k0, int) else pl.multiple_of(k0, tq), width)


def _rows(ref, k0, width, tq):
    return ref[0, _key_slice(k0, width, tq), :]


def _nsa_kernel(tab_ref, q_ref, ks_ref, vs_ref, kw_ref, vw_ref, sel_ref, et_ref, gate_ref, ocmp_ref,
                y_ref, tb_ref, m_ref, l_ref, acc_ref, part_ref):
    tq = q_ref.shape[1]
    n_win = WINDOW // tq
    qt = pl.program_id(1)

    @pl.when((pl.program_id(0) == 0) & (qt == 0))
    def _():
        ti = lax.broadcasted_iota(jnp.int32, (tq, tq), 0)
        ki = lax.broadcasted_iota(jnp.int32, (tq, tq), 1)
        for g in range(NSA_GROUPS):
            for r in range(NSA_REP):
                head = g * NSA_REP + r
                far_bias = tab_ref[REL_BUCKETS - 1, head]
                rows = slice(head * tq, (head + 1) * tq)
                tb_ref[rows, 0:tq] = jnp.where(ti < ki, 0.0, NEG_INF)
                for j in range(2, n_win):
                    tb_ref[rows, (n_win - j) * tq:(n_win - j + 1) * tq] = jnp.zeros((tq, tq), F32)
                tb_ref[rows, (n_win - 1) * tq:n_win * tq] = _rel_bias(ti - ki + tq, tab_ref, head, far_bias)
                tb_ref[rows, n_win * tq:(n_win + 1) * tq] = jnp.where(
                    ti >= ki, _rel_bias(ti - ki, tab_ref, head, far_bias), NEG_INF)

    lane = lax.broadcasted_iota(jnp.int32, (tq, LANES), 1)
    gates = jax.nn.sigmoid(gate_ref[0])

    q_parts, sel_parts = [], []
    for g in range(NSA_GROUPS):
        lane_g = (lane >= HEAD_DIM * g) & (lane < HEAD_DIM * (g + 1))
        for r in range(NSA_REP):
            qb = q_ref[0, :, r * LANES:(r + 1) * LANES]
            q_parts.append(jnp.where(lane_g, qb, jnp.zeros_like(qb)))
            sel_parts.append(sel_ref[0, g])
    q_all = jnp.concatenate(q_parts, axis=0)
    qs_all = jnp.concatenate([q_all, jnp.concatenate(sel_parts, axis=0)], axis=1)

    def bias_from(col0):
        return lambda c: tb_ref[:, col0 + c:col0 + c + LANES]

    def slc_score(k0, width):
        et = et_ref[_key_slice(k0, width, tq), :]
        return _dot_nt(qs_all, jnp.concatenate([_rows(ks_ref, k0, width, tq), et], axis=1))

    def slc_segments(fn):
        _causal_segments(fn, qt, tq, NSA_FAR_KEYS, bias_from((n_win - 1) * tq))

    part_ref[...] = _online_attention(slc_segments, slc_score, lambda k0, width: _rows(vs_ref, k0, width, tq),
                                      m_ref, l_ref, acc_ref, NSA_ONLINE_KEYS)

    def win_score(k0, width):
        return _dot_nt(q_all, _rows(kw_ref, k0, width, tq))

    def win_segments(fn):
        for n in range(n_win):
            pl.when(qt == n)(functools.partial(fn, 0, (n + 1) * tq, bias_from((n_win - n) * tq)))
        pl.when(qt >= n_win)(functools.partial(fn, (qt - n_win) * tq, (n_win + 1) * tq, bias_from(0)))

    o_win = _online_attention(win_segments, win_score, lambda k0, width: _rows(vw_ref, k0, width, tq),
                              m_ref, l_ref, acc_ref, NSA_ONLINE_KEYS)
    o_slc = part_ref[...]

    for r in range(NSA_REP):
        ys = []
        for g in range(NSA_GROUPS):
            head = g * NSA_REP + r
            rows = slice(head * tq, (head + 1) * tq)
            ys.append(gates[:, head * 3 + 1:head * 3 + 2] * o_slc[rows]
                      + gates[:, head * 3 + 2:head * 3 + 3] * o_win[rows])
        cols = slice(r * LANES, (r + 1) * LANES)
        y_ref[0, :, cols] = (ocmp_ref[0, :, cols] + jnp.where(lane < HEAD_DIM, ys[0], ys[1])).astype(y_ref.dtype)


def _nsa_attention(tab, proj3, sel, et, gates3, ocmp, col_blocks):
    B, S, _ = proj3.shape
    tq = NSA_TQ
    assert WINDOW % tq == 0 and WINDOW // tq >= 2 and S % tq == 0
    n_win = WINDOW // tq
    ks_c, vs_c, kw_c, vw_c = col_blocks
    rows = NSA_HEADS * tq

    def kv_spec(c):
        return pl.BlockSpec((1, S, LANES), lambda b, i: (b, 0, c))

    return pl.pallas_call(
        _nsa_kernel,
        grid=(B, S // tq),
        in_specs=[pl.BlockSpec(memory_space=pltpu.SMEM),
                  pl.BlockSpec((1, tq, 4 * LANES), lambda b, i: (b, i, 0)),
                  kv_spec(ks_c), kv_spec(vs_c), kv_spec(kw_c), kv_spec(vw_c),
                  pl.BlockSpec((1, NSA_GROUPS, tq, LANES), lambda b, i: (b, 0, i, 0)),
                  pl.BlockSpec((S, LANES), lambda b, i: (0, 0)),
                  pl.BlockSpec((1, tq, LANES), lambda b, i: (b, i, 0)),
                  pl.BlockSpec((1, tq, 4 * LANES), lambda b, i: (b, i, 0))],
        out_specs=pl.BlockSpec((1, tq, 4 * LANES), lambda b, i: (b, i, 0)),
        out_shape=jax.ShapeDtypeStruct((B, S, 4 * LANES), BF16),
        scratch_shapes=[pltpu.VMEM((rows, (n_win + 1) * tq), F32),
                        pltpu.VMEM((rows, LANES), F32),
                        pltpu.VMEM((rows, LANES), F32),
                        pltpu.VMEM((rows, LANES), F32),
                        pltpu.VMEM((rows, LANES), F32)],
        compiler_params=_params("arbitrary", "arbitrary"),
        name="nsa",
    )(tab, proj3, proj3, proj3, proj3, proj3, sel, et, gates3, ocmp)


def _diff_kernel(tab_ref, q_ref, k_ref, v_ref, lam_ref, g_ref, y_ref, tb_ref, m_ref, l_ref, acc_ref,
                 *, lambda_init):
    tq = q_ref.shape[1]
    h = pl.program_id(1)
    qt = pl.program_id(2)

    @pl.when(qt == 0)
    def _():
        ti = lax.broadcasted_iota(jnp.int32, (tq, tq), 0)
        ki = lax.broadcasted_iota(jnp.int32, (tq, tq), 1)
        head = NSA_HEADS + h
        far_bias = tab_ref[REL_BUCKETS - 1, head]
        tb_ref[:, 0:tq] = _rel_bias(ti - ki + tq, tab_ref, head, far_bias)
        tb_ref[:, tq:2 * tq] = jnp.where(ti >= ki, _rel_bias(ti - ki, tab_ref, head, far_bias), NEG_INF)

    lane = lax.broadcasted_iota(jnp.int32, (tq, LANES), 1)
    qb = q_ref[0]
    zero = jnp.zeros_like(qb)
    q2 = jnp.concatenate([jnp.where(lane < HEAD_DIM, qb, zero), jnp.where(lane >= HEAD_DIM, qb, zero)], axis=0)

    def score(k0, width):
        return _dot_nt(q2, _rows(k_ref, k0, width, tq))

    def near_bias(c):
        tb = tb_ref[:, c:c + LANES]
        return jnp.concatenate([tb, tb], axis=0)

    def segments(fn):
        _causal_segments(fn, qt, tq, DIFF_FAR_KEYS, near_bias)

    a = _online_attention(segments, score, lambda k0, width: _rows(v_ref, k0, width, tq), m_ref, l_ref, acc_ref,
                          DIFF_ONLINE_KEYS)
    lq1, lk1, lq2, lk2 = lam_ref[0:1, :], lam_ref[1:2, :], lam_ref[2:3, :], lam_ref[3:4, :]
    lam = (jnp.exp(jnp.sum(lq1 * lk1, axis=-1, keepdims=True))
           - jnp.exp(jnp.sum(lq2 * lk2, axis=-1, keepdims=True)) + lambda_init)
    o = a[:tq] - lam * a[tq:]
    o = o * lax.rsqrt(jnp.mean(o * o, axis=-1, keepdims=True) + LN_EPS) * g_ref[...]
    y_ref[0] = (o * (1.0 - lambda_init)).astype(y_ref.dtype)


def _diff_attention(tab, proj3, lam4, subln_g, col_blocks, lambda_init):
    B, S, _ = proj3.shape
    tq = min(DIFF_TQ, S)
    q_c, k_c, v_c = col_blocks
    return pl.pallas_call(
        functools.partial(_diff_kernel, lambda_init=lambda_init),
        grid=(B, DIFF_HEADS, S // tq),
        in_specs=[pl.BlockSpec(memory_space=pltpu.SMEM),
                  pl.BlockSpec((1, tq, LANES), lambda b, h, i: (b, i, q_c + h)),
                  pl.BlockSpec((1, S, LANES), lambda b, h, i: (b, 0, k_c + h)),
                  pl.BlockSpec((1, S, LANES), lambda b, h, i: (b, 0, v_c + h)),
                  pl.BlockSpec((SUBLANES, HEAD_DIM), lambda b, h, i: (0, 0)),
                  pl.BlockSpec((1, LANES), lambda b, h, i: (0, 0))],
        out_specs=pl.BlockSpec((1, tq, LANES), lambda b, h, i: (b, i, h)),
        out_shape=jax.ShapeDtypeStruct((B, S, DIFF_HEADS * LANES), BF16),
        scratch_shapes=[pltpu.VMEM((tq, 2 * tq), F32),
                        pltpu.VMEM((2 * tq, LANES), F32),
                        pltpu.VMEM((2 * tq, LANES), F32),
                        pltpu.VMEM((2 * tq, LANES), F32)],
        compiler_params=_params("arbitrary", "arbitrary", "arbitrary"),
        name="diff",
    )(tab, proj3, proj3, proj3, lam4, subln_g)


def _merge_kernel(x_ref, yn_ref, yd_ref, wgn_ref, wgd_ref, wbn_ref, wbd_ref, wo_ref, g_ref, b_ref, o_ref,
                  *, alpha):
    x = x_ref[...]
    xb = x.astype(BF16)
    merged = (jax.nn.sigmoid(_dot(xb, wgn_ref[...])) * _dot(yn_ref[...], wbn_ref[...])
              + jax.nn.sigmoid(_dot(xb, wgd_ref[...])) * _dot(yd_ref[...], wbd_ref[...]))
    z = alpha * x + _dot(merged.astype(BF16), wo_ref[...])
    o_ref[...] = _layer_norm(z, g_ref[...], b_ref[...])


def _merge(x2d, y_nsa, y_diff, wgn, wgd, wbn, wbd, wo, ln_g, ln_b, alpha):
    T, D = x2d.shape
    const = lambda shape: pl.BlockSpec(shape, lambda i: (0, 0))
    return pl.pallas_call(
        functools.partial(_merge_kernel, alpha=alpha),
        grid=(T // ROW_TILE,),
        in_specs=[pl.BlockSpec((ROW_TILE, D), lambda i: (i, 0)),
                  pl.BlockSpec((ROW_TILE, y_nsa.shape[1]), lambda i: (i, 0)),
                  pl.BlockSpec((ROW_TILE, y_diff.shape[1]), lambda i: (i, 0)),
                  const(wgn.shape), const(wgd.shape), const(wbn.shape), const(wbd.shape), const(wo.shape),
                  const((1, D)), const((1, D))],
        out_specs=pl.BlockSpec((ROW_TILE, D), lambda i: (i, 0)),
        out_shape=jax.ShapeDtypeStruct((T, D), F32),
        compiler_params=_params("arbitrary"),
        name="merge",
    )(x2d, y_nsa, y_diff, wgn, wgd, wbn, wbd, wo, ln_g, ln_b)


def _ffn_kernel(x_ref, halo_ref, p_ref, w_ref, cw_ref, cb_ref, wd_ref, g_ref, b_ref, wpg_ref, wpp_ref,
                o_ref, acc_ref, *, alpha, tiles_per_seq):
    tm = x_ref.shape[0]
    d_ff = wd_ref.shape[0]
    x = x_ref[...]
    xb = x.astype(BF16)
    keep = (pl.program_id(0) % tiles_per_seq != 0).astype(F32)
    hb = halo_ref[...].astype(BF16)
    for c in range(0, d_ff, FF_CHUNK):
        cols = slice(c, c + FF_CHUNK)
        wg = w_ref[:, cols]
        gm = _dot(xb, wg)
        gh = _dot(hb, wg) * keep
        um = _dot(xb, w_ref[:, d_ff + c:d_ff + c + FF_CHUNK])
        gext = jnp.concatenate([gh, gm], axis=0)
        g1 = pltpu.roll(gext, 1, 0)[HALO:]
        g2 = pltpu.roll(gext, 2, 0)[HALO:]
        conv = cb_ref[:, cols] + cw_ref[0:1, cols] * g2 + cw_ref[1:2, cols] * g1 + cw_ref[2:3, cols] * gm
        acc_ref[:, cols] = (_gelu(conv) * um).astype(BF16)
    x2 = _layer_norm(alpha * x + _dot(acc_ref[...], wd_ref[...]), g_ref[...], b_ref[...])
    gate = jax.nn.sigmoid(_dot(x2.astype(BF16), wpg_ref[...]))
    o_ref[...] = x2 + gate * _dot(p_ref[...].astype(BF16), wpp_ref[...])


def _ffn(x1, p2d, w_in, cw, cb, wd, ln_g, ln_b, wpg, wpp, alpha, seq):
    T, D = x1.shape
    tm = FFN_TILE
    assert seq % tm == 0 and wd.shape[0] % FF_CHUNK == 0
    hb = tm // HALO
    const = lambda shape: pl.BlockSpec(shape, lambda i: (0, 0))
    return pl.pallas_call(
        functools.partial(_ffn_kernel, alpha=alpha, tiles_per_seq=seq // tm),
        grid=(T // tm,),
        in_specs=[pl.BlockSpec((tm, D), lambda i: (i, 0)),
                  pl.BlockSpec((HALO, D), lambda i: (jnp.maximum(i * hb - 1, 0), 0)),
                  pl.BlockSpec((tm, p2d.shape[1]), lambda i: (i, 0)),
                  const(w_in.shape), const(cw.shape), const(cb.shape), const(wd.shape),
                  const((1, D)), const((1, D)), const(wpg.shape), const(wpp.shape)],
        out_specs=pl.BlockSpec((tm, D), lambda i: (i, 0)),
        out_shape=jax.ShapeDtypeStruct((T, D), F32),
        scratch_shapes=[pltpu.VMEM((tm, wd.shape[0]), BF16)],
        compiler_params=_params("arbitrary"),
        name="ffn",
    )(x1, x1, p2d, w_in, cw, cb, wd, ln_g, ln_b, wpg, wpp)


def _slc_from_cmp_t(ncp, n_slc):
    ratio = SLC_BLOCK // CMP_STRIDE
    span = CMP_BLOCK // CMP_STRIDE
    mat = np.zeros((n_slc, ncp), np.float32)
    for j in range(n_slc):
        for m in range(ratio):
            for n in range(span):
                i = ratio * j + m - n
                if 0 <= i < ncp - 1:
                    mat[j, i] += 1.0
    return mat


def _layer(x, p_l, w_in, pe_k, w1_k, w2_k, pe_v, w1_v, w2_v, lq1, lk1, lq2, lk2, subln_g, w_bn, w_bd, w_out,
           ln1_g, ln1_b, w_ffn_in, conv_w, conv_b, w_down, ln2_g, ln2_b, w_pp, w_pg, tab, lambda_init,
           alpha):
    B, S, D = x.shape
    T = B * S
    ncp = S // CMP_STRIDE
    n_slc = S // SLC_BLOCK
    q_w = NSA_HEADS * HEAD_DIM
    kv_w = NSA_GROUPS * HEAD_DIM
    dqk_w = DIFF_HEADS * 2 * HEAD_DIM
    sizes = (q_w,) + (kv_w,) * 6 + (NSA_HEADS * 3, dqk_w, dqk_w, dqk_w, D, D)
    offs = np.concatenate([[0], np.cumsum(sizes)])
    col = lambda i: w_in[:, int(offs[i]):int(offs[i + 1])]
    scale = HEAD_DIM ** -0.5

    n_idx = np.arange(q_w)
    perm = (NSA_REP * ((n_idx % LANES) // HEAD_DIM) + n_idx // LANES) * HEAD_DIM + n_idx % HEAD_DIM
    w_main = jnp.concatenate([col(0)[:, perm] * scale] + [col(i) for i in range(3, 7)]
                             + [col(8) * scale, col(9), col(10)], axis=1).astype(BF16)
    w_cmp = jnp.concatenate([col(1), col(2)], axis=1).astype(BF16)
    w_gate = jnp.pad(col(7), ((0, 0), (0, LANES - NSA_HEADS * 3))).astype(BF16)
    x2d = x.reshape(T, D)
    proj, k_cmp, v_cmp, gates = _proj(x2d, w_main, w_cmp, w_gate)
    proj3 = proj.reshape(B, S, proj.shape[1])
    gates3 = gates.reshape(B, S, LANES)
    c_kslc, c_vslc, c_kwin, c_vwin = (q_w // LANES + i for i in range(4))
    c_dq = q_w // LANES + 4
    c_dk = c_dq + DIFF_HEADS
    c_dv = c_dk + DIFF_HEADS

    cw = CMP_STRIDE * kv_w
    rep = lambda a: jnp.broadcast_to(a.reshape(2, 2, CMP_STRIDE, 1, HEAD_DIM, -1),
                                     (2, 2, CMP_STRIDE, NSA_GROUPS, HEAD_DIM, a.shape[-1]))
    pe = rep(jnp.stack([pe_k, pe_v])[..., None]).reshape(2, 2, 1, cw)
    w1 = rep(jnp.stack([w1_k, w1_v])).reshape(2, 2, cw, CMP_HIDDEN).astype(BF16)
    w2 = jnp.stack([w2_k, w2_v])
    w2p = jnp.stack([jnp.pad(w2, ((0, 0), (0, 0), (g * HEAD_DIM, LANES - (g + 1) * HEAD_DIM)))
                     for g in range(NSA_GROUPS)], axis=1).astype(BF16)
    cmp_kv = _compress(k_cmp.reshape(B, ncp, cw), v_cmp.reshape(B, ncp, cw), pe, w1, w2p)

    matt = jnp.asarray(_slc_from_cmp_t(ncp, n_slc), BF16)
    ocmp, sel = _cmp_attention(tab, proj3, cmp_kv, gates3, matt, n_slc)
    et_np = np.zeros((S, LANES), np.float32)
    et_np[np.arange(S), np.arange(S) // SLC_BLOCK] = 2.0 ** MASK_EXP
    y_nsa = _nsa_attention(tab, proj3, sel, jnp.asarray(et_np, BF16), gates3, ocmp,
                           (c_kslc, c_vslc, c_kwin, c_vwin))

    lam4 = jnp.pad(jnp.stack([lq1, lk1, lq2, lk2]), ((0, SUBLANES - 4), (0, 0)))
    y_diff = _diff_attention(tab, proj3, lam4, subln_g.reshape(1, LANES), (c_dq, c_dk, c_dv), lambda_init)

    x1 = _merge(x2d, y_nsa.reshape(T, q_w), y_diff.reshape(T, dqk_w),
                col(11).astype(BF16), col(12).astype(BF16), w_bn[perm].astype(BF16), w_bd.astype(BF16),
                w_out.astype(BF16), ln1_g.reshape(1, D), ln1_b.reshape(1, D), alpha)

    cw = jnp.pad(conv_w, ((0, SUBLANES - CONV_WIDTH), (0, 0)))
    out = _ffn(x1, p_l.reshape(T, p_l.shape[-1]), w_ffn_in.astype(BF16), cw, conv_b.reshape(1, -1),
               w_down.astype(BF16), ln2_g.reshape(1, D), ln2_b.reshape(1, D),
               w_pg.astype(BF16), w_pp.astype(BF16), alpha, S)
    return out.reshape(B, S, D)


def kernel(x, p, w_in, nsa_cmp_pe_k, nsa_cmp_w1_k, nsa_cmp_w2_k, nsa_cmp_pe_v, nsa_cmp_w1_v, nsa_cmp_w2_v, diff_lambda_q1, diff_lambda_k1, diff_lambda_q2, diff_lambda_k2, diff_subln_g, w_branch_nsa, w_branch_diff, w_out, ln1_g, ln1_b, w_ffn_in, ffn_conv_w, ffn_conv_b, w_ffn_down, ln2_g, ln2_b, w_ple_proj, w_ple_gate, rel_bias_table):
    depth = w_in.shape[0]
    alpha = (2.0 * depth) ** 0.25
    for l in range(depth):
        lambda_init = 0.8 - 0.6 * math.exp(-0.3 * l)
        x = _layer(x, p[l], w_in[l], nsa_cmp_pe_k[l], nsa_cmp_w1_k[l], nsa_cmp_w2_k[l], nsa_cmp_pe_v[l],
                   nsa_cmp_w1_v[l], nsa_cmp_w2_v[l], diff_lambda_q1[l], diff_lambda_k1[l], diff_lambda_q2[l],
                   diff_lambda_k2[l], diff_subln_g[l], w_branch_nsa[l], w_branch_diff[l], w_out[l], ln1_g[l],
                   ln1_b[l], w_ffn_in[l], ffn_conv_w[l], ffn_conv_b[l], w_ffn_down[l], ln2_g[l], ln2_b[l],
                   w_ple_proj[l], w_ple_gate[l], rel_bias_table, lambda_init, alpha)
    return x
```

```python
import functools
import math

import jax
import jax.numpy as jnp
import numpy as np
from jax import lax
from jax.experimental import pallas as pl
from jax.experimental.pallas import tpu as pltpu

F32 = jnp.float32
BF16 = jnp.bfloat16

NSA_HEADS = 8
NSA_GROUPS = 2
NSA_REP = NSA_HEADS // NSA_GROUPS
HEAD_DIM = 64
CMP_BLOCK = 32
CMP_STRIDE = 16
CMP_HIDDEN = 256
SLC_BLOCK = 64
SLC_TOPK = 16
SLC_LOCAL = 2
WINDOW = 512
DIFF_HEADS = 4
REL_BUCKETS = 32
REL_MAX_EXACT = 16
REL_MAX_DIST = 128
D_FF = 2816
CONV_WIDTH = 3
LN_EPS = 1e-5
NEG_INF = -1e30
BIG = 1e30
LOG2E = math.log2(math.e)
MASK_EXP = 100

LANES = 128
SUBLANES = 8
VMEM_LIMIT = 56 * 1024 * 1024

CMP_TQ = 256
NSA_TQ = 256
DIFF_TQ = 512
NSA_FAR_KEYS = 512
DIFF_FAR_KEYS = 1024
NSA_ONLINE_KEYS = 512
DIFF_ONLINE_KEYS = 1024
ROW_TILE = 512
WIDE_TILE = 1024
FF_CHUNK = 256
HALO = 16


def _rel_breakpoints():
    n = np.arange(0, 4 * REL_MAX_DIST)
    large = REL_MAX_EXACT + (np.log(np.maximum(n, 1).astype(np.float32) / REL_MAX_EXACT)
                             / np.float32(math.log(REL_MAX_DIST / REL_MAX_EXACT))
                             * (REL_BUCKETS - REL_MAX_EXACT)).astype(np.int32)
    bucket = np.where(n < REL_MAX_EXACT, n, np.minimum(large, REL_BUCKETS - 1))
    assert np.all(np.diff(bucket) >= 0)
    return [int(np.argmax(bucket >= b)) for b in range(1, REL_BUCKETS)]


REL_BREAKS = _rel_breakpoints()


def _dot(a, b):
    return jnp.dot(a, b, preferred_element_type=F32)


def _dot_nt(a, b):
    return lax.dot_general(a, b, (((1,), (1,)), ((), ())), preferred_element_type=F32)


def _rel_bias(dist, tab_ref, head, shift):
    val = jnp.full(dist.shape, (tab_ref[0, head] - shift) * LOG2E, F32)
    for b, brk in enumerate(REL_BREAKS, start=1):
        val = jnp.where(dist >= brk, (tab_ref[b, head] - shift) * LOG2E, val)
    return val


def _gelu(x):
    c = math.sqrt(2.0 / math.pi)
    half = 0.5 * x
    return half + half * jnp.tanh(x * (c + (c * 0.044715) * (x * x)))


def _layer_norm(z, g, b):
    mu = jnp.mean(z, axis=-1, keepdims=True)
    zc = z - mu
    var = jnp.mean(zc * zc, axis=-1, keepdims=True)
    return zc * lax.rsqrt(var + LN_EPS) * g + b


def _params(*sem):
    return pltpu.CompilerParams(dimension_semantics=sem, vmem_limit_bytes=VMEM_LIMIT)


def _proj_kernel(x_ref, wm_ref, wc_ref, wg_ref, om_ref, ok_ref, ov_ref, og_ref):
    xb = x_ref[...].astype(BF16)
    n = wm_ref.shape[1]
    for c in range(0, n, 2 * LANES):
        w = min(2 * LANES, n - c)
        om_ref[:, c:c + w] = _dot(xb, wm_ref[:, c:c + w]).astype(om_ref.dtype)
    kv = _dot(xb, wc_ref[...])
    ok_ref[...] = kv[:, :LANES].astype(ok_ref.dtype)
    ov_ref[...] = kv[:, LANES:].astype(ov_ref.dtype)
    og_ref[...] = _dot(xb, wg_ref[...])


def _proj(x2d, w_main, w_cmp, w_gate):
    T, D = x2d.shape
    n = w_main.shape[1]
    row = lambda width: pl.BlockSpec((ROW_TILE, width), lambda i: (i, 0))
    const = lambda width: pl.BlockSpec((D, width), lambda i: (0, 0))
    return pl.pallas_call(
        _proj_kernel,
        grid=(T // ROW_TILE,),
        in_specs=[row(D), const(n), const(2 * LANES), const(LANES)],
        out_specs=[row(n), row(LANES), row(LANES), row(LANES)],
        out_shape=[jax.ShapeDtypeStruct((T, n), BF16), jax.ShapeDtypeStruct((T, LANES), BF16),
                   jax.ShapeDtypeStruct((T, LANES), BF16), jax.ShapeDtypeStruct((T, LANES), F32)],
        compiler_params=_params("arbitrary"),
        name="proj",
    )(x2d, w_main, w_cmp, w_gate)


def _compress_kernel(xk_ref, xv_ref, pe_ref, w1_ref, w2_ref, o_ref):
    ncp, cw = xk_ref.shape[1], xk_ref.shape[2]
    lane = lax.broadcasted_iota(jnp.int32, (ncp, cw), 1)
    group = jnp.bitwise_and(jnp.right_shift(lane, int(math.log2(HEAD_DIM))), NSA_GROUPS - 1)
    for s, x_ref in enumerate((xk_ref, xv_ref)):
        x = x_ref[0].astype(F32)
        xa = x + pe_ref[s, 0]
        xb = x + pe_ref[s, 1]
        acc = jnp.zeros((ncp, LANES), F32)
        for g in range(NSA_GROUPS):
            a = _dot(jnp.where(group == g, xa, 0.0).astype(BF16), w1_ref[s, 0])
            b = _dot(jnp.where(group == g, xb, 0.0).astype(BF16), w1_ref[s, 1])
            h = a + pltpu.roll(b, ncp - 1, 0)
            acc = acc + _dot(_gelu(h).astype(BF16), w2_ref[s, g])
        o_ref[s, 0] = acc.astype(o_ref.dtype)


def _compress(xk, xv, pe, w1, w2):
    B, ncp, cw = xk.shape
    x_spec = pl.BlockSpec((1, ncp, cw), lambda b: (b, 0, 0))
    const = lambda shape: pl.BlockSpec(shape, lambda b: (0, 0, 0, 0))
    return pl.pallas_call(
        _compress_kernel,
        grid=(B,),
        in_specs=[x_spec, x_spec, const(pe.shape), const(w1.shape), const(w2.shape)],
        out_specs=pl.BlockSpec((2, 1, ncp, LANES), lambda b: (0, b, 0, 0)),
        out_shape=jax.ShapeDtypeStruct((2, B, ncp, LANES), BF16),
        compiler_params=_params("arbitrary"),
        name="compress",
    )(xk, xv, pe, w1, w2)


def _cmp_kernel(tab_ref, q_ref, kc_ref, vc_ref, gate_ref, matt_ref, ocmp_ref, sel_ref, bias_ref, ball_ref,
                *, n_slc):
    tq = q_ref.shape[1]
    ncp = kc_ref.shape[2]
    q0 = pl.program_id(0) * tq
    lane = lax.broadcasted_iota(jnp.int32, (tq, LANES), 1)

    @pl.when((pl.program_id(0) == 0) & (pl.program_id(1) == 0))
    def _():
        for head in range(NSA_HEADS):
            bias_ref[head] = _rel_bias(lane, tab_ref, head, 0.0)

    @pl.when(pl.program_id(1) == 0)
    def _():
        t_idx = q0 + lax.broadcasted_iota(jnp.int32, (tq, ncp), 0)
        c_idx = lax.broadcasted_iota(jnp.int32, (tq, ncp), 1)
        dist = t_idx - (c_idx * CMP_STRIDE + (CMP_BLOCK - 1))
        near = jnp.clip(dist, 0, LANES - 1)
        for head in range(NSA_HEADS):
            table = bias_ref[head]
            cols = [jnp.take_along_axis(table, near[:, c:c + LANES], axis=1) for c in range(0, ncp, LANES)]
            ball_ref[head * tq:(head + 1) * tq, :] = jnp.where(dist >= 0, jnp.concatenate(cols, axis=1), NEG_INF)

    gates = jax.nn.sigmoid(gate_ref[0])
    eye = (lax.broadcasted_iota(jnp.int32, (tq, tq), 0)
           == lax.broadcasted_iota(jnp.int32, (tq, tq), 1)).astype(BF16)
    kc = kc_ref[0, 0]
    vc = vc_ref[0, 0]
    jrow_i = lax.broadcasted_iota(jnp.int32, (n_slc, tq), 0)
    cur = jnp.right_shift(q0 + lax.broadcasted_iota(jnp.int32, (n_slc, tq), 1), int(math.log2(SLC_BLOCK)))
    forced = (jrow_i == 0) | ((cur - jrow_i >= 0) & (cur - jrow_i < SLC_LOCAL))
    blk_valid = jrow_i <= cur
    jrow = jrow_i.astype(F32)
    q_parts = []
    for g in range(NSA_GROUPS):
        lane_g = (lane >= HEAD_DIM * g) & (lane < HEAD_DIM * (g + 1))
        for r in range(NSA_REP):
            qb = q_ref[0, :, r * LANES:(r + 1) * LANES]
            q_parts.append(jnp.where(lane_g, qb, jnp.zeros_like(qb)))
    logit = _dot_nt(jnp.concatenate(q_parts, axis=0), kc) + ball_ref[...]
    m = jnp.max(logit, axis=-1, keepdims=True)
    e = jnp.exp2(logit - m)
    any_valid = (q0 + lax.broadcasted_iota(jnp.int32, (tq, 1), 0) >= CMP_BLOCK - 1).astype(F32)
    p_all = e * (jnp.concatenate([any_valid] * NSA_HEADS, axis=0) / jnp.sum(e, axis=-1, keepdims=True))
    o_all = _dot(p_all.astype(BF16), vc)
    gated = [[gates[:, h * 3:h * 3 + 1] * o_all[h * tq:(h + 1) * tq]
              for h in range(g * NSA_REP, (g + 1) * NSA_REP)] for g in range(NSA_GROUPS)]
    for g in range(NSA_GROUPS):
        psum = functools.reduce(lambda a, b: a + b,
                                [p_all[h * tq:(h + 1) * tq] for h in range(g * NSA_REP, (g + 1) * NSA_REP)])
        hi = psum.astype(BF16)
        lo = (psum - hi.astype(F32)).astype(BF16)
        p_slc = _dot_nt(matt_ref[...], hi) + _dot_nt(matt_ref[...], lo)
        score = jnp.where(forced, BIG, jnp.where(blk_valid, p_slc, NEG_INF))
        sel = jnp.zeros((n_slc, tq), F32)
        for _ in range(min(SLC_TOPK, n_slc)):
            mx = jnp.max(score, axis=0, keepdims=True)
            idx = jnp.min(jnp.where(score == mx, jrow, float(n_slc)), axis=0, keepdims=True)
            hit = jrow == idx
            sel = jnp.where(hit, 1.0, sel)
            score = jnp.where(hit, -3.0e38, score)
        selm1 = (sel - 1.0).astype(BF16)
        if n_slc < LANES:
            selm1 = jnp.concatenate([selm1, jnp.zeros((LANES - n_slc, tq), BF16)], axis=0)
        sel_ref[0, g] = _dot_nt(eye, selm1).astype(sel_ref.dtype)
    for r in range(NSA_REP):
        ocmp_ref[0, :, r * LANES:(r + 1) * LANES] = jnp.where(lane < HEAD_DIM, gated[0][r], gated[1][r])


def _cmp_attention(tab, proj3, cmp_kv, gates3, matt, n_slc):
    B, S, _ = proj3.shape
    ncp = cmp_kv.shape[2]
    tq = CMP_TQ
    return pl.pallas_call(
        functools.partial(_cmp_kernel, n_slc=n_slc),
        grid=(S // tq, B),
        in_specs=[pl.BlockSpec(memory_space=pltpu.SMEM),
                  pl.BlockSpec((1, tq, 4 * LANES), lambda i, b: (b, i, 0)),
                  pl.BlockSpec((1, 1, ncp, LANES), lambda i, b: (0, b, 0, 0)),
                  pl.BlockSpec((1, 1, ncp, LANES), lambda i, b: (1, b, 0, 0)),
                  pl.BlockSpec((1, tq, LANES), lambda i, b: (b, i, 0)),
                  pl.BlockSpec((n_slc, ncp), lambda i, b: (0, 0))],
        out_specs=[pl.BlockSpec((1, tq, 4 * LANES), lambda i, b: (b, i, 0)),
                   pl.BlockSpec((1, NSA_GROUPS, tq, LANES), lambda i, b: (b, 0, i, 0))],
        out_shape=[jax.ShapeDtypeStruct((B, S, 4 * LANES), F32),
                   jax.ShapeDtypeStruct((B, NSA_GROUPS, S, LANES), BF16)],
        scratch_shapes=[pltpu.VMEM((NSA_HEADS, tq, LANES), F32), pltpu.VMEM((NSA_HEADS * tq, ncp), F32)],
        compiler_params=_params("arbitrary", "arbitrary"),
        name="cmp",
    )(tab, proj3, cmp_kv, cmp_kv, gates3, matt)


class _OnlineSoftmax:
    def __init__(self, score, value, m_ref, l_ref, acc_ref, sub_keys):
        self.score, self.value, self.sub_keys = score, value, sub_keys
        self.m_ref, self.l_ref, self.acc_ref = m_ref, l_ref, acc_ref
        m_ref[...] = jnp.full(m_ref.shape, NEG_INF, F32)
        l_ref[...] = jnp.zeros(l_ref.shape, F32)
        acc_ref[...] = jnp.zeros(acc_ref.shape, F32)

    def step(self, k0, width, bias):
        m_ref, l_ref, acc_ref = self.m_ref, self.l_ref, self.acc_ref
        for j in range(0, width, self.sub_keys):
            w = min(self.sub_keys, width - j)
            s = self.score(k0 + j, w)
            cols = []
            for c in range(0, w, LANES):
                b = None if bias is None else bias(j + c)
                cols.append(s[:, c:c + LANES] if b is None else s[:, c:c + LANES] + b)
            m_old = m_ref[...]
            m_new = jnp.maximum(m_old, jnp.max(functools.reduce(jnp.maximum, cols), axis=-1, keepdims=True))
            alpha = jnp.exp2(m_old - m_new)
            ps = [jnp.exp2(col - m_new) for col in cols]
            l_ref[...] = alpha * l_ref[...] + functools.reduce(lambda a, b: a + b, ps)
            acc_ref[...] = alpha * acc_ref[...] + _dot(
                jnp.concatenate([p.astype(BF16) for p in ps], axis=1), self.value(k0 + j, w))
            m_ref[...] = m_new

    def result(self):
        return self.acc_ref[...] / jnp.sum(self.l_ref[...], axis=-1, keepdims=True)


def _causal_far_loop(fn, qt, tq, far_keys):
    far_tiles = far_keys // tq

    def far_body(i, carry):
        fn(i * far_keys, far_keys, None)
        return carry

    lax.fori_loop(0, lax.div(jnp.maximum(qt - 1, 0), far_tiles), far_body, 0)


def _causal_segments(fn, qt, tq, far_keys, near_bias):
    n_far = jnp.maximum(qt - 1, 0)
    far_tiles = far_keys // tq
    _causal_far_loop(fn, qt, tq, far_keys)
    for rem in range(far_tiles):
        lead = rem * tq
        bias = (lambda c, lead=lead: None if c < lead else near_bias(c - lead))
        pl.when((qt >= 1) & (lax.rem(n_far, far_tiles) == rem))(
            functools.partial(fn, (qt - 1 - rem) * tq, lead + 2 * tq, bias))
    pl.when(qt == 0)(functools.partial(fn, 0, tq, lambda c: near_bias(tq + c)))


def _key_slice(k0, width, tq):
    return pl.ds(k0 if isinstance(k0, int) else pl.multiple_of(k0, tq), width)


def _rows(ref, k0, width, tq):
    return ref[0, _key_slice(k0, width, tq), :]


def _nsa_kernel(tab_ref, q_ref, ks_ref, vs_ref, kw_ref, vw_ref, sel_ref, et_ref, gate_ref, ocmp_ref,
                y_ref, tb_ref, m_ref, l_ref, acc_ref, wm_ref, wl_ref, wacc_ref):
    tq = q_ref.shape[1]
    n_win = WINDOW // tq
    qt = pl.program_id(1)

    @pl.when((pl.program_id(0) == 0) & (qt == 0))
    def _():
        ti = lax.broadcasted_iota(jnp.int32, (tq, tq), 0)
        ki = lax.broadcasted_iota(jnp.int32, (tq, tq), 1)
        for g in range(NSA_GROUPS):
            for r in range(NSA_REP):
                head = g * NSA_REP + r
                far_bias = tab_ref[REL_BUCKETS - 1, head]
                rows = slice(head * tq, (head + 1) * tq)
                tb_ref[rows, 0:tq] = jnp.where(ti < ki, 0.0, NEG_INF)
                for j in range(2, n_win):
                    tb_ref[rows, (n_win - j) * tq:(n_win - j + 1) * tq] = jnp.zeros((tq, tq), F32)
                tb_ref[rows, (n_win - 1) * tq:n_win * tq] = _rel_bias(ti - ki + tq, tab_ref, head, far_bias)
                tb_ref[rows, n_win * tq:(n_win + 1) * tq] = jnp.where(
                    ti >= ki, _rel_bias(ti - ki, tab_ref, head, far_bias), NEG_INF)

    lane = lax.broadcasted_iota(jnp.int32, (tq, LANES), 1)
    gates = jax.nn.sigmoid(gate_ref[0])

    q_parts, sel_parts = [], []
    for g in range(NSA_GROUPS):
        lane_g = (lane >= HEAD_DIM * g) & (lane < HEAD_DIM * (g + 1))
        for r in range(NSA_REP):
            qb = q_ref[0, :, r * LANES:(r + 1) * LANES]
            q_parts.append(jnp.where(lane_g, qb, jnp.zeros_like(qb)))
            sel_parts.append(sel_ref[0, g])
    q_all = jnp.concatenate(q_parts, axis=0)
    qs_all = jnp.concatenate([q_all, jnp.concatenate(sel_parts, axis=0)], axis=1)

    def bias_from(col0):
        return lambda c: tb_ref[:, col0 + c:col0 + c + LANES]

    def slc_score(k0, width):
        et = et_ref[_key_slice(k0, width, tq), :]
        return _dot_nt(qs_all, jnp.concatenate([_rows(ks_ref, k0, width, tq), et], axis=1))

    def win_score(k0, width):
        return _dot_nt(q_all, _rows(kw_ref, k0, width, tq))

    slc = _OnlineSoftmax(slc_score, lambda k0, width: _rows(vs_ref, k0, width, tq),
                         m_ref, l_ref, acc_ref, NSA_ONLINE_KEYS)
    win = _OnlineSoftmax(win_score, lambda k0, width: _rows(vw_ref, k0, width, tq),
                         wm_ref, wl_ref, wacc_ref, NSA_ONLINE_KEYS)

    def near_and_window(slc_tiles, win_tiles):
        slc_lead = (slc_tiles - 2) * tq if slc_tiles >= 2 else 0
        slc_bias = bias_from((n_win - 1) * tq) if slc_tiles >= 2 else bias_from(n_win * tq)
        slc.step((qt + 1 - slc_tiles) * tq, slc_tiles * tq,
                 lambda c: None if c < slc_lead else slc_bias(c - slc_lead))
        win.step((qt + 1 - win_tiles) * tq, win_tiles * tq, bias_from((n_win + 1 - win_tiles) * tq))

    _causal_far_loop(slc.step, qt, tq, NSA_FAR_KEYS)
    far_tiles = NSA_FAR_KEYS // tq
    left_over = lax.rem(jnp.maximum(qt - 1, 0), far_tiles)
    for n in range(n_win):
        pl.when(qt == n)(functools.partial(near_and_window, n + 1, n + 1))
    for rem in range(far_tiles):
        pl.when((qt >= n_win) & (left_over == rem))(functools.partial(near_and_window, rem + 2, n_win + 1))
    o_slc = slc.result()
    o_win = win.result()

    for r in range(NSA_REP):
        ys = []
        for g in range(NSA_GROUPS):
            head = g * NSA_REP + r
            rows = slice(head * tq, (head + 1) * tq)
            ys.append(gates[:, head * 3 + 1:head * 3 + 2] * o_slc[rows]
                      + gates[:, head * 3 + 2:head * 3 + 3] * o_win[rows])
        cols = slice(r * LANES, (r + 1) * LANES)
        y_ref[0, :, cols] = (ocmp_ref[0, :, cols] + jnp.where(lane < HEAD_DIM, ys[0], ys[1])).astype(y_ref.dtype)


def _nsa_attention(tab, proj3, sel, et, gates3, ocmp, col_blocks):
    B, S, _ = proj3.shape
    tq = NSA_TQ
    assert WINDOW % tq == 0 and WINDOW // tq >= 2 and S % tq == 0
    n_win = WINDOW // tq
    ks_c, vs_c, kw_c, vw_c = col_blocks
    rows = NSA_HEADS * tq

    def kv_spec(c):
        return pl.BlockSpec((1, S, LANES), lambda b, i: (b, 0, c))

    return pl.pallas_call(
        _nsa_kernel,
        grid=(B, S // tq),
        in_specs=[pl.BlockSpec(memory_space=pltpu.SMEM),
                  pl.BlockSpec((1, tq, 4 * LANES), lambda b, i: (b, i, 0)),
                  kv_spec(ks_c), kv_spec(vs_c), kv_spec(kw_c), kv_spec(vw_c),
                  pl.BlockSpec((1, NSA_GROUPS, tq, LANES), lambda b, i: (b, 0, i, 0)),
                  pl.BlockSpec((S, LANES), lambda b, i: (0, 0)),
                  pl.BlockSpec((1, tq, LANES), lambda b, i: (b, i, 0)),
                  pl.BlockSpec((1, tq, 4 * LANES), lambda b, i: (b, i, 0))],
        out_specs=pl.BlockSpec((1, tq, 4 * LANES), lambda b, i: (b, i, 0)),
        out_shape=jax.ShapeDtypeStruct((B, S, 4 * LANES), BF16),
        scratch_shapes=[pltpu.VMEM((rows, (n_win + 1) * tq), F32)] + [pltpu.VMEM((rows, LANES), F32)] * 6,
        compiler_params=_params("arbitrary", "arbitrary"),
        name="nsa",
    )(tab, proj3, proj3, proj3, proj3, proj3, sel, et, gates3, ocmp)


def _diff_kernel(tab_ref, q_ref, k_ref, v_ref, lam_ref, g_ref, y_ref, tb_ref, m_ref, l_ref, acc_ref,
                 *, lambda_init):
    tq = q_ref.shape[1]
    h = pl.program_id(1)
    qt = pl.program_id(2)

    @pl.when(qt == 0)
    def _():
        ti = lax.broadcasted_iota(jnp.int32, (tq, tq), 0)
        ki = lax.broadcasted_iota(jnp.int32, (tq, tq), 1)
        head = NSA_HEADS + h
        far_bias = tab_ref[REL_BUCKETS - 1, head]
        tb_ref[:, 0:tq] = _rel_bias(ti - ki + tq, tab_ref, head, far_bias)
        tb_ref[:, tq:2 * tq] = jnp.where(ti >= ki, _rel_bias(ti - ki, tab_ref, head, far_bias), NEG_INF)

    lane = lax.broadcasted_iota(jnp.int32, (tq, LANES), 1)
    qb = q_ref[0]
    zero = jnp.zeros_like(qb)
    q2 = jnp.concatenate([jnp.where(lane < HEAD_DIM, qb, zero), jnp.where(lane >= HEAD_DIM, qb, zero)], axis=0)

    def score(k0, width):
        return _dot_nt(q2, _rows(k_ref, k0, width, tq))

    def near_bias(c):
        tb = tb_ref[:, c:c + LANES]
        return jnp.concatenate([tb, tb], axis=0)

    attn = _OnlineSoftmax(score, lambda k0, width: _rows(v_ref, k0, width, tq), m_ref, l_ref, acc_ref,
                          DIFF_ONLINE_KEYS)
    _causal_segments(attn.step, qt, tq, DIFF_FAR_KEYS, near_bias)
    a = attn.result()
    lq1, lk1, lq2, lk2 = lam_ref[0:1, :], lam_ref[1:2, :], lam_ref[2:3, :], lam_ref[3:4, :]
    lam = (jnp.exp(jnp.sum(lq1 * lk1, axis=-1, keepdims=True))
           - jnp.exp(jnp.sum(lq2 * lk2, axis=-1, keepdims=True)) + lambda_init)
    o = a[:tq] - lam * a[tq:]
    o = o * lax.rsqrt(jnp.mean(o * o, axis=-1, keepdims=True) + LN_EPS) * g_ref[...]
    y_ref[0] = (o * (1.0 - lambda_init)).astype(y_ref.dtype)


def _diff_attention(tab, proj3, lam4, subln_g, col_blocks, lambda_init):
    B, S, _ = proj3.shape
    tq = min(DIFF_TQ, S)
    q_c, k_c, v_c = col_blocks
    return pl.pallas_call(
        functools.partial(_diff_kernel, lambda_init=lambda_init),
        grid=(B, DIFF_HEADS, S // tq),
        in_specs=[pl.BlockSpec(memory_space=pltpu.SMEM),
                  pl.BlockSpec((1, tq, LANES), lambda b, h, i: (b, i, q_c + h)),
                  pl.BlockSpec((1, S, LANES), lambda b, h, i: (b, 0, k_c + h)),
                  pl.BlockSpec((1, S, LANES), lambda b, h, i: (b, 0, v_c + h)),
                  pl.BlockSpec((SUBLANES, HEAD_DIM), lambda b, h, i: (0, 0)),
                  pl.BlockSpec((1, LANES), lambda b, h, i: (0, 0))],
        out_specs=pl.BlockSpec((1, tq, LANES), lambda b, h, i: (b, i, h)),
        out_shape=jax.ShapeDtypeStruct((B, S, DIFF_HEADS * LANES), BF16),
        scratch_shapes=[pltpu.VMEM((tq, 2 * tq), F32),
                        pltpu.VMEM((2 * tq, LANES), F32),
                        pltpu.VMEM((2 * tq, LANES), F32),
                        pltpu.VMEM((2 * tq, LANES), F32)],
        compiler_params=_params("arbitrary", "arbitrary", "arbitrary"),
        name="diff",
    )(tab, proj3, proj3, proj3, lam4, subln_g)


def _merge_kernel(x_ref, yn_ref, yd_ref, wgn_ref, wgd_ref, wbn_ref, wbd_ref, wo_ref, g_ref, b_ref, o_ref,
                  *, alpha):
    for r0 in range(0, x_ref.shape[0], ROW_TILE):
        rows = slice(r0, r0 + ROW_TILE)
        x = x_ref[rows, :]
        xb = x.astype(BF16)
        merged = (jax.nn.sigmoid(_dot(xb, wgn_ref[...])) * _dot(yn_ref[rows, :], wbn_ref[...])
                  + jax.nn.sigmoid(_dot(xb, wgd_ref[...])) * _dot(yd_ref[rows, :], wbd_ref[...]))
        z = alpha * x + _dot(merged.astype(BF16), wo_ref[...])
        o_ref[rows, :] = _layer_norm(z, g_ref[...], b_ref[...])


def _merge(x2d, y_nsa, y_diff, wgn, wgd, wbn, wbd, wo, ln_g, ln_b, alpha):
    T, D = x2d.shape
    const = lambda shape: pl.BlockSpec(shape, lambda i: (0, 0))
    return pl.pallas_call(
        functools.partial(_merge_kernel, alpha=alpha),
        grid=(T // WIDE_TILE,),
        in_specs=[pl.BlockSpec((WIDE_TILE, D), lambda i: (i, 0)),
                  pl.BlockSpec((WIDE_TILE, y_nsa.shape[1]), lambda i: (i, 0)),
                  pl.BlockSpec((WIDE_TILE, y_diff.shape[1]), lambda i: (i, 0)),
                  const(wgn.shape), const(wgd.shape), const(wbn.shape), const(wbd.shape), const(wo.shape),
                  const((1, D)), const((1, D))],
        out_specs=pl.BlockSpec((WIDE_TILE, D), lambda i: (i, 0)),
        out_shape=jax.ShapeDtypeStruct((T, D), F32),
        compiler_params=_params("arbitrary"),
        name="merge",
    )(x2d, y_nsa, y_diff, wgn, wgd, wbn, wbd, wo, ln_g, ln_b)


def _ffn_kernel(x_ref, halo_ref, p_ref, w_ref, cw_ref, cb_ref, wd_ref, g_ref, b_ref, wpg_ref, wpp_ref,
                o_ref, acc_ref, *, alpha, tiles_per_seq):
    d_ff = wd_ref.shape[0]
    keep = (pl.program_id(0) % tiles_per_seq != 0).astype(F32)
    for r0 in range(0, x_ref.shape[0], ROW_TILE):
        rows = slice(r0, r0 + ROW_TILE)
        x = x_ref[rows, :]
        xb = x.astype(BF16)
        hb = (halo_ref[...] if r0 == 0 else x_ref[r0 - HALO:r0, :]).astype(BF16)
        for c in range(0, d_ff, FF_CHUNK):
            cols = slice(c, c + FF_CHUNK)
            wg = w_ref[:, cols]
            gm = _dot(xb, wg)
            gh = _dot(hb, wg)
            if r0 == 0:
                gh = gh * keep
            um = _dot(xb, w_ref[:, d_ff + c:d_ff + c + FF_CHUNK])
            gext = jnp.concatenate([gh, gm], axis=0)
            g1 = pltpu.roll(gext, 1, 0)[HALO:]
            g2 = pltpu.roll(gext, 2, 0)[HALO:]
            conv = cb_ref[:, cols] + cw_ref[0:1, cols] * g2 + cw_ref[1:2, cols] * g1 + cw_ref[2:3, cols] * gm
            acc_ref[rows, cols] = (_gelu(conv) * um).astype(BF16)
        x2 = _layer_norm(alpha * x + _dot(acc_ref[rows, :], wd_ref[...]), g_ref[...], b_ref[...])
        gate = jax.nn.sigmoid(_dot(x2.astype(BF16), wpg_ref[...]))
        o_ref[rows, :] = x2 + gate * _dot(p_ref[rows, :].astype(BF16), wpp_ref[...])


def _ffn(x1, p2d, w_in, cw, cb, wd, ln_g, ln_b, wpg, wpp, alpha, seq):
    T, D = x1.shape
    tm = WIDE_TILE
    assert seq % tm == 0 and wd.shape[0] % FF_CHUNK == 0
    hb = tm // HALO
    const = lambda shape: pl.BlockSpec(shape, lambda i: (0, 0))
    return pl.pallas_call(
        functools.partial(_ffn_kernel, alpha=alpha, tiles_per_seq=seq // tm),
        grid=(T // tm,),
        in_specs=[pl.BlockSpec((tm, D), lambda i: (i, 0)),
                  pl.BlockSpec((HALO, D), lambda i: (jnp.maximum(i * hb - 1, 0), 0)),
                  pl.BlockSpec((tm, p2d.shape[1]), lambda i: (i, 0)),
                  const(w_in.shape), const(cw.shape), const(cb.shape), const(wd.shape),
                  const((1, D)), const((1, D)), const(wpg.shape), const(wpp.shape)],
        out_specs=pl.BlockSpec((tm, D), lambda i: (i, 0)),
        out_shape=jax.ShapeDtypeStruct((T, D), F32),
        scratch_shapes=[pltpu.VMEM((tm, wd.shape[0]), BF16)],
        compiler_params=_params("arbitrary"),
        name="ffn",
    )(x1, x1, p2d, w_in, cw, cb, wd, ln_g, ln_b, wpg, wpp)


def _slc_from_cmp_t(ncp, n_slc):
    ratio = SLC_BLOCK // CMP_STRIDE
    span = CMP_BLOCK // CMP_STRIDE
    mat = np.zeros((n_slc, ncp), np.float32)
    for j in range(n_slc):
        for m in range(ratio):
            for n in range(span):
                i = ratio * j + m - n
                if 0 <= i < ncp - 1:
                    mat[j, i] += 1.0
    return mat


def _layer(x, p_l, w_in, pe_k, w1_k, w2_k, pe_v, w1_v, w2_v, lq1, lk1, lq2, lk2, subln_g, w_bn, w_bd, w_out,
           ln1_g, ln1_b, w_ffn_in, conv_w, conv_b, w_down, ln2_g, ln2_b, w_pp, w_pg, tab, lambda_init,
           alpha):
    B, S, D = x.shape
    T = B * S
    ncp = S // CMP_STRIDE
    n_slc = S // SLC_BLOCK
    q_w = NSA_HEADS * HEAD_DIM
    kv_w = NSA_GROUPS * HEAD_DIM
    dqk_w = DIFF_HEADS * 2 * HEAD_DIM
    sizes = (q_w,) + (kv_w,) * 6 + (NSA_HEADS * 3, dqk_w, dqk_w, dqk_w, D, D)
    offs = np.concatenate([[0], np.cumsum(sizes)])
    col = lambda i: w_in[:, int(offs[i]):int(offs[i + 1])]
    scale = HEAD_DIM ** -0.5 * LOG2E

    n_idx = np.arange(q_w)
    perm = (NSA_REP * ((n_idx % LANES) // HEAD_DIM) + n_idx // LANES) * HEAD_DIM + n_idx % HEAD_DIM
    w_main = jnp.concatenate([col(0)[:, perm] * scale] + [col(i) for i in range(3, 7)]
                             + [col(8) * scale, col(9), col(10)], axis=1).astype(BF16)
    w_cmp = jnp.concatenate([col(1), col(2)], axis=1).astype(BF16)
    w_gate = jnp.pad(col(7), ((0, 0), (0, LANES - NSA_HEADS * 3))).astype(BF16)
    x2d = x.reshape(T, D)
    proj, k_cmp, v_cmp, gates = _proj(x2d, w_main, w_cmp, w_gate)
    proj3 = proj.reshape(B, S, proj.shape[1])
    gates3 = gates.reshape(B, S, LANES)
    c_kslc, c_vslc, c_kwin, c_vwin = (q_w // LANES + i for i in range(4))
    c_dq = q_w // LANES + 4
    c_dk = c_dq + DIFF_HEADS
    c_dv = c_dk + DIFF_HEADS

    cw = CMP_STRIDE * kv_w
    rep = lambda a: jnp.broadcast_to(a.reshape(2, 2, CMP_STRIDE, 1, HEAD_DIM, -1),
                                     (2, 2, CMP_STRIDE, NSA_GROUPS, HEAD_DIM, a.shape[-1]))
    pe = rep(jnp.stack([pe_k, pe_v])[..., None]).reshape(2, 2, 1, cw)
    w1 = rep(jnp.stack([w1_k, w1_v])).reshape(2, 2, cw, CMP_HIDDEN).astype(BF16)
    w2 = jnp.stack([w2_k, w2_v])
    w2p = jnp.stack([jnp.pad(w2, ((0, 0), (0, 0), (g * HEAD_DIM, LANES - (g + 1) * HEAD_DIM)))
                     for g in range(NSA_GROUPS)], axis=1).astype(BF16)
    cmp_kv = _compress(k_cmp.reshape(B, ncp, cw), v_cmp.reshape(B, ncp, cw), pe, w1, w2p)

    matt = jnp.asarray(_slc_from_cmp_t(ncp, n_slc), BF16)
    ocmp, sel = _cmp_attention(tab, proj3, cmp_kv, gates3, matt, n_slc)
    et_np = np.zeros((S, LANES), np.float32)
    et_np[np.arange(S), np.arange(S) // SLC_BLOCK] = 2.0 ** MASK_EXP
    y_nsa = _nsa_attention(tab, proj3, sel, jnp.asarray(et_np, BF16), gates3, ocmp,
                           (c_kslc, c_vslc, c_kwin, c_vwin))

    lam4 = jnp.pad(jnp.stack([lq1, lk1, lq2, lk2]), ((0, SUBLANES - 4), (0, 0)))
    y_diff = _diff_attention(tab, proj3, lam4, subln_g.reshape(1, LANES), (c_dq, c_dk, c_dv), lambda_init)

    x1 = _merge(x2d, y_nsa.reshape(T, q_w), y_diff.reshape(T, dqk_w),
                col(11).astype(BF16), col(12).astype(BF16), w_bn[perm].astype(BF16), w_bd.astype(BF16),
                w_out.astype(BF16), ln1_g.reshape(1, D), ln1_b.reshape(1, D), alpha)

    cw = jnp.pad(conv_w, ((0, SUBLANES - CONV_WIDTH), (0, 0)))
    out = _ffn(x1, p_l.reshape(T, p_l.shape[-1]), w_ffn_in.astype(BF16), cw, conv_b.reshape(1, -1),
               w_down.astype(BF16), ln2_g.reshape(1, D), ln2_b.reshape(1, D),
               w_pg.astype(BF16), w_pp.astype(BF16), alpha, S)
    return out.reshape(B, S, D)


def kernel(x, p, w_in, nsa_cmp_pe_k, nsa_cmp_w1_k, nsa_cmp_w2_k, nsa_cmp_pe_v, nsa_cmp_w1_v, nsa_cmp_w2_v, diff_lambda_q1, diff_lambda_k1, diff_lambda_q2, diff_lambda_k2, diff_subln_g, w_branch_nsa, w_branch_diff, w_out, ln1_g, ln1_b, w_ffn_in, ffn_conv_w, ffn_conv_b, w_ffn_down, ln2_g, ln2_b, w_ple_proj, w_ple_gate, rel_bias_table):
    depth = w_in.shape[0]
    alpha = (2.0 * depth) ** 0.25
    for l in range(depth):
        lambda_init = 0.8 - 0.6 * math.exp(-0.3 * l)
        x = _layer(x, p[l], w_in[l], nsa_cmp_pe_k[l], nsa_cmp_w1_k[l], nsa_cmp_w2_k[l], nsa_cmp_pe_v[l],
                   nsa_cmp_w1_v[l], nsa_cmp_w2_v[l], diff_lambda_q1[l], diff_lambda_k1[l], diff_lambda_q2[l],
                   diff_lambda_k2[l], diff_subln_g[l], w_branch_nsa[l], w_branch_diff[l], w_out[l], ln1_g[l],
                   ln1_b[l], w_ffn_in[l], ffn_conv_w[l], ffn_conv_b[l], w_ffn_down[l], ln2_g[l], ln2_b[l],
                   w_ple_proj[l], w_ple_gate[l], rel_bias_table, lambda_init, alpha)
    return x
```

```python
import functools
import math

import jax
import jax.numpy as jnp
import numpy as np
from jax import lax
from jax.experimental import pallas as pl
from jax.experimental.pallas import tpu as pltpu

F32 = jnp.float32
BF16 = jnp.bfloat16

NSA_HEADS = 8
NSA_GROUPS = 2
NSA_REP = NSA_HEADS // NSA_GROUPS
HEAD_DIM = 64
CMP_BLOCK = 32
CMP_STRIDE = 16
CMP_HIDDEN = 256
SLC_BLOCK = 64
SLC_TOPK = 16
SLC_LOCAL = 2
WINDOW = 512
DIFF_HEADS = 4
REL_BUCKETS = 32
REL_MAX_EXACT = 16
REL_MAX_DIST = 128
D_FF = 2816
CONV_WIDTH = 3
LN_EPS = 1e-5
NEG_INF = -1e30
BIG = 1e30
LOG2E = math.log2(math.e)
MASK_EXP = 100

LANES = 128
SUBLANES = 8
VMEM_LIMIT = 56 * 1024 * 1024

CMP_TQ = 256
NSA_TQ = 256
DIFF_TQ = 512
NSA_FAR_KEYS = 1024
DIFF_FAR_KEYS = 2048
NSA_ONLINE_KEYS = 512
DIFF_ONLINE_KEYS = 1024
ROW_TILE = 512
WIDE_TILE = 1024
FF_CHUNK = 256
HALO = 16


def _rel_breakpoints():
    n = np.arange(0, 4 * REL_MAX_DIST)
    large = REL_MAX_EXACT + (np.log(np.maximum(n, 1).astype(np.float32) / REL_MAX_EXACT)
                             / np.float32(math.log(REL_MAX_DIST / REL_MAX_EXACT))
                             * (REL_BUCKETS - REL_MAX_EXACT)).astype(np.int32)
    bucket = np.where(n < REL_MAX_EXACT, n, np.minimum(large, REL_BUCKETS - 1))
    assert np.all(np.diff(bucket) >= 0)
    return [int(np.argmax(bucket >= b)) for b in range(1, REL_BUCKETS)]


REL_BREAKS = _rel_breakpoints()


def _dot(a, b):
    return jnp.dot(a, b, preferred_element_type=F32)


def _dot_nt(a, b):
    return lax.dot_general(a, b, (((1,), (1,)), ((), ())), preferred_element_type=F32)


def _rel_bias(dist, tab_ref, head, shift):
    val = jnp.full(dist.shape, (tab_ref[0, head] - shift) * LOG2E, F32)
    for b, brk in enumerate(REL_BREAKS, start=1):
        val = jnp.where(dist >= brk, (tab_ref[b, head] - shift) * LOG2E, val)
    return val


def _gelu(x):
    c = math.sqrt(2.0 / math.pi)
    half = 0.5 * x
    return half + half * jnp.tanh(x * (c + (c * 0.044715) * (x * x)))


def _layer_norm(z, g, b):
    mu = jnp.mean(z, axis=-1, keepdims=True)
    zc = z - mu
    var = jnp.mean(zc * zc, axis=-1, keepdims=True)
    return zc * lax.rsqrt(var + LN_EPS) * g + b


def _params(*sem):
    return pltpu.CompilerParams(dimension_semantics=sem, vmem_limit_bytes=VMEM_LIMIT)


def _proj_kernel(x_ref, wm_ref, wc_ref, wg_ref, om_ref, ok_ref, ov_ref, og_ref):
    xb = x_ref[...].astype(BF16)
    n = wm_ref.shape[1]
    for c in range(0, n, 2 * LANES):
        w = min(2 * LANES, n - c)
        om_ref[:, c:c + w] = _dot(xb, wm_ref[:, c:c + w]).astype(om_ref.dtype)
    kv = _dot(xb, wc_ref[...])
    ok_ref[...] = kv[:, :LANES].astype(ok_ref.dtype)
    ov_ref[...] = kv[:, LANES:].astype(ov_ref.dtype)
    og_ref[...] = _dot(xb, wg_ref[...])


def _proj(x2d, w_main, w_cmp, w_gate):
    T, D = x2d.shape
    n = w_main.shape[1]
    row = lambda width: pl.BlockSpec((ROW_TILE, width), lambda i: (i, 0))
    const = lambda width: pl.BlockSpec((D, width), lambda i: (0, 0))
    return pl.pallas_call(
        _proj_kernel,
        grid=(T // ROW_TILE,),
        in_specs=[row(D), const(n), const(2 * LANES), const(LANES)],
        out_specs=[row(n), row(LANES), row(LANES), row(LANES)],
        out_shape=[jax.ShapeDtypeStruct((T, n), BF16), jax.ShapeDtypeStruct((T, LANES), BF16),
                   jax.ShapeDtypeStruct((T, LANES), BF16), jax.ShapeDtypeStruct((T, LANES), F32)],
        compiler_params=_params("arbitrary"),
        name="proj",
    )(x2d, w_main, w_cmp, w_gate)


def _compress_kernel(xk_ref, xv_ref, pe_ref, w1_ref, w2_ref, o_ref):
    ncp, cw = xk_ref.shape[1], xk_ref.shape[2]
    lane = lax.broadcasted_iota(jnp.int32, (ncp, cw), 1)
    group = jnp.bitwise_and(jnp.right_shift(lane, int(math.log2(HEAD_DIM))), NSA_GROUPS - 1)
    for s, x_ref in enumerate((xk_ref, xv_ref)):
        x = x_ref[0].astype(F32)
        xa = x + pe_ref[s, 0]
        xb = x + pe_ref[s, 1]
        acc = jnp.zeros((ncp, LANES), F32)
        for g in range(NSA_GROUPS):
            a = _dot(jnp.where(group == g, xa, 0.0).astype(BF16), w1_ref[s, 0])
            b = _dot(jnp.where(group == g, xb, 0.0).astype(BF16), w1_ref[s, 1])
            h = a + pltpu.roll(b, ncp - 1, 0)
            acc = acc + _dot(_gelu(h).astype(BF16), w2_ref[s, g])
        o_ref[s, 0] = acc.astype(o_ref.dtype)


def _compress(xk, xv, pe, w1, w2):
    B, ncp, cw = xk.shape
    x_spec = pl.BlockSpec((1, ncp, cw), lambda b: (b, 0, 0))
    const = lambda shape: pl.BlockSpec(shape, lambda b: (0, 0, 0, 0))
    return pl.pallas_call(
        _compress_kernel,
        grid=(B,),
        in_specs=[x_spec, x_spec, const(pe.shape), const(w1.shape), const(w2.shape)],
        out_specs=pl.BlockSpec((2, 1, ncp, LANES), lambda b: (0, b, 0, 0)),
        out_shape=jax.ShapeDtypeStruct((2, B, ncp, LANES), BF16),
        compiler_params=_params("arbitrary"),
        name="compress",
    )(xk, xv, pe, w1, w2)


def _cmp_kernel(tab_ref, q_ref, kc_ref, vc_ref, gate_ref, matt_ref, ocmp_ref, sel_ref, bias_ref, ball_ref,
                *, n_slc):
    tq = q_ref.shape[1]
    ncp = kc_ref.shape[2]
    q0 = pl.program_id(0) * tq
    lane = lax.broadcasted_iota(jnp.int32, (tq, LANES), 1)

    @pl.when((pl.program_id(0) == 0) & (pl.program_id(1) == 0))
    def _():
        for head in range(NSA_HEADS):
            bias_ref[head] = _rel_bias(lane, tab_ref, head, 0.0)

    @pl.when(pl.program_id(1) == 0)
    def _():
        t_idx = q0 + lax.broadcasted_iota(jnp.int32, (tq, ncp), 0)
        c_idx = lax.broadcasted_iota(jnp.int32, (tq, ncp), 1)
        dist = t_idx - (c_idx * CMP_STRIDE + (CMP_BLOCK - 1))
        near = jnp.clip(dist, 0, LANES - 1)
        for head in range(NSA_HEADS):
            table = bias_ref[head]
            cols = [jnp.take_along_axis(table, near[:, c:c + LANES], axis=1) for c in range(0, ncp, LANES)]
            ball_ref[head * tq:(head + 1) * tq, :] = jnp.where(dist >= 0, jnp.concatenate(cols, axis=1), NEG_INF)

    gates = jax.nn.sigmoid(gate_ref[0])
    eye = (lax.broadcasted_iota(jnp.int32, (tq, tq), 0)
           == lax.broadcasted_iota(jnp.int32, (tq, tq), 1)).astype(BF16)
    kc = kc_ref[0, 0]
    vc = vc_ref[0, 0]
    jrow_i = lax.broadcasted_iota(jnp.int32, (n_slc, tq), 0)
    cur = jnp.right_shift(q0 + lax.broadcasted_iota(jnp.int32, (n_slc, tq), 1), int(math.log2(SLC_BLOCK)))
    forced = (jrow_i == 0) | ((cur - jrow_i >= 0) & (cur - jrow_i < SLC_LOCAL))
    blk_valid = jrow_i <= cur
    jrow = jrow_i.astype(F32)
    q_parts = []
    for g in range(NSA_GROUPS):
        lane_g = (lane >= HEAD_DIM * g) & (lane < HEAD_DIM * (g + 1))
        for r in range(NSA_REP):
            qb = q_ref[0, :, r * LANES:(r + 1) * LANES]
            q_parts.append(jnp.where(lane_g, qb, jnp.zeros_like(qb)))
    logit = _dot_nt(jnp.concatenate(q_parts, axis=0), kc) + ball_ref[...]
    m = jnp.max(logit, axis=-1, keepdims=True)
    e = jnp.exp2(logit - m)
    any_valid = (q0 + lax.broadcasted_iota(jnp.int32, (tq, 1), 0) >= CMP_BLOCK - 1).astype(F32)
    p_all = e * (jnp.concatenate([any_valid] * NSA_HEADS, axis=0) / jnp.sum(e, axis=-1, keepdims=True))
    o_all = _dot(p_all.astype(BF16), vc)
    gated = [[gates[:, h * 3:h * 3 + 1] * o_all[h * tq:(h + 1) * tq]
              for h in range(g * NSA_REP, (g + 1) * NSA_REP)] for g in range(NSA_GROUPS)]
    for g in range(NSA_GROUPS):
        psum = functools.reduce(lambda a, b: a + b,
                                [p_all[h * tq:(h + 1) * tq] for h in range(g * NSA_REP, (g + 1) * NSA_REP)])
        hi = psum.astype(BF16)
        lo = (psum - hi.astype(F32)).astype(BF16)
        p_slc = _dot_nt(matt_ref[...], hi) + _dot_nt(matt_ref[...], lo)
        score = jnp.where(forced, BIG, jnp.where(blk_valid, p_slc, NEG_INF))
        sel = jnp.zeros((n_slc, tq), F32)
        for _ in range(min(SLC_TOPK, n_slc)):
            mx = jnp.max(score, axis=0, keepdims=True)
            idx = jnp.min(jnp.where(score == mx, jrow, float(n_slc)), axis=0, keepdims=True)
            hit = jrow == idx
            sel = jnp.where(hit, 1.0, sel)
            score = jnp.where(hit, -3.0e38, score)
        selm1 = (sel - 1.0).astype(BF16)
        if n_slc < LANES:
            selm1 = jnp.concatenate([selm1, jnp.zeros((LANES - n_slc, tq), BF16)], axis=0)
        sel_ref[0, g] = _dot_nt(eye, selm1).astype(sel_ref.dtype)
    for r in range(NSA_REP):
        ocmp_ref[0, :, r * LANES:(r + 1) * LANES] = jnp.where(lane < HEAD_DIM, gated[0][r], gated[1][r])


def _cmp_attention(tab, proj3, cmp_kv, gates3, matt, n_slc):
    B, S, _ = proj3.shape
    ncp = cmp_kv.shape[2]
    tq = CMP_TQ
    return pl.pallas_call(
        functools.partial(_cmp_kernel, n_slc=n_slc),
        grid=(S // tq, B),
        in_specs=[pl.BlockSpec(memory_space=pltpu.SMEM),
                  pl.BlockSpec((1, tq, 4 * LANES), lambda i, b: (b, i, 0)),
                  pl.BlockSpec((1, 1, ncp, LANES), lambda i, b: (0, b, 0, 0)),
                  pl.BlockSpec((1, 1, ncp, LANES), lambda i, b: (1, b, 0, 0)),
                  pl.BlockSpec((1, tq, LANES), lambda i, b: (b, i, 0)),
                  pl.BlockSpec((n_slc, ncp), lambda i, b: (0, 0))],
        out_specs=[pl.BlockSpec((1, tq, 4 * LANES), lambda i, b: (b, i, 0)),
                   pl.BlockSpec((1, NSA_GROUPS, tq, LANES), lambda i, b: (b, 0, i, 0))],
        out_shape=[jax.ShapeDtypeStruct((B, S, 4 * LANES), F32),
                   jax.ShapeDtypeStruct((B, NSA_GROUPS, S, LANES), BF16)],
        scratch_shapes=[pltpu.VMEM((NSA_HEADS, tq, LANES), F32), pltpu.VMEM((NSA_HEADS * tq, ncp), F32)],
        compiler_params=_params("arbitrary", "arbitrary"),
        name="cmp",
    )(tab, proj3, cmp_kv, cmp_kv, gates3, matt)


class _OnlineSoftmax:
    def __init__(self, score, value, m_ref, l_ref, acc_ref, sub_keys):
        self.score, self.value, self.sub_keys = score, value, sub_keys
        self.m_ref, self.l_ref, self.acc_ref = m_ref, l_ref, acc_ref
        m_ref[...] = jnp.full(m_ref.shape, NEG_INF, F32)
        l_ref[...] = jnp.zeros(l_ref.shape, F32)
        acc_ref[...] = jnp.zeros(acc_ref.shape, F32)

    def step(self, k0, width, bias):
        m_ref, l_ref, acc_ref = self.m_ref, self.l_ref, self.acc_ref
        for j in range(0, width, self.sub_keys):
            w = min(self.sub_keys, width - j)
            s = self.score(k0 + j, w)
            cols = []
            for c in range(0, w, LANES):
                b = None if bias is None else bias(j + c)
                cols.append(s[:, c:c + LANES] if b is None else s[:, c:c + LANES] + b)
            m_old = m_ref[...]
            m_new = jnp.maximum(m_old, jnp.max(functools.reduce(jnp.maximum, cols), axis=-1, keepdims=True))
            alpha = jnp.exp2(m_old - m_new)
            ps = [jnp.exp2(col - m_new) for col in cols]
            l_ref[...] = alpha * l_ref[...] + functools.reduce(lambda a, b: a + b, ps)
            acc_ref[...] = alpha * acc_ref[...] + _dot(
                jnp.concatenate([p.astype(BF16) for p in ps], axis=1), self.value(k0 + j, w))
            m_ref[...] = m_new

    def result(self):
        return self.acc_ref[...] / jnp.sum(self.l_ref[...], axis=-1, keepdims=True)


def _causal_far_loop(fn, qt, tq, far_keys):
    far_tiles = far_keys // tq

    def far_body(i, carry):
        fn(i * far_keys, far_keys, None)
        return carry

    lax.fori_loop(0, lax.div(jnp.maximum(qt - 1, 0), far_tiles), far_body, 0)


def _causal_segments(fn, qt, tq, far_keys, near_bias):
    n_far = jnp.maximum(qt - 1, 0)
    far_tiles = far_keys // tq
    _causal_far_loop(fn, qt, tq, far_keys)
    for rem in range(far_tiles):
        lead = rem * tq
        bias = (lambda c, lead=lead: None if c < lead else near_bias(c - lead))
        pl.when((qt >= 1) & (lax.rem(n_far, far_tiles) == rem))(
            functools.partial(fn, (qt - 1 - rem) * tq, lead + 2 * tq, bias))
    pl.when(qt == 0)(functools.partial(fn, 0, tq, lambda c: near_bias(tq + c)))


def _key_slice(k0, width, tq):
    return pl.ds(k0 if isinstance(k0, int) else pl.multiple_of(k0, tq), width)


def _rows(ref, k0, width, tq):
    return ref[0, _key_slice(k0, width, tq), :]


def _nsa_kernel(tab_ref, q_ref, ks_ref, vs_ref, kw_ref, vw_ref, sel_ref, et_ref, gate_ref, ocmp_ref,
                y_ref, tb_ref, m_ref, l_ref, acc_ref, wm_ref, wl_ref, wacc_ref):
    tq = q_ref.shape[1]
    n_win = WINDOW // tq
    qt = pl.program_id(1)

    @pl.when((pl.program_id(0) == 0) & (qt == 0))
    def _():
        ti = lax.broadcasted_iota(jnp.int32, (tq, tq), 0)
        ki = lax.broadcasted_iota(jnp.int32, (tq, tq), 1)
        for g in range(NSA_GROUPS):
            for r in range(NSA_REP):
                head = g * NSA_REP + r
                far_bias = tab_ref[REL_BUCKETS - 1, head]
                rows = slice(head * tq, (head + 1) * tq)
                tb_ref[rows, 0:tq] = jnp.where(ti < ki, 0.0, NEG_INF)
                for j in range(2, n_win):
                    tb_ref[rows, (n_win - j) * tq:(n_win - j + 1) * tq] = jnp.zeros((tq, tq), F32)
                tb_ref[rows, (n_win - 1) * tq:n_win * tq] = _rel_bias(ti - ki + tq, tab_ref, head, far_bias)
                tb_ref[rows, n_win * tq:(n_win + 1) * tq] = jnp.where(
                    ti >= ki, _rel_bias(ti - ki, tab_ref, head, far_bias), NEG_INF)

    lane = lax.broadcasted_iota(jnp.int32, (tq, LANES), 1)
    gates = jax.nn.sigmoid(gate_ref[0])

    q_parts, sel_parts = [], []
    for g in range(NSA_GROUPS):
        lane_g = (lane >= HEAD_DIM * g) & (lane < HEAD_DIM * (g + 1))
        for r in range(NSA_REP):
            qb = q_ref[0, :, r * LANES:(r + 1) * LANES]
            q_parts.append(jnp.where(lane_g, qb, jnp.zeros_like(qb)))
            sel_parts.append(sel_ref[0, g])
    q_all = jnp.concatenate(q_parts, axis=0)
    qs_all = jnp.concatenate([q_all, jnp.concatenate(sel_parts, axis=0)], axis=1)

    def bias_from(col0):
        return lambda c: tb_ref[:, col0 + c:col0 + c + LANES]

    def slc_score(k0, width):
        et = et_ref[_key_slice(k0, width, tq), :]
        return _dot_nt(qs_all, jnp.concatenate([_rows(ks_ref, k0, width, tq), et], axis=1))

    def win_score(k0, width):
        return _dot_nt(q_all, _rows(kw_ref, k0, width, tq))

    slc = _OnlineSoftmax(slc_score, lambda k0, width: _rows(vs_ref, k0, width, tq),
                         m_ref, l_ref, acc_ref, NSA_ONLINE_KEYS)
    win = _OnlineSoftmax(win_score, lambda k0, width: _rows(vw_ref, k0, width, tq),
                         wm_ref, wl_ref, wacc_ref, NSA_ONLINE_KEYS)

    def near_and_window(slc_tiles, win_tiles):
        slc_lead = (slc_tiles - 2) * tq if slc_tiles >= 2 else 0
        slc_bias = bias_from((n_win - 1) * tq) if slc_tiles >= 2 else bias_from(n_win * tq)
        slc.step((qt + 1 - slc_tiles) * tq, slc_tiles * tq,
                 lambda c: None if c < slc_lead else slc_bias(c - slc_lead))
        win.step((qt + 1 - win_tiles) * tq, win_tiles * tq, bias_from((n_win + 1 - win_tiles) * tq))

    _causal_far_loop(slc.step, qt, tq, NSA_FAR_KEYS)
    far_tiles = NSA_FAR_KEYS // tq
    left_over = lax.rem(jnp.maximum(qt - 1, 0), far_tiles)
    for n in range(n_win):
        pl.when(qt == n)(functools.partial(near_and_window, n + 1, n + 1))
    for rem in range(far_tiles):
        pl.when((qt >= n_win) & (left_over == rem))(functools.partial(near_and_window, rem + 2, n_win + 1))
    o_slc = slc.result()
    o_win = win.result()

    for r in range(NSA_REP):
        ys = []
        for g in range(NSA_GROUPS):
            head = g * NSA_REP + r
            rows = slice(head * tq, (head + 1) * tq)
            ys.append(gates[:, head * 3 + 1:head * 3 + 2] * o_slc[rows]
                      + gates[:, head * 3 + 2:head * 3 + 3] * o_win[rows])
        cols = slice(r * LANES, (r + 1) * LANES)
        y_ref[0, :, cols] = (ocmp_ref[0, :, cols] + jnp.where(lane < HEAD_DIM, ys[0], ys[1])).astype(y_ref.dtype)


def _nsa_attention(tab, proj3, sel, et, gates3, ocmp, col_blocks):
    B, S, _ = proj3.shape
    tq = NSA_TQ
    assert WINDOW % tq == 0 and WINDOW // tq >= 2 and S % tq == 0
    n_win = WINDOW // tq
    ks_c, vs_c, kw_c, vw_c = col_blocks
    rows = NSA_HEADS * tq

    def kv_spec(c):
        return pl.BlockSpec((1, S, LANES), lambda b, i: (b, 0, c))

    return pl.pallas_call(
        _nsa_kernel,
        grid=(B, S // tq),
        in_specs=[pl.BlockSpec(memory_space=pltpu.SMEM),
                  pl.BlockSpec((1, tq, 4 * LANES), lambda b, i: (b, i, 0)),
                  kv_spec(ks_c), kv_spec(vs_c), kv_spec(kw_c), kv_spec(vw_c),
                  pl.BlockSpec((1, NSA_GROUPS, tq, LANES), lambda b, i: (b, 0, i, 0)),
                  pl.BlockSpec((S, LANES), lambda b, i: (0, 0)),
                  pl.BlockSpec((1, tq, LANES), lambda b, i: (b, i, 0)),
                  pl.BlockSpec((1, tq, 4 * LANES), lambda b, i: (b, i, 0))],
        out_specs=pl.BlockSpec((1, tq, 4 * LANES), lambda b, i: (b, i, 0)),
        out_shape=jax.ShapeDtypeStruct((B, S, 4 * LANES), BF16),
        scratch_shapes=[pltpu.VMEM((rows, (n_win + 1) * tq), F32)] + [pltpu.VMEM((rows, LANES), F32)] * 6,
        compiler_params=_params("arbitrary", "arbitrary"),
        name="nsa",
    )(tab, proj3, proj3, proj3, proj3, proj3, sel, et, gates3, ocmp)


def _diff_kernel(tab_ref, q_ref, k_ref, v_ref, lam_ref, g_ref, y_ref, tb_ref, m_ref, l_ref, acc_ref,
                 *, lambda_init):
    tq = q_ref.shape[1]
    h = pl.program_id(1)
    qt = pl.program_id(2)

    @pl.when(qt == 0)
    def _():
        ti = lax.broadcasted_iota(jnp.int32, (tq, tq), 0)
        ki = lax.broadcasted_iota(jnp.int32, (tq, tq), 1)
        head = NSA_HEADS + h
        far_bias = tab_ref[REL_BUCKETS - 1, head]
        tb_ref[:, 0:tq] = _rel_bias(ti - ki + tq, tab_ref, head, far_bias)
        tb_ref[:, tq:2 * tq] = jnp.where(ti >= ki, _rel_bias(ti - ki, tab_ref, head, far_bias), NEG_INF)

    lane = lax.broadcasted_iota(jnp.int32, (tq, LANES), 1)
    qb = q_ref[0]
    zero = jnp.zeros_like(qb)
    q2 = jnp.concatenate([jnp.where(lane < HEAD_DIM, qb, zero), jnp.where(lane >= HEAD_DIM, qb, zero)], axis=0)

    def score(k0, width):
        return _dot_nt(q2, _rows(k_ref, k0, width, tq))

    def near_bias(c):
        tb = tb_ref[:, c:c + LANES]
        return jnp.concatenate([tb, tb], axis=0)

    attn = _OnlineSoftmax(score, lambda k0, width: _rows(v_ref, k0, width, tq), m_ref, l_ref, acc_ref,
                          DIFF_ONLINE_KEYS)
    _causal_segments(attn.step, qt, tq, DIFF_FAR_KEYS, near_bias)
    a = attn.result()
    lq1, lk1, lq2, lk2 = lam_ref[0:1, :], lam_ref[1:2, :], lam_ref[2:3, :], lam_ref[3:4, :]
    lam = (jnp.exp(jnp.sum(lq1 * lk1, axis=-1, keepdims=True))
           - jnp.exp(jnp.sum(lq2 * lk2, axis=-1, keepdims=True)) + lambda_init)
    o = a[:tq] - lam * a[tq:]
    o = o * lax.rsqrt(jnp.mean(o * o, axis=-1, keepdims=True) + LN_EPS) * g_ref[...]
    y_ref[0] = (o * (1.0 - lambda_init)).astype(y_ref.dtype)


def _diff_attention(tab, proj3, lam4, subln_g, col_blocks, lambda_init):
    B, S, _ = proj3.shape
    tq = min(DIFF_TQ, S)
    q_c, k_c, v_c = col_blocks
    return pl.pallas_call(
        functools.partial(_diff_kernel, lambda_init=lambda_init),
        grid=(B, DIFF_HEADS, S // tq),
        in_specs=[pl.BlockSpec(memory_space=pltpu.SMEM),
                  pl.BlockSpec((1, tq, LANES), lambda b, h, i: (b, i, q_c + h)),
                  pl.BlockSpec((1, S, LANES), lambda b, h, i: (b, 0, k_c + h)),
                  pl.BlockSpec((1, S, LANES), lambda b, h, i: (b, 0, v_c + h)),
                  pl.BlockSpec((SUBLANES, HEAD_DIM), lambda b, h, i: (0, 0)),
                  pl.BlockSpec((1, LANES), lambda b, h, i: (0, 0))],
        out_specs=pl.BlockSpec((1, tq, LANES), lambda b, h, i: (b, i, h)),
        out_shape=jax.ShapeDtypeStruct((B, S, DIFF_HEADS * LANES), BF16),
        scratch_shapes=[pltpu.VMEM((tq, 2 * tq), F32),
                        pltpu.VMEM((2 * tq, LANES), F32),
                        pltpu.VMEM((2 * tq, LANES), F32),
                        pltpu.VMEM((2 * tq, LANES), F32)],
        compiler_params=_params("arbitrary", "arbitrary", "arbitrary"),
        name="diff",
    )(tab, proj3, proj3, proj3, lam4, subln_g)


def _merge_kernel(x_ref, yn_ref, yd_ref, wgn_ref, wgd_ref, wbn_ref, wbd_ref, wo_ref, g_ref, b_ref, o_ref,
                  *, alpha):
    for r0 in range(0, x_ref.shape[0], ROW_TILE):
        rows = slice(r0, r0 + ROW_TILE)
        x = x_ref[rows, :]
        xb = x.astype(BF16)
        merged = (jax.nn.sigmoid(_dot(xb, wgn_ref[...])) * _dot(yn_ref[rows, :], wbn_ref[...])
                  + jax.nn.sigmoid(_dot(xb, wgd_ref[...])) * _dot(yd_ref[rows, :], wbd_ref[...]))
        z = alpha * x + _dot(merged.astype(BF16), wo_ref[...])
        o_ref[rows, :] = _layer_norm(z, g_ref[...], b_ref[...])


def _merge(x2d, y_nsa, y_diff, wgn, wgd, wbn, wbd, wo, ln_g, ln_b, alpha):
    T, D = x2d.shape
    const = lambda shape: pl.BlockSpec(shape, lambda i: (0, 0))
    return pl.pallas_call(
        functools.partial(_merge_kernel, alpha=alpha),
        grid=(T // WIDE_TILE,),
        in_specs=[pl.BlockSpec((WIDE_TILE, D), lambda i: (i, 0)),
                  pl.BlockSpec((WIDE_TILE, y_nsa.shape[1]), lambda i: (i, 0)),
                  pl.BlockSpec((WIDE_TILE, y_diff.shape[1]), lambda i: (i, 0)),
                  const(wgn.shape), const(wgd.shape), const(wbn.shape), const(wbd.shape), const(wo.shape),
                  const((1, D)), const((1, D))],
        out_specs=pl.BlockSpec((WIDE_TILE, D), lambda i: (i, 0)),
        out_shape=jax.ShapeDtypeStruct((T, D), F32),
        compiler_params=_params("arbitrary"),
        name="merge",
    )(x2d, y_nsa, y_diff, wgn, wgd, wbn, wbd, wo, ln_g, ln_b)


def _ffn_kernel(x_ref, halo_ref, p_ref, w_ref, cw_ref, cb_ref, wd_ref, g_ref, b_ref, wpg_ref, wpp_ref,
                o_ref, acc_ref, *, alpha, tiles_per_seq):
    d_ff = wd_ref.shape[0]
    keep = (pl.program_id(0) % tiles_per_seq != 0).astype(F32)
    for r0 in range(0, x_ref.shape[0], ROW_TILE):
        rows = slice(r0, r0 + ROW_TILE)
        x = x_ref[rows, :]
        xb = x.astype(BF16)
        hb = (halo_ref[...] if r0 == 0 else x_ref[r0 - HALO:r0, :]).astype(BF16)
        for c in range(0, d_ff, FF_CHUNK):
            cols = slice(c, c + FF_CHUNK)
            wg = w_ref[:, cols]
            gm = _dot(xb, wg)
            gh = _dot(hb, wg)
            if r0 == 0:
                gh = gh * keep
            um = _dot(xb, w_ref[:, d_ff + c:d_ff + c + FF_CHUNK])
            gext = jnp.concatenate([gh, gm], axis=0)
            g1 = pltpu.roll(gext, 1, 0)[HALO:]
            g2 = pltpu.roll(gext, 2, 0)[HALO:]
            conv = cb_ref[:, cols] + cw_ref[0:1, cols] * g2 + cw_ref[1:2, cols] * g1 + cw_ref[2:3, cols] * gm
            acc_ref[rows, cols] = (_gelu(conv) * um).astype(BF16)
        x2 = _layer_norm(alpha * x + _dot(acc_ref[rows, :], wd_ref[...]), g_ref[...], b_ref[...])
        gate = jax.nn.sigmoid(_dot(x2.astype(BF16), wpg_ref[...]))
        o_ref[rows, :] = x2 + gate * _dot(p_ref[rows, :].astype(BF16), wpp_ref[...])


def _ffn(x1, p2d, w_in, cw, cb, wd, ln_g, ln_b, wpg, wpp, alpha, seq):
    T, D = x1.shape
    tm = WIDE_TILE
    assert seq % tm == 0 and wd.shape[0] % FF_CHUNK == 0
    hb = tm // HALO
    const = lambda shape: pl.BlockSpec(shape, lambda i: (0, 0))
    return pl.pallas_call(
        functools.partial(_ffn_kernel, alpha=alpha, tiles_per_seq=seq // tm),
        grid=(T // tm,),
        in_specs=[pl.BlockSpec((tm, D), lambda i: (i, 0)),
                  pl.BlockSpec((HALO, D), lambda i: (jnp.maximum(i * hb - 1, 0), 0)),
                  pl.BlockSpec((tm, p2d.shape[1]), lambda i: (i, 0)),
                  const(w_in.shape), const(cw.shape), const(cb.shape), const(wd.shape),
                  const((1, D)), const((1, D)), const(wpg.shape), const(wpp.shape)],
        out_specs=pl.BlockSpec((tm, D), lambda i: (i, 0)),
        out_shape=jax.ShapeDtypeStruct((T, D), F32),
        scratch_shapes=[pltpu.VMEM((tm, wd.shape[0]), BF16)],
        compiler_params=_params("arbitrary"),
        name="ffn",
    )(x1, x1, p2d, w_in, cw, cb, wd, ln_g, ln_b, wpg, wpp)


def _slc_from_cmp_t(ncp, n_slc):
    ratio = SLC_BLOCK // CMP_STRIDE
    span = CMP_BLOCK // CMP_STRIDE
    mat = np.zeros((n_slc, ncp), np.float32)
    for j in range(n_slc):
        for m in range(ratio):
            for n in range(span):
                i = ratio * j + m - n
                if 0 <= i < ncp - 1:
                    mat[j, i] += 1.0
    return mat


def _layer(x, p_l, w_in, pe_k, w1_k, w2_k, pe_v, w1_v, w2_v, lq1, lk1, lq2, lk2, subln_g, w_bn, w_bd, w_out,
           ln1_g, ln1_b, w_ffn_in, conv_w, conv_b, w_down, ln2_g, ln2_b, w_pp, w_pg, tab, lambda_init,
           alpha):
    B, S, D = x.shape
    T = B * S
    ncp = S // CMP_STRIDE
    n_slc = S // SLC_BLOCK
    q_w = NSA_HEADS * HEAD_DIM
    kv_w = NSA_GROUPS * HEAD_DIM
    dqk_w = DIFF_HEADS * 2 * HEAD_DIM
    sizes = (q_w,) + (kv_w,) * 6 + (NSA_HEADS * 3, dqk_w, dqk_w, dqk_w, D, D)
    offs = np.concatenate([[0], np.cumsum(sizes)])
    col = lambda i: w_in[:, int(offs[i]):int(offs[i + 1])]
    scale = HEAD_DIM ** -0.5 * LOG2E

    n_idx = np.arange(q_w)
    perm = (NSA_REP * ((n_idx % LANES) // HEAD_DIM) + n_idx // LANES) * HEAD_DIM + n_idx % HEAD_DIM
    w_main = jnp.concatenate([col(0)[:, perm] * scale] + [col(i) for i in range(3, 7)]
                             + [col(8) * scale, col(9), col(10)], axis=1).astype(BF16)
    w_cmp = jnp.concatenate([col(1), col(2)], axis=1).astype(BF16)
    w_gate = jnp.pad(col(7), ((0, 0), (0, LANES - NSA_HEADS * 3))).astype(BF16)
    x2d = x.reshape(T, D)
    proj, k_cmp, v_cmp, gates = _proj(x2d, w_main, w_cmp, w_gate)
    proj3 = proj.reshape(B, S, proj.shape[1])
    gates3 = gates.reshape(B, S, LANES)
    c_kslc, c_vslc, c_kwin, c_vwin = (q_w // LANES + i for i in range(4))
    c_dq = q_w // LANES + 4
    c_dk = c_dq + DIFF_HEADS
    c_dv = c_dk + DIFF_HEADS

    cw = CMP_STRIDE * kv_w
    rep = lambda a: jnp.broadcast_to(a.reshape(2, 2, CMP_STRIDE, 1, HEAD_DIM, -1),
                                     (2, 2, CMP_STRIDE, NSA_GROUPS, HEAD_DIM, a.shape[-1]))
    pe = rep(jnp.stack([pe_k, pe_v])[..., None]).reshape(2, 2, 1, cw)
    w1 = rep(jnp.stack([w1_k, w1_v])).reshape(2, 2, cw, CMP_HIDDEN).astype(BF16)
    w2 = jnp.stack([w2_k, w2_v])
    w2p = jnp.stack([jnp.pad(w2, ((0, 0), (0, 0), (g * HEAD_DIM, LANES - (g + 1) * HEAD_DIM)))
                     for g in range(NSA_GROUPS)], axis=1).astype(BF16)
    cmp_kv = _compress(k_cmp.reshape(B, ncp, cw), v_cmp.reshape(B, ncp, cw), pe, w1, w2p)

    matt = jnp.asarray(_slc_from_cmp_t(ncp, n_slc), BF16)
    ocmp, sel = _cmp_attention(tab, proj3, cmp_kv, gates3, matt, n_slc)
    et_np = np.zeros((S, LANES), np.float32)
    et_np[np.arange(S), np.arange(S) // SLC_BLOCK] = 2.0 ** MASK_EXP
    y_nsa = _nsa_attention(tab, proj3, sel, jnp.asarray(et_np, BF16), gates3, ocmp,
                           (c_kslc, c_vslc, c_kwin, c_vwin))

    lam4 = jnp.pad(jnp.stack([lq1, lk1, lq2, lk2]), ((0, SUBLANES - 4), (0, 0)))
    y_diff = _diff_attention(tab, proj3, lam4, subln_g.reshape(1, LANES), (c_dq, c_dk, c_dv), lambda_init)

    x1 = _merge(x2d, y_nsa.reshape(T, q_w), y_diff.reshape(T, dqk_w),
                col(11).astype(BF16), col(12).astype(BF16), w_bn[perm].astype(BF16), w_bd.astype(BF16),
                w_out.astype(BF16), ln1_g.reshape(1, D), ln1_b.reshape(1, D), alpha)

    cw = jnp.pad(conv_w, ((0, SUBLANES - CONV_WIDTH), (0, 0)))
    out = _ffn(x1, p_l.reshape(T, p_l.shape[-1]), w_ffn_in.astype(BF16), cw, conv_b.reshape(1, -1),
               w_down.astype(BF16), ln2_g.reshape(1, D), ln2_b.reshape(1, D),
               w_pg.astype(BF16), w_pp.astype(BF16), alpha, S)
    return out.reshape(B, S, D)


def kernel(x, p, w_in, nsa_cmp_pe_k, nsa_cmp_w1_k, nsa_cmp_w2_k, nsa_cmp_pe_v, nsa_cmp_w1_v, nsa_cmp_w2_v, diff_lambda_q1, diff_lambda_k1, diff_lambda_q2, diff_lambda_k2, diff_subln_g, w_branch_nsa, w_branch_diff, w_out, ln1_g, ln1_b, w_ffn_in, ffn_conv_w, ffn_conv_b, w_ffn_down, ln2_g, ln2_b, w_ple_proj, w_ple_gate, rel_bias_table):
    depth = w_in.shape[0]
    alpha = (2.0 * depth) ** 0.25
    for l in range(depth):
        lambda_init = 0.8 - 0.6 * math.exp(-0.3 * l)
        x = _layer(x, p[l], w_in[l], nsa_cmp_pe_k[l], nsa_cmp_w1_k[l], nsa_cmp_w2_k[l], nsa_cmp_pe_v[l],
                   nsa_cmp_w1_v[l], nsa_cmp_w2_v[l], diff_lambda_q1[l], diff_lambda_k1[l], diff_lambda_q2[l],
                   diff_lambda_k2[l], diff_subln_g[l], w_branch_nsa[l], w_branch_diff[l], w_out[l], ln1_g[l],
                   ln1_b[l], w_ffn_in[l], ffn_conv_w[l], ffn_conv_b[l], w_ffn_down[l], ln2_g[l], ln2_b[l],
                   w_ple_proj[l], w_ple_gate[l], rel_bias_table, lambda_init, alpha)
    return x
```

```python
import functools
import math

import jax
import jax.numpy as jnp
import numpy as np
from jax import lax
from jax.experimental import pallas as pl
from jax.experimental.pallas import tpu as pltpu

F32 = jnp.float32
BF16 = jnp.bfloat16

NSA_HEADS = 8
NSA_GROUPS = 2
NSA_REP = NSA_HEADS // NSA_GROUPS
HEAD_DIM = 64
CMP_BLOCK = 32
CMP_STRIDE = 16
CMP_HIDDEN = 256
SLC_BLOCK = 64
SLC_TOPK = 16
SLC_LOCAL = 2
WINDOW = 512
DIFF_HEADS = 4
REL_BUCKETS = 32
REL_MAX_EXACT = 16
REL_MAX_DIST = 128
D_FF = 2816
CONV_WIDTH = 3
LN_EPS = 1e-5
NEG_INF = -1e30
BIG = 1e30
LOG2E = math.log2(math.e)
MASK_EXP = 100

LANES = 128
SUBLANES = 8
VMEM_LIMIT = 56 * 1024 * 1024

CMP_TQ = 256
NSA_TQ = 256
DIFF_TQ = 512
NSA_FAR_KEYS = 512
DIFF_FAR_KEYS = 2048
NSA_ONLINE_KEYS = 512
DIFF_ONLINE_KEYS = 2048
ROW_TILE = 512
WIDE_TILE = 1024
FF_CHUNK = 512
HALO = 16


def _rel_breakpoints():
    n = np.arange(0, 4 * REL_MAX_DIST)
    large = REL_MAX_EXACT + (np.log(np.maximum(n, 1).astype(np.float32) / REL_MAX_EXACT)
                             / np.float32(math.log(REL_MAX_DIST / REL_MAX_EXACT))
                             * (REL_BUCKETS - REL_MAX_EXACT)).astype(np.int32)
    bucket = np.where(n < REL_MAX_EXACT, n, np.minimum(large, REL_BUCKETS - 1))
    assert np.all(np.diff(bucket) >= 0)
    return [int(np.argmax(bucket >= b)) for b in range(1, REL_BUCKETS)]


REL_BREAKS = _rel_breakpoints()


def _dot(a, b):
    return jnp.dot(a, b, preferred_element_type=F32)


def _dot_nt(a, b):
    return lax.dot_general(a, b, (((1,), (1,)), ((), ())), preferred_element_type=F32)


def _rel_bias(dist, tab_ref, head, shift):
    val = jnp.full(dist.shape, (tab_ref[0, head] - shift) * LOG2E, F32)
    for b, brk in enumerate(REL_BREAKS, start=1):
        val = jnp.where(dist >= brk, (tab_ref[b, head] - shift) * LOG2E, val)
    return val


def _gelu(x):
    c = math.sqrt(2.0 / math.pi)
    half = 0.5 * x
    return half + half * jnp.tanh(x * (c + (c * 0.044715) * (x * x)))


def _layer_norm(z, g, b):
    mu = jnp.mean(z, axis=-1, keepdims=True)
    zc = z - mu
    var = jnp.mean(zc * zc, axis=-1, keepdims=True)
    return zc * lax.rsqrt(var + LN_EPS) * g + b


def _params(*sem):
    return pltpu.CompilerParams(dimension_semantics=sem, vmem_limit_bytes=VMEM_LIMIT)


def _proj_kernel(x_ref, wm_ref, wc_ref, wg_ref, om_ref, ok_ref, ov_ref, og_ref):
    xb = x_ref[...].astype(BF16)
    n = wm_ref.shape[1]
    for c in range(0, n, 2 * LANES):
        w = min(2 * LANES, n - c)
        om_ref[:, c:c + w] = _dot(xb, wm_ref[:, c:c + w]).astype(om_ref.dtype)
    kv = _dot(xb, wc_ref[...])
    ok_ref[...] = kv[:, :LANES].astype(ok_ref.dtype)
    ov_ref[...] = kv[:, LANES:].astype(ov_ref.dtype)
    og_ref[...] = _dot(xb, wg_ref[...])


def _proj(x2d, w_main, w_cmp, w_gate):
    T, D = x2d.shape
    n = w_main.shape[1]
    row = lambda width: pl.BlockSpec((ROW_TILE, width), lambda i: (i, 0))
    const = lambda width: pl.BlockSpec((D, width), lambda i: (0, 0))
    return pl.pallas_call(
        _proj_kernel,
        grid=(T // ROW_TILE,),
        in_specs=[row(D), const(n), const(2 * LANES), const(LANES)],
        out_specs=[row(n), row(LANES), row(LANES), row(LANES)],
        out_shape=[jax.ShapeDtypeStruct((T, n), BF16), jax.ShapeDtypeStruct((T, LANES), BF16),
                   jax.ShapeDtypeStruct((T, LANES), BF16), jax.ShapeDtypeStruct((T, LANES), F32)],
        compiler_params=_params("arbitrary"),
        name="proj",
    )(x2d, w_main, w_cmp, w_gate)


def _compress_kernel(xk_ref, xv_ref, pe_ref, w1_ref, w2_ref, o_ref):
    ncp, cw = xk_ref.shape[1], xk_ref.shape[2]
    lane = lax.broadcasted_iota(jnp.int32, (ncp, cw), 1)
    group = jnp.bitwise_and(jnp.right_shift(lane, int(math.log2(HEAD_DIM))), NSA_GROUPS - 1)
    for s, x_ref in enumerate((xk_ref, xv_ref)):
        x = x_ref[0].astype(F32)
        xa = x + pe_ref[s, 0]
        xb = x + pe_ref[s, 1]
        acc = jnp.zeros((ncp, LANES), F32)
        for g in range(NSA_GROUPS):
            a = _dot(jnp.where(group == g, xa, 0.0).astype(BF16), w1_ref[s, 0])
            b = _dot(jnp.where(group == g, xb, 0.0).astype(BF16), w1_ref[s, 1])
            h = a + pltpu.roll(b, ncp - 1, 0)
            acc = acc + _dot(_gelu(h).astype(BF16), w2_ref[s, g])
        o_ref[s, 0] = acc.astype(o_ref.dtype)


def _compress(xk, xv, pe, w1, w2):
    B, ncp, cw = xk.shape
    x_spec = pl.BlockSpec((1, ncp, cw), lambda b: (b, 0, 0))
    const = lambda shape: pl.BlockSpec(shape, lambda b: (0, 0, 0, 0))
    return pl.pallas_call(
        _compress_kernel,
        grid=(B,),
        in_specs=[x_spec, x_spec, const(pe.shape), const(w1.shape), const(w2.shape)],
        out_specs=pl.BlockSpec((2, 1, ncp, LANES), lambda b: (0, b, 0, 0)),
        out_shape=jax.ShapeDtypeStruct((2, B, ncp, LANES), BF16),
        compiler_params=_params("arbitrary"),
        name="compress",
    )(xk, xv, pe, w1, w2)


def _cmp_kernel(tab_ref, q_ref, kc_ref, vc_ref, gate_ref, matt_ref, ocmp_ref, sel_ref, bias_ref, ball_ref,
                *, n_slc):
    tq = q_ref.shape[1]
    ncp = kc_ref.shape[2]
    q0 = pl.program_id(0) * tq
    lane = lax.broadcasted_iota(jnp.int32, (tq, LANES), 1)

    @pl.when((pl.program_id(0) == 0) & (pl.program_id(1) == 0))
    def _():
        for head in range(NSA_HEADS):
            bias_ref[head] = _rel_bias(lane, tab_ref, head, 0.0)

    @pl.when(pl.program_id(1) == 0)
    def _():
        t_idx = q0 + lax.broadcasted_iota(jnp.int32, (tq, ncp), 0)
        c_idx = lax.broadcasted_iota(jnp.int32, (tq, ncp), 1)
        dist = t_idx - (c_idx * CMP_STRIDE + (CMP_BLOCK - 1))
        near = jnp.clip(dist, 0, LANES - 1)
        for head in range(NSA_HEADS):
            table = bias_ref[head]
            cols = [jnp.take_along_axis(table, near[:, c:c + LANES], axis=1) for c in range(0, ncp, LANES)]
            ball_ref[head * tq:(head + 1) * tq, :] = jnp.where(dist >= 0, jnp.concatenate(cols, axis=1), NEG_INF)

    gates = jax.nn.sigmoid(gate_ref[0])
    eye = (lax.broadcasted_iota(jnp.int32, (tq, tq), 0)
           == lax.broadcasted_iota(jnp.int32, (tq, tq), 1)).astype(BF16)
    kc = kc_ref[0, 0]
    vc = vc_ref[0, 0]
    jrow_i = lax.broadcasted_iota(jnp.int32, (n_slc, tq), 0)
    cur = jnp.right_shift(q0 + lax.broadcasted_iota(jnp.int32, (n_slc, tq), 1), int(math.log2(SLC_BLOCK)))
    forced = (jrow_i == 0) | ((cur - jrow_i >= 0) & (cur - jrow_i < SLC_LOCAL))
    blk_valid = jrow_i <= cur
    jrow = jrow_i.astype(F32)
    q_parts = []
    for g in range(NSA_GROUPS):
        lane_g = (lane >= HEAD_DIM * g) & (lane < HEAD_DIM * (g + 1))
        for r in range(NSA_REP):
            qb = q_ref[0, :, r * LANES:(r + 1) * LANES]
            q_parts.append(jnp.where(lane_g, qb, jnp.zeros_like(qb)))
    logit = _dot_nt(jnp.concatenate(q_parts, axis=0), kc) + ball_ref[...]
    m = jnp.max(logit, axis=-1, keepdims=True)
    e = jnp.exp2(logit - m)
    any_valid = (q0 + lax.broadcasted_iota(jnp.int32, (tq, 1), 0) >= CMP_BLOCK - 1).astype(F32)
    p_all = e * (jnp.concatenate([any_valid] * NSA_HEADS, axis=0) / jnp.sum(e, axis=-1, keepdims=True))
    o_all = _dot(p_all.astype(BF16), vc)
    gated = [[gates[:, h * 3:h * 3 + 1] * o_all[h * tq:(h + 1) * tq]
              for h in range(g * NSA_REP, (g + 1) * NSA_REP)] for g in range(NSA_GROUPS)]
    for g in range(NSA_GROUPS):
        psum = functools.reduce(lambda a, b: a + b,
                                [p_all[h * tq:(h + 1) * tq] for h in range(g * NSA_REP, (g + 1) * NSA_REP)])
        hi = psum.astype(BF16)
        lo = (psum - hi.astype(F32)).astype(BF16)
        p_slc = _dot_nt(matt_ref[...], hi) + _dot_nt(matt_ref[...], lo)
        score = jnp.where(forced, BIG, jnp.where(blk_valid, p_slc, NEG_INF))
        sel = jnp.zeros((n_slc, tq), F32)
        for _ in range(min(SLC_TOPK, n_slc)):
            mx = jnp.max(score, axis=0, keepdims=True)
            idx = jnp.min(jnp.where(score == mx, jrow, float(n_slc)), axis=0, keepdims=True)
            hit = jrow == idx
            sel = jnp.where(hit, 1.0, sel)
            score = jnp.where(hit, -3.0e38, score)
        selm1 = (sel - 1.0).astype(BF16)
        if n_slc < LANES:
            selm1 = jnp.concatenate([selm1, jnp.zeros((LANES - n_slc, tq), BF16)], axis=0)
        sel_ref[0, g] = _dot_nt(eye, selm1).astype(sel_ref.dtype)
    for r in range(NSA_REP):
        ocmp_ref[0, :, r * LANES:(r + 1) * LANES] = jnp.where(lane < HEAD_DIM, gated[0][r], gated[1][r])


def _cmp_attention(tab, proj3, cmp_kv, gates3, matt, n_slc):
    B, S, _ = proj3.shape
    ncp = cmp_kv.shape[2]
    tq = CMP_TQ
    return pl.pallas_call(
        functools.partial(_cmp_kernel, n_slc=n_slc),
        grid=(S // tq, B),
        in_specs=[pl.BlockSpec(memory_space=pltpu.SMEM),
                  pl.BlockSpec((1, tq, 4 * LANES), lambda i, b: (b, i, 0)),
                  pl.BlockSpec((1, 1, ncp, LANES), lambda i, b: (0, b, 0, 0)),
                  pl.BlockSpec((1, 1, ncp, LANES), lambda i, b: (1, b, 0, 0)),
                  pl.BlockSpec((1, tq, LANES), lambda i, b: (b, i, 0)),
                  pl.BlockSpec((n_slc, ncp), lambda i, b: (0, 0))],
        out_specs=[pl.BlockSpec((1, tq, 4 * LANES), lambda i, b: (b, i, 0)),
                   pl.BlockSpec((1, NSA_GROUPS, tq, LANES), lambda i, b: (b, 0, i, 0))],
        out_shape=[jax.ShapeDtypeStruct((B, S, 4 * LANES), F32),
                   jax.ShapeDtypeStruct((B, NSA_GROUPS, S, LANES), BF16)],
        scratch_shapes=[pltpu.VMEM((NSA_HEADS, tq, LANES), F32), pltpu.VMEM((NSA_HEADS * tq, ncp), F32)],
        compiler_params=_params("arbitrary", "arbitrary"),
        name="cmp",
    )(tab, proj3, cmp_kv, cmp_kv, gates3, matt)


class _OnlineSoftmax:
    def __init__(self, score, value, m_ref, l_ref, acc_ref, sub_keys):
        self.score, self.value, self.sub_keys = score, value, sub_keys
        self.m_ref, self.l_ref, self.acc_ref = m_ref, l_ref, acc_ref
        m_ref[...] = jnp.full(m_ref.shape, NEG_INF, F32)
        l_ref[...] = jnp.zeros(l_ref.shape, F32)
        acc_ref[...] = jnp.zeros(acc_ref.shape, F32)

    def step(self, k0, width, bias):
        m_ref, l_ref, acc_ref = self.m_ref, self.l_ref, self.acc_ref
        for j in range(0, width, self.sub_keys):
            w = min(self.sub_keys, width - j)
            s = self.score(k0 + j, w)
            cols = []
            for c in range(0, w, LANES):
                b = None if bias is None else bias(j + c)
                cols.append(s[:, c:c + LANES] if b is None else s[:, c:c + LANES] + b)
            m_old = m_ref[...]
            m_new = jnp.maximum(m_old, jnp.max(functools.reduce(jnp.maximum, cols), axis=-1, keepdims=True))
            alpha = jnp.exp2(m_old - m_new)
            ps = [jnp.exp2(col - m_new) for col in cols]
            l_ref[...] = alpha * l_ref[...] + functools.reduce(lambda a, b: a + b, ps)
            acc_ref[...] = alpha * acc_ref[...] + _dot(
                jnp.concatenate([p.astype(BF16) for p in ps], axis=1), self.value(k0 + j, w))
            m_ref[...] = m_new

    def result(self):
        return self.acc_ref[...] / jnp.sum(self.l_ref[...], axis=-1, keepdims=True)


def _causal_far_loop(fn, qt, tq, far_keys):
    far_tiles = far_keys // tq

    def far_body(i, carry):
        fn(i * far_keys, far_keys, None)
        return carry

    lax.fori_loop(0, lax.div(jnp.maximum(qt - 1, 0), far_tiles), far_body, 0)


def _causal_segments(fn, qt, tq, far_keys, near_bias):
    n_far = jnp.maximum(qt - 1, 0)
    far_tiles = far_keys // tq
    _causal_far_loop(fn, qt, tq, far_keys)
    for rem in range(far_tiles):
        lead = rem * tq
        bias = (lambda c, lead=lead: None if c < lead else near_bias(c - lead))
        pl.when((qt >= 1) & (lax.rem(n_far, far_tiles) == rem))(
            functools.partial(fn, (qt - 1 - rem) * tq, lead + 2 * tq, bias))
    pl.when(qt == 0)(functools.partial(fn, 0, tq, lambda c: near_bias(tq + c)))


def _key_slice(k0, width, tq):
    return pl.ds(k0 if isinstance(k0, int) else pl.multiple_of(k0, tq), width)


def _rows(ref, k0, width, tq):
    return ref[0, _key_slice(k0, width, tq), :]


def _nsa_kernel(tab_ref, q_ref, ks_ref, vs_ref, kw_ref, vw_ref, sel_ref, et_ref, gate_ref, ocmp_ref,
                y_ref, tb_ref, m_ref, l_ref, acc_ref, wm_ref, wl_ref, wacc_ref):
    tq = q_ref.shape[1]
    n_win = WINDOW // tq
    qt = pl.program_id(1)

    @pl.when((pl.program_id(0) == 0) & (qt == 0))
    def _():
        ti = lax.broadcasted_iota(jnp.int32, (tq, tq), 0)
        ki = lax.broadcasted_iota(jnp.int32, (tq, tq), 1)
        for g in range(NSA_GROUPS):
            for r in range(NSA_REP):
                head = g * NSA_REP + r
                far_bias = tab_ref[REL_BUCKETS - 1, head]
                rows = slice(head * tq, (head + 1) * tq)
                tb_ref[rows, 0:tq] = jnp.where(ti < ki, 0.0, NEG_INF)
                for j in range(2, n_win):
                    tb_ref[rows, (n_win - j) * tq:(n_win - j + 1) * tq] = jnp.zeros((tq, tq), F32)
                tb_ref[rows, (n_win - 1) * tq:n_win * tq] = _rel_bias(ti - ki + tq, tab_ref, head, far_bias)
                tb_ref[rows, n_win * tq:(n_win + 1) * tq] = jnp.where(
                    ti >= ki, _rel_bias(ti - ki, tab_ref, head, far_bias), NEG_INF)

    lane = lax.broadcasted_iota(jnp.int32, (tq, LANES), 1)
    gates = jax.nn.sigmoid(gate_ref[0])

    q_parts, sel_parts = [], []
    for g in range(NSA_GROUPS):
        lane_g = (lane >= HEAD_DIM * g) & (lane < HEAD_DIM * (g + 1))
        for r in range(NSA_REP):
            qb = q_ref[0, :, r * LANES:(r + 1) * LANES]
            q_parts.append(jnp.where(lane_g, qb, jnp.zeros_like(qb)))
            sel_parts.append(sel_ref[0, g])
    q_all = jnp.concatenate(q_parts, axis=0)
    qs_all = jnp.concatenate([q_all, jnp.concatenate(sel_parts, axis=0)], axis=1)

    def bias_from(col0):
        return lambda c: tb_ref[:, col0 + c:col0 + c + LANES]

    def slc_score(k0, width):
        et = et_ref[_key_slice(k0, width, tq), :]
        return _dot_nt(qs_all, jnp.concatenate([_rows(ks_ref, k0, width, tq), et], axis=1))

    def win_score(k0, width):
        return _dot_nt(q_all, _rows(kw_ref, k0, width, tq))

    slc = _OnlineSoftmax(slc_score, lambda k0, width: _rows(vs_ref, k0, width, tq),
                         m_ref, l_ref, acc_ref, NSA_ONLINE_KEYS)
    win = _OnlineSoftmax(win_score, lambda k0, width: _rows(vw_ref, k0, width, tq),
                         wm_ref, wl_ref, wacc_ref, NSA_ONLINE_KEYS)

    def near_and_window(slc_tiles, win_tiles):
        slc_lead = (slc_tiles - 2) * tq if slc_tiles >= 2 else 0
        slc_bias = bias_from((n_win - 1) * tq) if slc_tiles >= 2 else bias_from(n_win * tq)
        slc.step((qt + 1 - slc_tiles) * tq, slc_tiles * tq,
                 lambda c: None if c < slc_lead else slc_bias(c - slc_lead))
        win.step((qt + 1 - win_tiles) * tq, win_tiles * tq, bias_from((n_win + 1 - win_tiles) * tq))

    _causal_far_loop(slc.step, qt, tq, NSA_FAR_KEYS)
    far_tiles = NSA_FAR_KEYS // tq
    left_over = lax.rem(jnp.maximum(qt - 1, 0), far_tiles)
    for n in range(n_win):
        pl.when(qt == n)(functools.partial(near_and_window, n + 1, n + 1))
    for rem in range(far_tiles):
        pl.when((qt >= n_win) & (left_over == rem))(functools.partial(near_and_window, rem + 2, n_win + 1))
    o_slc = slc.result()
    o_win = win.result()

    for r in range(NSA_REP):
        ys = []
        for g in range(NSA_GROUPS):
            head = g * NSA_REP + r
            rows = slice(head * tq, (head + 1) * tq)
            ys.append(gates[:, head * 3 + 1:head * 3 + 2] * o_slc[rows]
                      + gates[:, head * 3 + 2:head * 3 + 3] * o_win[rows])
        cols = slice(r * LANES, (r + 1) * LANES)
        y_ref[0, :, cols] = (ocmp_ref[0, :, cols] + jnp.where(lane < HEAD_DIM, ys[0], ys[1])).astype(y_ref.dtype)


def _nsa_attention(tab, proj3, sel, et, gates3, ocmp, col_blocks):
    B, S, _ = proj3.shape
    tq = NSA_TQ
    assert WINDOW % tq == 0 and WINDOW // tq >= 2 and S % tq == 0
    n_win = WINDOW // tq
    ks_c, vs_c, kw_c, vw_c = col_blocks
    rows = NSA_HEADS * tq

    def kv_spec(c):
        return pl.BlockSpec((1, S, LANES), lambda b, i: (b, 0, c))

    return pl.pallas_call(
        _nsa_kernel,
        grid=(B, S // tq),
        in_specs=[pl.BlockSpec(memory_space=pltpu.SMEM),
                  pl.BlockSpec((1, tq, 4 * LANES), lambda b, i: (b, i, 0)),
                  kv_spec(ks_c), kv_spec(vs_c), kv_spec(kw_c), kv_spec(vw_c),
                  pl.BlockSpec((1, NSA_GROUPS, tq, LANES), lambda b, i: (b, 0, i, 0)),
                  pl.BlockSpec((S, LANES), lambda b, i: (0, 0)),
                  pl.BlockSpec((1, tq, LANES), lambda b, i: (b, i, 0)),
                  pl.BlockSpec((1, tq, 4 * LANES), lambda b, i: (b, i, 0))],
        out_specs=pl.BlockSpec((1, tq, 4 * LANES), lambda b, i: (b, i, 0)),
        out_shape=jax.ShapeDtypeStruct((B, S, 4 * LANES), BF16),
        scratch_shapes=[pltpu.VMEM((rows, (n_win + 1) * tq), F32)] + [pltpu.VMEM((rows, LANES), F32)] * 6,
        compiler_params=_params("arbitrary", "arbitrary"),
        name="nsa",
    )(tab, proj3, proj3, proj3, proj3, proj3, sel, et, gates3, ocmp)


def _diff_kernel(tab_ref, q_ref, k_ref, v_ref, lam_ref, g_ref, y_ref, tb_ref, m_ref, l_ref, acc_ref,
                 *, lambda_init):
    tq = q_ref.shape[1]
    h = pl.program_id(1)
    qt = pl.program_id(2)

    @pl.when(qt == 0)
    def _():
        ti = lax.broadcasted_iota(jnp.int32, (tq, tq), 0)
        ki = lax.broadcasted_iota(jnp.int32, (tq, tq), 1)
        head = NSA_HEADS + h
        far_bias = tab_ref[REL_BUCKETS - 1, head]
        tb_ref[:, 0:tq] = _rel_bias(ti - ki + tq, tab_ref, head, far_bias)
        tb_ref[:, tq:2 * tq] = jnp.where(ti >= ki, _rel_bias(ti - ki, tab_ref, head, far_bias), NEG_INF)

    lane = lax.broadcasted_iota(jnp.int32, (tq, LANES), 1)
    qb = q_ref[0]
    zero = jnp.zeros_like(qb)
    q2 = jnp.concatenate([jnp.where(lane < HEAD_DIM, qb, zero), jnp.where(lane >= HEAD_DIM, qb, zero)], axis=0)

    def score(k0, width):
        return _dot_nt(q2, _rows(k_ref, k0, width, tq))

    def near_bias(c):
        tb = tb_ref[:, c:c + LANES]
        return jnp.concatenate([tb, tb], axis=0)

    attn = _OnlineSoftmax(score, lambda k0, width: _rows(v_ref, k0, width, tq), m_ref, l_ref, acc_ref,
                          DIFF_ONLINE_KEYS)
    _causal_segments(attn.step, qt, tq, DIFF_FAR_KEYS, near_bias)
    a = attn.result()
    lq1, lk1, lq2, lk2 = lam_ref[0:1, :], lam_ref[1:2, :], lam_ref[2:3, :], lam_ref[3:4, :]
    lam = (jnp.exp(jnp.sum(lq1 * lk1, axis=-1, keepdims=True))
           - jnp.exp(jnp.sum(lq2 * lk2, axis=-1, keepdims=True)) + lambda_init)
    o = a[:tq] - lam * a[tq:]
    o = o * lax.rsqrt(jnp.mean(o * o, axis=-1, keepdims=True) + LN_EPS) * g_ref[...]
    y_ref[0] = (o * (1.0 - lambda_init)).astype(y_ref.dtype)


def _diff_attention(tab, proj3, lam4, subln_g, col_blocks, lambda_init):
    B, S, _ = proj3.shape
    tq = min(DIFF_TQ, S)
    q_c, k_c, v_c = col_blocks
    return pl.pallas_call(
        functools.partial(_diff_kernel, lambda_init=lambda_init),
        grid=(B, DIFF_HEADS, S // tq),
        in_specs=[pl.BlockSpec(memory_space=pltpu.SMEM),
                  pl.BlockSpec((1, tq, LANES), lambda b, h, i: (b, i, q_c + h)),
                  pl.BlockSpec((1, S, LANES), lambda b, h, i: (b, 0, k_c + h)),
                  pl.BlockSpec((1, S, LANES), lambda b, h, i: (b, 0, v_c + h)),
                  pl.BlockSpec((SUBLANES, HEAD_DIM), lambda b, h, i: (0, 0)),
                  pl.BlockSpec((1, LANES), lambda b, h, i: (0, 0))],
        out_specs=pl.BlockSpec((1, tq, LANES), lambda b, h, i: (b, i, h)),
        out_shape=jax.ShapeDtypeStruct((B, S, DIFF_HEADS * LANES), BF16),
        scratch_shapes=[pltpu.VMEM((tq, 2 * tq), F32),
                        pltpu.VMEM((2 * tq, LANES), F32),
                        pltpu.VMEM((2 * tq, LANES), F32),
                        pltpu.VMEM((2 * tq, LANES), F32)],
        compiler_params=_params("arbitrary", "arbitrary", "arbitrary"),
        name="diff",
    )(tab, proj3, proj3, proj3, lam4, subln_g)


def _merge_kernel(x_ref, yn_ref, yd_ref, wgn_ref, wgd_ref, wbn_ref, wbd_ref, wo_ref, g_ref, b_ref, o_ref,
                  *, alpha):
    for r0 in range(0, x_ref.shape[0], ROW_TILE):
        rows = slice(r0, r0 + ROW_TILE)
        x = x_ref[rows, :]
        xb = x.astype(BF16)
        merged = (jax.nn.sigmoid(_dot(xb, wgn_ref[...])) * _dot(yn_ref[rows, :], wbn_ref[...])
                  + jax.nn.sigmoid(_dot(xb, wgd_ref[...])) * _dot(yd_ref[rows, :], wbd_ref[...]))
        z = alpha * x + _dot(merged.astype(BF16), wo_ref[...])
        o_ref[rows, :] = _layer_norm(z, g_ref[...], b_ref[...])


def _merge(x2d, y_nsa, y_diff, wgn, wgd, wbn, wbd, wo, ln_g, ln_b, alpha):
    T, D = x2d.shape
    const = lambda shape: pl.BlockSpec(shape, lambda i: (0, 0))
    return pl.pallas_call(
        functools.partial(_merge_kernel, alpha=alpha),
        grid=(T // WIDE_TILE,),
        in_specs=[pl.BlockSpec((WIDE_TILE, D), lambda i: (i, 0)),
                  pl.BlockSpec((WIDE_TILE, y_nsa.shape[1]), lambda i: (i, 0)),
                  pl.BlockSpec((WIDE_TILE, y_diff.shape[1]), lambda i: (i, 0)),
                  const(wgn.shape), const(wgd.shape), const(wbn.shape), const(wbd.shape), const(wo.shape),
                  const((1, D)), const((1, D))],
        out_specs=pl.BlockSpec((WIDE_TILE, D), lambda i: (i, 0)),
        out_shape=jax.ShapeDtypeStruct((T, D), F32),
        compiler_params=_params("arbitrary"),
        name="merge",
    )(x2d, y_nsa, y_diff, wgn, wgd, wbn, wbd, wo, ln_g, ln_b)


def _ffn_kernel(x_ref, halo_ref, p_ref, w_ref, cw_ref, cb_ref, wd_ref, g_ref, b_ref, wpg_ref, wpp_ref,
                o_ref, acc_ref, *, alpha, tiles_per_seq):
    d_ff = wd_ref.shape[0]
    keep = (pl.program_id(0) % tiles_per_seq != 0).astype(F32)
    for r0 in range(0, x_ref.shape[0], ROW_TILE):
        rows = slice(r0, r0 + ROW_TILE)
        x = x_ref[rows, :]
        xb = x.astype(BF16)
        hb = (halo_ref[...] if r0 == 0 else x_ref[r0 - HALO:r0, :]).astype(BF16)
        for c in range(0, d_ff, FF_CHUNK):
            width = min(FF_CHUNK, d_ff - c)
            cols = slice(c, c + width)
            wg = w_ref[:, cols]
            gm = _dot(xb, wg)
            gh = _dot(hb, wg)
            if r0 == 0:
                gh = gh * keep
            um = _dot(xb, w_ref[:, d_ff + c:d_ff + c + width])
            gext = jnp.concatenate([gh, gm], axis=0)
            g1 = pltpu.roll(gext, 1, 0)[HALO:]
            g2 = pltpu.roll(gext, 2, 0)[HALO:]
            conv = cb_ref[:, cols] + cw_ref[0:1, cols] * g2 + cw_ref[1:2, cols] * g1 + cw_ref[2:3, cols] * gm
            acc_ref[rows, cols] = (_gelu(conv) * um).astype(BF16)
        x2 = _layer_norm(alpha * x + _dot(acc_ref[rows, :], wd_ref[...]), g_ref[...], b_ref[...])
        gate = jax.nn.sigmoid(_dot(x2.astype(BF16), wpg_ref[...]))
        o_ref[rows, :] = x2 + gate * _dot(p_ref[rows, :].astype(BF16), wpp_ref[...])


def _ffn(x1, p2d, w_in, cw, cb, wd, ln_g, ln_b, wpg, wpp, alpha, seq):
    T, D = x1.shape
    tm = WIDE_TILE
    assert seq % tm == 0 and wd.shape[0] % LANES == 0
    hb = tm // HALO
    const = lambda shape: pl.BlockSpec(shape, lambda i: (0, 0))
    return pl.pallas_call(
        functools.partial(_ffn_kernel, alpha=alpha, tiles_per_seq=seq // tm),
        grid=(T // tm,),
        in_specs=[pl.BlockSpec((tm, D), lambda i: (i, 0)),
                  pl.BlockSpec((HALO, D), lambda i: (jnp.maximum(i * hb - 1, 0), 0)),
                  pl.BlockSpec((tm, p2d.shape[1]), lambda i: (i, 0)),
                  const(w_in.shape), const(cw.shape), const(cb.shape), const(wd.shape),
                  const((1, D)), const((1, D)), const(wpg.shape), const(wpp.shape)],
        out_specs=pl.BlockSpec((tm, D), lambda i: (i, 0)),
        out_shape=jax.ShapeDtypeStruct((T, D), F32),
        scratch_shapes=[pltpu.VMEM((tm, wd.shape[0]), BF16)],
        compiler_params=_params("arbitrary"),
        name="ffn",
    )(x1, x1, p2d, w_in, cw, cb, wd, ln_g, ln_b, wpg, wpp)


def _slc_from_cmp_t(ncp, n_slc):
    ratio = SLC_BLOCK // CMP_STRIDE
    span = CMP_BLOCK // CMP_STRIDE
    mat = np.zeros((n_slc, ncp), np.float32)
    for j in range(n_slc):
        for m in range(ratio):
            for n in range(span):
                i = ratio * j + m - n
                if 0 <= i < ncp - 1:
                    mat[j, i] += 1.0
    return mat


def _layer(x, p_l, w_in, pe_k, w1_k, w2_k, pe_v, w1_v, w2_v, lq1, lk1, lq2, lk2, subln_g, w_bn, w_bd, w_out,
           ln1_g, ln1_b, w_ffn_in, conv_w, conv_b, w_down, ln2_g, ln2_b, w_pp, w_pg, tab, lambda_init,
           alpha):
    B, S, D = x.shape
    T = B * S
    ncp = S // CMP_STRIDE
    n_slc = S // SLC_BLOCK
    q_w = NSA_HEADS * HEAD_DIM
    kv_w = NSA_GROUPS * HEAD_DIM
    dqk_w = DIFF_HEADS * 2 * HEAD_DIM
    sizes = (q_w,) + (kv_w,) * 6 + (NSA_HEADS * 3, dqk_w, dqk_w, dqk_w, D, D)
    offs = np.concatenate([[0], np.cumsum(sizes)])
    col = lambda i: w_in[:, int(offs[i]):int(offs[i + 1])]
    scale = HEAD_DIM ** -0.5 * LOG2E

    n_idx = np.arange(q_w)
    perm = (NSA_REP * ((n_idx % LANES) // HEAD_DIM) + n_idx // LANES) * HEAD_DIM + n_idx % HEAD_DIM
    w_main = jnp.concatenate([col(0)[:, perm] * scale] + [col(i) for i in range(3, 7)]
                             + [col(8) * scale, col(9), col(10)], axis=1).astype(BF16)
    w_cmp = jnp.concatenate([col(1), col(2)], axis=1).astype(BF16)
    w_gate = jnp.pad(col(7), ((0, 0), (0, LANES - NSA_HEADS * 3))).astype(BF16)
    x2d = x.reshape(T, D)
    proj, k_cmp, v_cmp, gates = _proj(x2d, w_main, w_cmp, w_gate)
    proj3 = proj.reshape(B, S, proj.shape[1])
    gates3 = gates.reshape(B, S, LANES)
    c_kslc, c_vslc, c_kwin, c_vwin = (q_w // LANES + i for i in range(4))
    c_dq = q_w // LANES + 4
    c_dk = c_dq + DIFF_HEADS
    c_dv = c_dk + DIFF_HEADS

    cw = CMP_STRIDE * kv_w
    rep = lambda a: jnp.broadcast_to(a.reshape(2, 2, CMP_STRIDE, 1, HEAD_DIM, -1),
                                     (2, 2, CMP_STRIDE, NSA_GROUPS, HEAD_DIM, a.shape[-1]))
    pe = rep(jnp.stack([pe_k, pe_v])[..., None]).reshape(2, 2, 1, cw)
    w1 = rep(jnp.stack([w1_k, w1_v])).reshape(2, 2, cw, CMP_HIDDEN).astype(BF16)
    w2 = jnp.stack([w2_k, w2_v])
    w2p = jnp.stack([jnp.pad(w2, ((0, 0), (0, 0), (g * HEAD_DIM, LANES - (g + 1) * HEAD_DIM)))
                     for g in range(NSA_GROUPS)], axis=1).astype(BF16)
    cmp_kv = _compress(k_cmp.reshape(B, ncp, cw), v_cmp.reshape(B, ncp, cw), pe, w1, w2p)

    matt = jnp.asarray(_slc_from_cmp_t(ncp, n_slc), BF16)
    ocmp, sel = _cmp_attention(tab, proj3, cmp_kv, gates3, matt, n_slc)
    et_np = np.zeros((S, LANES), np.float32)
    et_np[np.arange(S), np.arange(S) // SLC_BLOCK] = 2.0 ** MASK_EXP
    y_nsa = _nsa_attention(tab, proj3, sel, jnp.asarray(et_np, BF16), gates3, ocmp,
                           (c_kslc, c_vslc, c_kwin, c_vwin))

    lam4 = jnp.pad(jnp.stack([lq1, lk1, lq2, lk2]), ((0, SUBLANES - 4), (0, 0)))
    y_diff = _diff_attention(tab, proj3, lam4, subln_g.reshape(1, LANES), (c_dq, c_dk, c_dv), lambda_init)

    x1 = _merge(x2d, y_nsa.reshape(T, q_w), y_diff.reshape(T, dqk_w),
                col(11).astype(BF16), col(12).astype(BF16), w_bn[perm].astype(BF16), w_bd.astype(BF16),
                w_out.astype(BF16), ln1_g.reshape(1, D), ln1_b.reshape(1, D), alpha)

    cw = jnp.pad(conv_w, ((0, SUBLANES - CONV_WIDTH), (0, 0)))
    out = _ffn(x1, p_l.reshape(T, p_l.shape[-1]), w_ffn_in.astype(BF16), cw, conv_b.reshape(1, -1),
               w_down.astype(BF16), ln2_g.reshape(1, D), ln2_b.reshape(1, D),
               w_pg.astype(BF16), w_pp.astype(BF16), alpha, S)
    return out.reshape(B, S, D)


def kernel(x, p, w_in, nsa_cmp_pe_k, nsa_cmp_w1_k, nsa_cmp_w2_k, nsa_cmp_pe_v, nsa_cmp_w1_v, nsa_cmp_w2_v, diff_lambda_q1, diff_lambda_k1, diff_lambda_q2, diff_lambda_k2, diff_subln_g, w_branch_nsa, w_branch_diff, w_out, ln1_g, ln1_b, w_ffn_in, ffn_conv_w, ffn_conv_b, w_ffn_down, ln2_g, ln2_b, w_ple_proj, w_ple_gate, rel_bias_table):
    depth = w_in.shape[0]
    alpha = (2.0 * depth) ** 0.25
    for l in range(depth):
        lambda_init = 0.8 - 0.6 * math.exp(-0.3 * l)
        x = _layer(x, p[l], w_in[l], nsa_cmp_pe_k[l], nsa_cmp_w1_k[l], nsa_cmp_w2_k[l], nsa_cmp_pe_v[l],
                   nsa_cmp_w1_v[l], nsa_cmp_w2_v[l], diff_lambda_q1[l], diff_lambda_k1[l], diff_lambda_q2[l],
                   diff_lambda_k2[l], diff_subln_g[l], w_branch_nsa[l], w_branch_diff[l], w_out[l], ln1_g[l],
                   ln1_b[l], w_ffn_in[l], ffn_conv_w[l], ffn_conv_b[l], w_ffn_down[l], ln2_g[l], ln2_b[l],
                   w_ple_proj[l], w_ple_gate[l], rel_bias_table, lambda_init, alpha)
    return x
```

```python
import functools
import math

import jax
import jax.numpy as jnp
import numpy as np
from jax import lax
from jax.experimental import pallas as pl
from jax.experimental.pallas import tpu as pltpu

F32 = jnp.float32
BF16 = jnp.bfloat16

NSA_HEADS = 8
NSA_GROUPS = 2
NSA_REP = NSA_HEADS // NSA_GROUPS
HEAD_DIM = 64
CMP_BLOCK = 32
CMP_STRIDE = 16
CMP_HIDDEN = 256
SLC_BLOCK = 64
SLC_TOPK = 16
SLC_LOCAL = 2
WINDOW = 512
DIFF_HEADS = 4
REL_BUCKETS = 32
REL_MAX_EXACT = 16
REL_MAX_DIST = 128
D_FF = 2816
CONV_WIDTH = 3
LN_EPS = 1e-5
NEG_INF = -1e30
BIG = 1e30
LOG2E = math.log2(math.e)
MASK_EXP = 100

LANES = 128
SUBLANES = 8
VMEM_LIMIT = 56 * 1024 * 1024

CMP_TQ = 256
NSA_TQ = 256
DIFF_TQ = 512
NSA_FAR_KEYS = 512
DIFF_FAR_KEYS = 2048
NSA_ONLINE_KEYS = 512
DIFF_ONLINE_KEYS = 2048
ROW_TILE = 512
WIDE_TILE = 1024
FF_CHUNK = 512
HALO = 16


def _rel_breakpoints():
    n = np.arange(0, 4 * REL_MAX_DIST)
    large = REL_MAX_EXACT + (np.log(np.maximum(n, 1).astype(np.float32) / REL_MAX_EXACT)
                             / np.float32(math.log(REL_MAX_DIST / REL_MAX_EXACT))
                             * (REL_BUCKETS - REL_MAX_EXACT)).astype(np.int32)
    bucket = np.where(n < REL_MAX_EXACT, n, np.minimum(large, REL_BUCKETS - 1))
    assert np.all(np.diff(bucket) >= 0)
    return [int(np.argmax(bucket >= b)) for b in range(1, REL_BUCKETS)]


REL_BREAKS = _rel_breakpoints()


def _dot(a, b):
    return jnp.dot(a, b, preferred_element_type=F32)


def _dot_nt(a, b):
    return lax.dot_general(a, b, (((1,), (1,)), ((), ())), preferred_element_type=F32)


def _rel_bias(dist, tab_ref, head, shift):
    val = jnp.full(dist.shape, (tab_ref[0, head] - shift) * LOG2E, F32)
    for b, brk in enumerate(REL_BREAKS, start=1):
        val = jnp.where(dist >= brk, (tab_ref[b, head] - shift) * LOG2E, val)
    return val


def _gelu(x):
    c = math.sqrt(2.0 / math.pi)
    half = 0.5 * x
    return half + half * jnp.tanh(x * (c + (c * 0.044715) * (x * x)))


def _layer_norm(z, g, b):
    mu = jnp.mean(z, axis=-1, keepdims=True)
    zc = z - mu
    var = jnp.mean(zc * zc, axis=-1, keepdims=True)
    return zc * lax.rsqrt(var + LN_EPS) * g + b


def _params(*sem):
    return pltpu.CompilerParams(dimension_semantics=sem, vmem_limit_bytes=VMEM_LIMIT)


def _proj_kernel(x_ref, wm_ref, wc_ref, wg_ref, om_ref, ok_ref, ov_ref, og_ref, k_scr, v_scr):
    xb = x_ref[...].astype(BF16)
    n = wm_ref.shape[1]
    for c in range(0, n, 2 * LANES):
        w = min(2 * LANES, n - c)
        om_ref[:, c:c + w] = _dot(xb, wm_ref[:, c:c + w]).astype(om_ref.dtype)
    kv = _dot(xb, wc_ref[...])
    k_scr[...] = kv[:, :LANES]
    v_scr[...] = kv[:, LANES:]
    chunks = k_scr.shape[0] // CMP_STRIDE
    for t in range(CMP_STRIDE):
        token_t = pl.ds(t, chunks, stride=CMP_STRIDE)
        ok_ref[:, t * LANES:(t + 1) * LANES] = k_scr[token_t, :].astype(ok_ref.dtype)
        ov_ref[:, t * LANES:(t + 1) * LANES] = v_scr[token_t, :].astype(ov_ref.dtype)
    og_ref[...] = _dot(xb, wg_ref[...])


def _proj(x2d, w_main, w_cmp, w_gate):
    T, D = x2d.shape
    n = w_main.shape[1]
    row = lambda width: pl.BlockSpec((ROW_TILE, width), lambda i: (i, 0))
    const = lambda width: pl.BlockSpec((D, width), lambda i: (0, 0))
    chunk_rows = pl.BlockSpec((ROW_TILE // CMP_STRIDE, CMP_STRIDE * LANES), lambda i: (i, 0))
    return pl.pallas_call(
        _proj_kernel,
        grid=(T // ROW_TILE,),
        in_specs=[row(D), const(n), const(2 * LANES), const(LANES)],
        out_specs=[row(n), chunk_rows, chunk_rows, row(LANES)],
        out_shape=[jax.ShapeDtypeStruct((T, n), BF16),
                   jax.ShapeDtypeStruct((T // CMP_STRIDE, CMP_STRIDE * LANES), BF16),
                   jax.ShapeDtypeStruct((T // CMP_STRIDE, CMP_STRIDE * LANES), BF16),
                   jax.ShapeDtypeStruct((T, LANES), F32)],
        scratch_shapes=[pltpu.VMEM((ROW_TILE, LANES), F32), pltpu.VMEM((ROW_TILE, LANES), F32)],
        compiler_params=_params("arbitrary"),
        name="proj",
    )(x2d, w_main, w_cmp, w_gate)


def _compress_kernel(xk_ref, xv_ref, pe_ref, w1_ref, w2_ref, o_ref):
    ncp, cw = xk_ref.shape[1], xk_ref.shape[2]
    lane = lax.broadcasted_iota(jnp.int32, (ncp, cw), 1)
    group = jnp.bitwise_and(jnp.right_shift(lane, int(math.log2(HEAD_DIM))), NSA_GROUPS - 1)
    for s, x_ref in enumerate((xk_ref, xv_ref)):
        x = x_ref[0].astype(F32)
        xa = x + pe_ref[s, 0]
        xb = x + pe_ref[s, 1]
        acc = jnp.zeros((ncp, LANES), F32)
        for g in range(NSA_GROUPS):
            a = _dot(jnp.where(group == g, xa, 0.0).astype(BF16), w1_ref[s, 0])
            b = _dot(jnp.where(group == g, xb, 0.0).astype(BF16), w1_ref[s, 1])
            h = a + pltpu.roll(b, ncp - 1, 0)
            acc = acc + _dot(_gelu(h).astype(BF16), w2_ref[s, g])
        o_ref[s, 0] = acc.astype(o_ref.dtype)


def _compress(xk, xv, pe, w1, w2):
    B, ncp, cw = xk.shape
    x_spec = pl.BlockSpec((1, ncp, cw), lambda b: (b, 0, 0))
    const = lambda shape: pl.BlockSpec(shape, lambda b: (0, 0, 0, 0))
    return pl.pallas_call(
        _compress_kernel,
        grid=(B,),
        in_specs=[x_spec, x_spec, const(pe.shape), const(w1.shape), const(w2.shape)],
        out_specs=pl.BlockSpec((2, 1, ncp, LANES), lambda b: (0, b, 0, 0)),
        out_shape=jax.ShapeDtypeStruct((2, B, ncp, LANES), BF16),
        compiler_params=_params("arbitrary"),
        name="compress",
    )(xk, xv, pe, w1, w2)


def _cmp_kernel(tab_ref, q_ref, kc_ref, vc_ref, gate_ref, matt_ref, ocmp_ref, sel_ref, bias_ref, ball_ref,
                *, n_slc):
    tq = q_ref.shape[1]
    ncp = kc_ref.shape[2]
    q0 = pl.program_id(0) * tq
    lane = lax.broadcasted_iota(jnp.int32, (tq, LANES), 1)

    @pl.when((pl.program_id(0) == 0) & (pl.program_id(1) == 0))
    def _():
        for head in range(NSA_HEADS):
            bias_ref[head] = _rel_bias(lane, tab_ref, head, 0.0)

    @pl.when(pl.program_id(1) == 0)
    def _():
        t_idx = q0 + lax.broadcasted_iota(jnp.int32, (tq, ncp), 0)
        c_idx = lax.broadcasted_iota(jnp.int32, (tq, ncp), 1)
        dist = t_idx - (c_idx * CMP_STRIDE + (CMP_BLOCK - 1))
        near = jnp.clip(dist, 0, LANES - 1)
        for head in range(NSA_HEADS):
            table = bias_ref[head]
            cols = [jnp.take_along_axis(table, near[:, c:c + LANES], axis=1) for c in range(0, ncp, LANES)]
            ball_ref[head * tq:(head + 1) * tq, :] = jnp.where(dist >= 0, jnp.concatenate(cols, axis=1), NEG_INF)

    gates = jax.nn.sigmoid(gate_ref[0])
    eye = (lax.broadcasted_iota(jnp.int32, (tq, tq), 0)
           == lax.broadcasted_iota(jnp.int32, (tq, tq), 1)).astype(BF16)
    kc = kc_ref[0, 0]
    vc = vc_ref[0, 0]
    jrow_i = lax.broadcasted_iota(jnp.int32, (n_slc, tq), 0)
    cur = jnp.right_shift(q0 + lax.broadcasted_iota(jnp.int32, (n_slc, tq), 1), int(math.log2(SLC_BLOCK)))
    forced = (jrow_i == 0) | ((cur - jrow_i >= 0) & (cur - jrow_i < SLC_LOCAL))
    blk_valid = jrow_i <= cur
    jrow = jrow_i.astype(F32)
    q_parts = []
    for g in range(NSA_GROUPS):
        lane_g = (lane >= HEAD_DIM * g) & (lane < HEAD_DIM * (g + 1))
        for r in range(NSA_REP):
            qb = q_ref[0, :, r * LANES:(r + 1) * LANES]
            q_parts.append(jnp.where(lane_g, qb, jnp.zeros_like(qb)))
    logit = _dot_nt(jnp.concatenate(q_parts, axis=0), kc) + ball_ref[...]
    m = jnp.max(logit, axis=-1, keepdims=True)
    e = jnp.exp2(logit - m)
    any_valid = (q0 + lax.broadcasted_iota(jnp.int32, (tq, 1), 0) >= CMP_BLOCK - 1).astype(F32)
    p_all = e * (jnp.concatenate([any_valid] * NSA_HEADS, axis=0) / jnp.sum(e, axis=-1, keepdims=True))
    o_all = _dot(p_all.astype(BF16), vc)
    gated = [[gates[:, h * 3:h * 3 + 1] * o_all[h * tq:(h + 1) * tq]
              for h in range(g * NSA_REP, (g + 1) * NSA_REP)] for g in range(NSA_GROUPS)]
    for g in range(NSA_GROUPS):
        psum = functools.reduce(lambda a, b: a + b,
                                [p_all[h * tq:(h + 1) * tq] for h in range(g * NSA_REP, (g + 1) * NSA_REP)])
        hi = psum.astype(BF16)
        lo = (psum - hi.astype(F32)).astype(BF16)
        p_slc = _dot_nt(matt_ref[...], hi) + _dot_nt(matt_ref[...], lo)
        score = jnp.where(forced, BIG, jnp.where(blk_valid, p_slc, NEG_INF))
        sel = jnp.zeros((n_slc, tq), F32)
        for _ in range(min(SLC_TOPK, n_slc)):
            mx = jnp.max(score, axis=0, keepdims=True)
            idx = jnp.min(jnp.where(score == mx, jrow, float(n_slc)), axis=0, keepdims=True)
            hit = jrow == idx
            sel = jnp.where(hit, 1.0, sel)
            score = jnp.where(hit, -3.0e38, score)
        selm1 = (sel - 1.0).astype(BF16)
        if n_slc < LANES:
            selm1 = jnp.concatenate([selm1, jnp.zeros((LANES - n_slc, tq), BF16)], axis=0)
        sel_ref[0, g] = _dot_nt(eye, selm1).astype(sel_ref.dtype)
    for r in range(NSA_REP):
        ocmp_ref[0, :, r * LANES:(r + 1) * LANES] = jnp.where(lane < HEAD_DIM, gated[0][r], gated[1][r])


def _cmp_attention(tab, proj3, cmp_kv, gates3, matt, n_slc):
    B, S, _ = proj3.shape
    ncp = cmp_kv.shape[2]
    tq = CMP_TQ
    return pl.pallas_call(
        functools.partial(_cmp_kernel, n_slc=n_slc),
        grid=(S // tq, B),
        in_specs=[pl.BlockSpec(memory_space=pltpu.SMEM),
                  pl.BlockSpec((1, tq, 4 * LANES), lambda i, b: (b, i, 0)),
                  pl.BlockSpec((1, 1, ncp, LANES), lambda i, b: (0, b, 0, 0)),
                  pl.BlockSpec((1, 1, ncp, LANES), lambda i, b: (1, b, 0, 0)),
                  pl.BlockSpec((1, tq, LANES), lambda i, b: (b, i, 0)),
                  pl.BlockSpec((n_slc, ncp), lambda i, b: (0, 0))],
        out_specs=[pl.BlockSpec((1, tq, 4 * LANES), lambda i, b: (b, i, 0)),
                   pl.BlockSpec((1, NSA_GROUPS, tq, LANES), lambda i, b: (b, 0, i, 0))],
        out_shape=[jax.ShapeDtypeStruct((B, S, 4 * LANES), F32),
                   jax.ShapeDtypeStruct((B, NSA_GROUPS, S, LANES), BF16)],
        scratch_shapes=[pltpu.VMEM((NSA_HEADS, tq, LANES), F32), pltpu.VMEM((NSA_HEADS * tq, ncp), F32)],
        compiler_params=_params("arbitrary", "arbitrary"),
        name="cmp",
    )(tab, proj3, cmp_kv, cmp_kv, gates3, matt)


class _OnlineSoftmax:
    def __init__(self, score, value, m_ref, l_ref, acc_ref, sub_keys):
        self.score, self.value, self.sub_keys = score, value, sub_keys
        self.m_ref, self.l_ref, self.acc_ref = m_ref, l_ref, acc_ref
        m_ref[...] = jnp.full(m_ref.shape, NEG_INF, F32)
        l_ref[...] = jnp.zeros(l_ref.shape, F32)
        acc_ref[...] = jnp.zeros(acc_ref.shape, F32)

    def step(self, k0, width, bias):
        m_ref, l_ref, acc_ref = self.m_ref, self.l_ref, self.acc_ref
        for j in range(0, width, self.sub_keys):
            w = min(self.sub_keys, width - j)
            s = self.score(k0 + j, w)
            cols = []
            for c in range(0, w, LANES):
                b = None if bias is None else bias(j + c)
                cols.append(s[:, c:c + LANES] if b is None else s[:, c:c + LANES] + b)
            m_old = m_ref[...]
            m_new = jnp.maximum(m_old, jnp.max(functools.reduce(jnp.maximum, cols), axis=-1, keepdims=True))
            alpha = jnp.exp2(m_old - m_new)
            ps = [jnp.exp2(col - m_new) for col in cols]
            l_ref[...] = alpha * l_ref[...] + functools.reduce(lambda a, b: a + b, ps)
            acc_ref[...] = alpha * acc_ref[...] + _dot(
                jnp.concatenate([p.astype(BF16) for p in ps], axis=1), self.value(k0 + j, w))
            m_ref[...] = m_new

    def result(self):
        return self.acc_ref[...] / jnp.sum(self.l_ref[...], axis=-1, keepdims=True)


def _causal_far_loop(fn, qt, tq, far_keys):
    far_tiles = far_keys // tq

    def far_body(i, carry):
        fn(i * far_keys, far_keys, None)
        return carry

    lax.fori_loop(0, lax.div(jnp.maximum(qt - 1, 0), far_tiles), far_body, 0)


def _causal_segments(fn, qt, tq, far_keys, near_bias):
    n_far = jnp.maximum(qt - 1, 0)
    far_tiles = far_keys // tq
    _causal_far_loop(fn, qt, tq, far_keys)
    for rem in range(far_tiles):
        lead = rem * tq
        bias = (lambda c, lead=lead: None if c < lead else near_bias(c - lead))
        pl.when((qt >= 1) & (lax.rem(n_far, far_tiles) == rem))(
            functools.partial(fn, (qt - 1 - rem) * tq, lead + 2 * tq, bias))
    pl.when(qt == 0)(functools.partial(fn, 0, tq, lambda c: near_bias(tq + c)))


def _key_slice(k0, width, tq):
    return pl.ds(k0 if isinstance(k0, int) else pl.multiple_of(k0, tq), width)


def _rows(ref, k0, width, tq):
    return ref[0, _key_slice(k0, width, tq), :]


def _nsa_kernel(tab_ref, q_ref, ks_ref, vs_ref, kw_ref, vw_ref, sel_ref, et_ref, gate_ref, ocmp_ref,
                y_ref, tb_ref, m_ref, l_ref, acc_ref, wm_ref, wl_ref, wacc_ref):
    tq = q_ref.shape[1]
    n_win = WINDOW // tq
    qt = pl.program_id(1)

    @pl.when((pl.program_id(0) == 0) & (qt == 0))
    def _():
        ti = lax.broadcasted_iota(jnp.int32, (tq, tq), 0)
        ki = lax.broadcasted_iota(jnp.int32, (tq, tq), 1)
        for g in range(NSA_GROUPS):
            for r in range(NSA_REP):
                head = g * NSA_REP + r
                far_bias = tab_ref[REL_BUCKETS - 1, head]
                rows = slice(head * tq, (head + 1) * tq)
                tb_ref[rows, 0:tq] = jnp.where(ti < ki, 0.0, NEG_INF)
                for j in range(2, n_win):
                    tb_ref[rows, (n_win - j) * tq:(n_win - j + 1) * tq] = jnp.zeros((tq, tq), F32)
                tb_ref[rows, (n_win - 1) * tq:n_win * tq] = _rel_bias(ti - ki + tq, tab_ref, head, far_bias)
                tb_ref[rows, n_win * tq:(n_win + 1) * tq] = jnp.where(
                    ti >= ki, _rel_bias(ti - ki, tab_ref, head, far_bias), NEG_INF)

    lane = lax.broadcasted_iota(jnp.int32, (tq, LANES), 1)
    gates = jax.nn.sigmoid(gate_ref[0])

    q_parts, sel_parts = [], []
    for g in range(NSA_GROUPS):
        lane_g = (lane >= HEAD_DIM * g) & (lane < HEAD_DIM * (g + 1))
        for r in range(NSA_REP):
            qb = q_ref[0, :, r * LANES:(r + 1) * LANES]
            q_parts.append(jnp.where(lane_g, qb, jnp.zeros_like(qb)))
            sel_parts.append(sel_ref[0, g])
    q_all = jnp.concatenate(q_parts, axis=0)
    qs_all = jnp.concatenate([q_all, jnp.concatenate(sel_parts, axis=0)], axis=1)

    def bias_from(col0):
        return lambda c: tb_ref[:, col0 + c:col0 + c + LANES]

    def slc_score(k0, width):
        et = et_ref[_key_slice(k0, width, tq), :]
        return _dot_nt(qs_all, jnp.concatenate([_rows(ks_ref, k0, width, tq), et], axis=1))

    def win_score(k0, width):
        return _dot_nt(q_all, _rows(kw_ref, k0, width, tq))

    slc = _OnlineSoftmax(slc_score, lambda k0, width: _rows(vs_ref, k0, width, tq),
                         m_ref, l_ref, acc_ref, NSA_ONLINE_KEYS)
    win = _OnlineSoftmax(win_score, lambda k0, width: _rows(vw_ref, k0, width, tq),
                         wm_ref, wl_ref, wacc_ref, NSA_ONLINE_KEYS)

    def near_and_window(slc_tiles, win_tiles):
        slc_lead = (slc_tiles - 2) * tq if slc_tiles >= 2 else 0
        slc_bias = bias_from((n_win - 1) * tq) if slc_tiles >= 2 else bias_from(n_win * tq)
        slc.step((qt + 1 - slc_tiles) * tq, slc_tiles * tq,
                 lambda c: None if c < slc_lead else slc_bias(c - slc_lead))
        win.step((qt + 1 - win_tiles) * tq, win_tiles * tq, bias_from((n_win + 1 - win_tiles) * tq))

    _causal_far_loop(slc.step, qt, tq, NSA_FAR_KEYS)
    far_tiles = NSA_FAR_KEYS // tq
    left_over = lax.rem(jnp.maximum(qt - 1, 0), far_tiles)
    for n in range(n_win):
        pl.when(qt == n)(functools.partial(near_and_window, n + 1, n + 1))
    for rem in range(far_tiles):
        pl.when((qt >= n_win) & (left_over == rem))(functools.partial(near_and_window, rem + 2, n_win + 1))
    o_slc = slc.result()
    o_win = win.result()

    for r in range(NSA_REP):
        ys = []
        for g in range(NSA_GROUPS):
            head = g * NSA_REP + r
            rows = slice(head * tq, (head + 1) * tq)
            ys.append(gates[:, head * 3 + 1:head * 3 + 2] * o_slc[rows]
                      + gates[:, head * 3 + 2:head * 3 + 3] * o_win[rows])
        cols = slice(r * LANES, (r + 1) * LANES)
        y_ref[0, :, cols] = (ocmp_ref[0, :, cols] + jnp.where(lane < HEAD_DIM, ys[0], ys[1])).astype(y_ref.dtype)


def _nsa_attention(tab, proj3, sel, et, gates3, ocmp, col_blocks):
    B, S, _ = proj3.shape
    tq = NSA_TQ
    assert WINDOW % tq == 0 and WINDOW // tq >= 2 and S % tq == 0
    n_win = WINDOW // tq
    ks_c, vs_c, kw_c, vw_c = col_blocks
    rows = NSA_HEADS * tq

    def kv_spec(c):
        return pl.BlockSpec((1, S, LANES), lambda b, i: (b, 0, c))

    return pl.pallas_call(
        _nsa_kernel,
        grid=(B, S // tq),
        in_specs=[pl.BlockSpec(memory_space=pltpu.SMEM),
                  pl.BlockSpec((1, tq, 4 * LANES), lambda b, i: (b, i, 0)),
                  kv_spec(ks_c), kv_spec(vs_c), kv_spec(kw_c), kv_spec(vw_c),
                  pl.BlockSpec((1, NSA_GROUPS, tq, LANES), lambda b, i: (b, 0, i, 0)),
                  pl.BlockSpec((S, LANES), lambda b, i: (0, 0)),
                  pl.BlockSpec((1, tq, LANES), lambda b, i: (b, i, 0)),
                  pl.BlockSpec((1, tq, 4 * LANES), lambda b, i: (b, i, 0))],
        out_specs=pl.BlockSpec((1, tq, 4 * LANES), lambda b, i: (b, i, 0)),
        out_shape=jax.ShapeDtypeStruct((B, S, 4 * LANES), BF16),
        scratch_shapes=[pltpu.VMEM((rows, (n_win + 1) * tq), F32)] + [pltpu.VMEM((rows, LANES), F32)] * 6,
        compiler_params=_params("arbitrary", "arbitrary"),
        name="nsa",
    )(tab, proj3, proj3, proj3, proj3, proj3, sel, et, gates3, ocmp)


def _diff_kernel(tab_ref, q_ref, k_ref, v_ref, lam_ref, g_ref, y_ref, tb_ref, m_ref, l_ref, acc_ref,
                 *, lambda_init):
    tq = q_ref.shape[1]
    h = pl.program_id(1)
    qt = pl.program_id(2)

    @pl.when(qt == 0)
    def _():
        ti = lax.broadcasted_iota(jnp.int32, (tq, tq), 0)
        ki = lax.broadcasted_iota(jnp.int32, (tq, tq), 1)
        head = NSA_HEADS + h
        far_bias = tab_ref[REL_BUCKETS - 1, head]
        tb_ref[:, 0:tq] = _rel_bias(ti - ki + tq, tab_ref, head, far_bias)
        tb_ref[:, tq:2 * tq] = jnp.where(ti >= ki, _rel_bias(ti - ki, tab_ref, head, far_bias), NEG_INF)

    lane = lax.broadcasted_iota(jnp.int32, (tq, LANES), 1)
    qb = q_ref[0]
    zero = jnp.zeros_like(qb)
    q2 = jnp.concatenate([jnp.where(lane < HEAD_DIM, qb, zero), jnp.where(lane >= HEAD_DIM, qb, zero)], axis=0)

    def score(k0, width):
        return _dot_nt(q2, _rows(k_ref, k0, width, tq))

    def near_bias(c):
        tb = tb_ref[:, c:c + LANES]
        return jnp.concatenate([tb, tb], axis=0)

    attn = _OnlineSoftmax(score, lambda k0, width: _rows(v_ref, k0, width, tq), m_ref, l_ref, acc_ref,
                          DIFF_ONLINE_KEYS)
    _causal_segments(attn.step, qt, tq, DIFF_FAR_KEYS, near_bias)
    a = attn.result()
    lq1, lk1, lq2, lk2 = lam_ref[0:1, :], lam_ref[1:2, :], lam_ref[2:3, :], lam_ref[3:4, :]
    lam = (jnp.exp(jnp.sum(lq1 * lk1, axis=-1, keepdims=True))
           - jnp.exp(jnp.sum(lq2 * lk2, axis=-1, keepdims=True)) + lambda_init)
    o = a[:tq] - lam * a[tq:]
    o = o * lax.rsqrt(jnp.mean(o * o, axis=-1, keepdims=True) + LN_EPS) * g_ref[...]
    y_ref[0] = (o * (1.0 - lambda_init)).astype(y_ref.dtype)


def _diff_attention(tab, proj3, lam4, subln_g, col_blocks, lambda_init):
    B, S, _ = proj3.shape
    tq = min(DIFF_TQ, S)
    q_c, k_c, v_c = col_blocks
    return pl.pallas_call(
        functools.partial(_diff_kernel, lambda_init=lambda_init),
        grid=(B, DIFF_HEADS, S // tq),
        in_specs=[pl.BlockSpec(memory_space=pltpu.SMEM),
                  pl.BlockSpec((1, tq, LANES), lambda b, h, i: (b, i, q_c + h)),
                  pl.BlockSpec((1, S, LANES), lambda b, h, i: (b, 0, k_c + h)),
                  pl.BlockSpec((1, S, LANES), lambda b, h, i: (b, 0, v_c + h)),
                  pl.BlockSpec((SUBLANES, HEAD_DIM), lambda b, h, i: (0, 0)),
                  pl.BlockSpec((1, LANES), lambda b, h, i: (0, 0))],
        out_specs=pl.BlockSpec((1, tq, LANES), lambda b, h, i: (b, i, h)),
        out_shape=jax.ShapeDtypeStruct((B, S, DIFF_HEADS * LANES), BF16),
        scratch_shapes=[pltpu.VMEM((tq, 2 * tq), F32),
                        pltpu.VMEM((2 * tq, LANES), F32),
                        pltpu.VMEM((2 * tq, LANES), F32),
                        pltpu.VMEM((2 * tq, LANES), F32)],
        compiler_params=_params("arbitrary", "arbitrary", "arbitrary"),
        name="diff",
    )(tab, proj3, proj3, proj3, lam4, subln_g)


def _merge_kernel(x_ref, yn_ref, yd_ref, wgn_ref, wgd_ref, wbn_ref, wbd_ref, wo_ref, g_ref, b_ref, o_ref,
                  *, alpha):
    for r0 in range(0, x_ref.shape[0], ROW_TILE):
        rows = slice(r0, r0 + ROW_TILE)
        x = x_ref[rows, :]
        xb = x.astype(BF16)
        merged = (jax.nn.sigmoid(_dot(xb, wgn_ref[...])) * _dot(yn_ref[rows, :], wbn_ref[...])
                  + jax.nn.sigmoid(_dot(xb, wgd_ref[...])) * _dot(yd_ref[rows, :], wbd_ref[...]))
        z = alpha * x + _dot(merged.astype(BF16), wo_ref[...])
        o_ref[rows, :] = _layer_norm(z, g_ref[...], b_ref[...])


def _merge(x2d, y_nsa, y_diff, wgn, wgd, wbn, wbd, wo, ln_g, ln_b, alpha):
    T, D = x2d.shape
    const = lambda shape: pl.BlockSpec(shape, lambda i: (0, 0))
    return pl.pallas_call(
        functools.partial(_merge_kernel, alpha=alpha),
        grid=(T // WIDE_TILE,),
        in_specs=[pl.BlockSpec((WIDE_TILE, D), lambda i: (i, 0)),
                  pl.BlockSpec((WIDE_TILE, y_nsa.shape[1]), lambda i: (i, 0)),
                  pl.BlockSpec((WIDE_TILE, y_diff.shape[1]), lambda i: (i, 0)),
                  const(wgn.shape), const(wgd.shape), const(wbn.shape), const(wbd.shape), const(wo.shape),
                  const((1, D)), const((1, D))],
        out_specs=pl.BlockSpec((WIDE_TILE, D), lambda i: (i, 0)),
        out_shape=jax.ShapeDtypeStruct((T, D), F32),
        compiler_params=_params("arbitrary"),
        name="merge",
    )(x2d, y_nsa, y_diff, wgn, wgd, wbn, wbd, wo, ln_g, ln_b)


def _ffn_kernel(x_ref, halo_ref, p_ref, w_ref, cw_ref, cb_ref, wd_ref, g_ref, b_ref, wpg_ref, wpp_ref,
                o_ref, acc_ref, *, alpha, tiles_per_seq):
    d_ff = wd_ref.shape[0]
    keep = (pl.program_id(0) % tiles_per_seq != 0).astype(F32)
    for r0 in range(0, x_ref.shape[0], ROW_TILE):
        rows = slice(r0, r0 + ROW_TILE)
        x = x_ref[rows, :]
        xb = x.astype(BF16)
        hb = (halo_ref[...] if r0 == 0 else x_ref[r0 - HALO:r0, :]).astype(BF16)
        for c in range(0, d_ff, FF_CHUNK):
            width = min(FF_CHUNK, d_ff - c)
            cols = slice(c, c + width)
            wg = w_ref[:, cols]
            gm = _dot(xb, wg)
            gh = _dot(hb, wg)
            if r0 == 0:
                gh = gh * keep
            um = _dot(xb, w_ref[:, d_ff + c:d_ff + c + width])
            gext = jnp.concatenate([gh, gm], axis=0)
            g1 = pltpu.roll(gext, 1, 0)[HALO:]
            g2 = pltpu.roll(gext, 2, 0)[HALO:]
            conv = cb_ref[:, cols] + cw_ref[0:1, cols] * g2 + cw_ref[1:2, cols] * g1 + cw_ref[2:3, cols] * gm
            acc_ref[rows, cols] = (_gelu(conv) * um).astype(BF16)
        x2 = _layer_norm(alpha * x + _dot(acc_ref[rows, :], wd_ref[...]), g_ref[...], b_ref[...])
        gate = jax.nn.sigmoid(_dot(x2.astype(BF16), wpg_ref[...]))
        o_ref[rows, :] = x2 + gate * _dot(p_ref[rows, :].astype(BF16), wpp_ref[...])


def _ffn(x1, p2d, w_in, cw, cb, wd, ln_g, ln_b, wpg, wpp, alpha, seq):
    T, D = x1.shape
    tm = WIDE_TILE
    assert seq % tm == 0 and wd.shape[0] % LANES == 0
    hb = tm // HALO
    const = lambda shape: pl.BlockSpec(shape, lambda i: (0, 0))
    return pl.pallas_call(
        functools.partial(_ffn_kernel, alpha=alpha, tiles_per_seq=seq // tm),
        grid=(T // tm,),
        in_specs=[pl.BlockSpec((tm, D), lambda i: (i, 0)),
                  pl.BlockSpec((HALO, D), lambda i: (jnp.maximum(i * hb - 1, 0), 0)),
                  pl.BlockSpec((tm, p2d.shape[1]), lambda i: (i, 0)),
                  const(w_in.shape), const(cw.shape), const(cb.shape), const(wd.shape),
                  const((1, D)), const((1, D)), const(wpg.shape), const(wpp.shape)],
        out_specs=pl.BlockSpec((tm, D), lambda i: (i, 0)),
        out_shape=jax.ShapeDtypeStruct((T, D), F32),
        scratch_shapes=[pltpu.VMEM((tm, wd.shape[0]), BF16)],
        compiler_params=_params("arbitrary"),
        name="ffn",
    )(x1, x1, p2d, w_in, cw, cb, wd, ln_g, ln_b, wpg, wpp)


def _slc_from_cmp_t(ncp, n_slc):
    ratio = SLC_BLOCK // CMP_STRIDE
    span = CMP_BLOCK // CMP_STRIDE
    mat = np.zeros((n_slc, ncp), np.float32)
    for j in range(n_slc):
        for m in range(ratio):
            for n in range(span):
                i = ratio * j + m - n
                if 0 <= i < ncp - 1:
                    mat[j, i] += 1.0
    return mat


def _layer(x, p_l, w_in, pe_k, w1_k, w2_k, pe_v, w1_v, w2_v, lq1, lk1, lq2, lk2, subln_g, w_bn, w_bd, w_out,
           ln1_g, ln1_b, w_ffn_in, conv_w, conv_b, w_down, ln2_g, ln2_b, w_pp, w_pg, tab, lambda_init,
           alpha):
    B, S, D = x.shape
    T = B * S
    ncp = S // CMP_STRIDE
    n_slc = S // SLC_BLOCK
    q_w = NSA_HEADS * HEAD_DIM
    kv_w = NSA_GROUPS * HEAD_DIM
    dqk_w = DIFF_HEADS * 2 * HEAD_DIM
    sizes = (q_w,) + (kv_w,) * 6 + (NSA_HEADS * 3, dqk_w, dqk_w, dqk_w, D, D)
    offs = np.concatenate([[0], np.cumsum(sizes)])
    col = lambda i: w_in[:, int(offs[i]):int(offs[i + 1])]
    scale = HEAD_DIM ** -0.5 * LOG2E

    n_idx = np.arange(q_w)
    perm = (NSA_REP * ((n_idx % LANES) // HEAD_DIM) + n_idx // LANES) * HEAD_DIM + n_idx % HEAD_DIM
    w_main = jnp.concatenate([col(0)[:, perm] * scale] + [col(i) for i in range(3, 7)]
                             + [col(8) * scale, col(9), col(10)], axis=1).astype(BF16)
    w_cmp = jnp.concatenate([col(1), col(2)], axis=1).astype(BF16)
    w_gate = jnp.pad(col(7), ((0, 0), (0, LANES - NSA_HEADS * 3))).astype(BF16)
    x2d = x.reshape(T, D)
    proj, k_cmp, v_cmp, gates = _proj(x2d, w_main, w_cmp, w_gate)
    proj3 = proj.reshape(B, S, proj.shape[1])
    gates3 = gates.reshape(B, S, LANES)
    c_kslc, c_vslc, c_kwin, c_vwin = (q_w // LANES + i for i in range(4))
    c_dq = q_w // LANES + 4
    c_dk = c_dq + DIFF_HEADS
    c_dv = c_dk + DIFF_HEADS

    cw = CMP_STRIDE * kv_w
    rep = lambda a: jnp.broadcast_to(a.reshape(2, 2, CMP_STRIDE, 1, HEAD_DIM, -1),
                                     (2, 2, CMP_STRIDE, NSA_GROUPS, HEAD_DIM, a.shape[-1]))
    pe = rep(jnp.stack([pe_k, pe_v])[..., None]).reshape(2, 2, 1, cw)
    w1 = rep(jnp.stack([w1_k, w1_v])).reshape(2, 2, cw, CMP_HIDDEN).astype(BF16)
    w2 = jnp.stack([w2_k, w2_v])
    w2p = jnp.stack([jnp.pad(w2, ((0, 0), (0, 0), (g * HEAD_DIM, LANES - (g + 1) * HEAD_DIM)))
                     for g in range(NSA_GROUPS)], axis=1).astype(BF16)
    cmp_kv = _compress(k_cmp.reshape(B, ncp, cw), v_cmp.reshape(B, ncp, cw), pe, w1, w2p)

    matt = jnp.asarray(_slc_from_cmp_t(ncp, n_slc), BF16)
    ocmp, sel = _cmp_attention(tab, proj3, cmp_kv, gates3, matt, n_slc)
    et_np = np.zeros((S, LANES), np.float32)
    et_np[np.arange(S), np.arange(S) // SLC_BLOCK] = 2.0 ** MASK_EXP
    y_nsa = _nsa_attention(tab, proj3, sel, jnp.asarray(et_np, BF16), gates3, ocmp,
                           (c_kslc, c_vslc, c_kwin, c_vwin))

    lam4 = jnp.pad(jnp.stack([lq1, lk1, lq2, lk2]), ((0, SUBLANES - 4), (0, 0)))
    y_diff = _diff_attention(tab, proj3, lam4, subln_g.reshape(1, LANES), (c_dq, c_dk, c_dv), lambda_init)

    x1 = _merge(x2d, y_nsa.reshape(T, q_w), y_diff.reshape(T, dqk_w),
                col(11).astype(BF16), col(12).astype(BF16), w_bn[perm].astype(BF16), w_bd.astype(BF16),
                w_out.astype(BF16), ln1_g.reshape(1, D), ln1_b.reshape(1, D), alpha)

    cw = jnp.pad(conv_w, ((0, SUBLANES - CONV_WIDTH), (0, 0)))
    out = _ffn(x1, p_l.reshape(T, p_l.shape[-1]), w_ffn_in.astype(BF16), cw, conv_b.reshape(1, -1),
               w_down.astype(BF16), ln2_g.reshape(1, D), ln2_b.reshape(1, D),
               w_pg.astype(BF16), w_pp.astype(BF16), alpha, S)
    return out.reshape(B, S, D)


def kernel(x, p, w_in, nsa_cmp_pe_k, nsa_cmp_w1_k, nsa_cmp_w2_k, nsa_cmp_pe_v, nsa_cmp_w1_v, nsa_cmp_w2_v, diff_lambda_q1, diff_lambda_k1, diff_lambda_q2, diff_lambda_k2, diff_subln_g, w_branch_nsa, w_branch_diff, w_out, ln1_g, ln1_b, w_ffn_in, ffn_conv_w, ffn_conv_b, w_ffn_down, ln2_g, ln2_b, w_ple_proj, w_ple_gate, rel_bias_table):
    depth = w_in.shape[0]
    alpha = (2.0 * depth) ** 0.25
    for l in range(depth):
        lambda_init = 0.8 - 0.6 * math.exp(-0.3 * l)
        x = _layer(x, p[l], w_in[l], nsa_cmp_pe_k[l], nsa_cmp_w1_k[l], nsa_cmp_w2_k[l], nsa_cmp_pe_v[l],
                   nsa_cmp_w1_v[l], nsa_cmp_w2_v[l], diff_lambda_q1[l], diff_lambda_k1[l], diff_lambda_q2[l],
                   diff_lambda_k2[l], diff_subln_g[l], w_branch_nsa[l], w_branch_diff[l], w_out[l], ln1_g[l],
                   ln1_b[l], w_ffn_in[l], ffn_conv_w[l], ffn_conv_b[l], w_ffn_down[l], ln2_g[l], ln2_b[l],
                   w_ple_proj[l], w_ple_gate[l], rel_bias_table, lambda_init, alpha)
    return x
```

```python
import functools
import math

import jax
import jax.numpy as jnp
import numpy as np
from jax import lax
from jax.experimental import pallas as pl
from jax.experimental.pallas import tpu as pltpu

F32 = jnp.float32
BF16 = jnp.bfloat16

NSA_HEADS = 8
NSA_GROUPS = 2
NSA_REP = NSA_HEADS // NSA_GROUPS
HEAD_DIM = 64
CMP_BLOCK = 32
CMP_STRIDE = 16
CMP_HIDDEN = 256
SLC_BLOCK = 64
SLC_TOPK = 16
SLC_LOCAL = 2
WINDOW = 512
DIFF_HEADS = 4
REL_BUCKETS = 32
REL_MAX_EXACT = 16
REL_MAX_DIST = 128
D_FF = 2816
CONV_WIDTH = 3
LN_EPS = 1e-5
NEG_INF = -1e30
BIG = 1e30
LOG2E = math.log2(math.e)
MASK_EXP = 100

LANES = 128
SUBLANES = 8
VMEM_LIMIT = 56 * 1024 * 1024

CMP_TQ = 256
NSA_TQ = 256
DIFF_TQ = 512
NSA_FAR_KEYS = 512
DIFF_FAR_KEYS = 2048
NSA_ONLINE_KEYS = 512
DIFF_ONLINE_KEYS = 2048
ROW_TILE = 512
WIDE_TILE = 1024
FF_CHUNK = 512
HALO = 16


def _rel_breakpoints():
    n = np.arange(0, 4 * REL_MAX_DIST)
    large = REL_MAX_EXACT + (np.log(np.maximum(n, 1).astype(np.float32) / REL_MAX_EXACT)
                             / np.float32(math.log(REL_MAX_DIST / REL_MAX_EXACT))
                             * (REL_BUCKETS - REL_MAX_EXACT)).astype(np.int32)
    bucket = np.where(n < REL_MAX_EXACT, n, np.minimum(large, REL_BUCKETS - 1))
    assert np.all(np.diff(bucket) >= 0)
    return [int(np.argmax(bucket >= b)) for b in range(1, REL_BUCKETS)]


REL_BREAKS = _rel_breakpoints()


def _dot(a, b):
    return jnp.dot(a, b, preferred_element_type=F32)


def _dot_nt(a, b):
    return lax.dot_general(a, b, (((1,), (1,)), ((), ())), preferred_element_type=F32)


def _rel_bias(dist, tab_ref, head, shift):
    val = jnp.full(dist.shape, (tab_ref[0, head] - shift) * LOG2E, F32)
    for b, brk in enumerate(REL_BREAKS, start=1):
        val = jnp.where(dist >= brk, (tab_ref[b, head] - shift) * LOG2E, val)
    return val


def _bias_lookup(table, dist):
    assert REL_BREAKS[-1] < LANES
    near = jnp.clip(dist, 0, LANES - 1)
    cols = [jnp.take_along_axis(table, near[:, c:c + LANES], axis=1) for c in range(0, dist.shape[1], LANES)]
    return jnp.concatenate(cols, axis=1)


def _gelu(x):
    c = math.sqrt(2.0 / math.pi)
    half = 0.5 * x
    return half + half * jnp.tanh(x * (c + (c * 0.044715) * (x * x)))


def _layer_norm(z, g, b):
    mu = jnp.mean(z, axis=-1, keepdims=True)
    zc = z - mu
    var = jnp.mean(zc * zc, axis=-1, keepdims=True)
    return zc * lax.rsqrt(var + LN_EPS) * g + b


def _params(*sem):
    return pltpu.CompilerParams(dimension_semantics=sem, vmem_limit_bytes=VMEM_LIMIT)


def _proj_kernel(x_ref, wm_ref, wc_ref, wg_ref, om_ref, ok_ref, ov_ref, og_ref, k_scr, v_scr):
    xb = x_ref[...].astype(BF16)
    n = wm_ref.shape[1]
    for c in range(0, n, 2 * LANES):
        w = min(2 * LANES, n - c)
        om_ref[:, c:c + w] = _dot(xb, wm_ref[:, c:c + w]).astype(om_ref.dtype)
    kv = _dot(xb, wc_ref[...])
    k_scr[...] = kv[:, :LANES]
    v_scr[...] = kv[:, LANES:]
    chunks = k_scr.shape[0] // CMP_STRIDE
    for t in range(CMP_STRIDE):
        token_t = pl.ds(t, chunks, stride=CMP_STRIDE)
        ok_ref[:, t * LANES:(t + 1) * LANES] = k_scr[token_t, :].astype(ok_ref.dtype)
        ov_ref[:, t * LANES:(t + 1) * LANES] = v_scr[token_t, :].astype(ov_ref.dtype)
    og_ref[...] = _dot(xb, wg_ref[...])


def _proj(x2d, w_main, w_cmp, w_gate):
    T, D = x2d.shape
    n = w_main.shape[1]
    row = lambda width: pl.BlockSpec((ROW_TILE, width), lambda i: (i, 0))
    const = lambda width: pl.BlockSpec((D, width), lambda i: (0, 0))
    chunk_rows = pl.BlockSpec((ROW_TILE // CMP_STRIDE, CMP_STRIDE * LANES), lambda i: (i, 0))
    return pl.pallas_call(
        _proj_kernel,
        grid=(T // ROW_TILE,),
        in_specs=[row(D), const(n), const(2 * LANES), const(LANES)],
        out_specs=[row(n), chunk_rows, chunk_rows, row(LANES)],
        out_shape=[jax.ShapeDtypeStruct((T, n), BF16),
                   jax.ShapeDtypeStruct((T // CMP_STRIDE, CMP_STRIDE * LANES), BF16),
                   jax.ShapeDtypeStruct((T // CMP_STRIDE, CMP_STRIDE * LANES), BF16),
                   jax.ShapeDtypeStruct((T, LANES), F32)],
        scratch_shapes=[pltpu.VMEM((ROW_TILE, LANES), F32), pltpu.VMEM((ROW_TILE, LANES), F32)],
        compiler_params=_params("arbitrary"),
        name="proj",
    )(x2d, w_main, w_cmp, w_gate)


def _compress_kernel(xk_ref, xv_ref, pe_ref, w1_ref, w2_ref, o_ref):
    ncp, cw = xk_ref.shape[1], xk_ref.shape[2]
    lane = lax.broadcasted_iota(jnp.int32, (ncp, cw), 1)
    group = jnp.bitwise_and(jnp.right_shift(lane, int(math.log2(HEAD_DIM))), NSA_GROUPS - 1)
    for s, x_ref in enumerate((xk_ref, xv_ref)):
        x = x_ref[0].astype(F32)
        xa = x + pe_ref[s, 0]
        xb = x + pe_ref[s, 1]
        acc = jnp.zeros((ncp, LANES), F32)
        for g in range(NSA_GROUPS):
            a = _dot(jnp.where(group == g, xa, 0.0).astype(BF16), w1_ref[s, 0])
            b = _dot(jnp.where(group == g, xb, 0.0).astype(BF16), w1_ref[s, 1])
            h = a + pltpu.roll(b, ncp - 1, 0)
            acc = acc + _dot(_gelu(h).astype(BF16), w2_ref[s, g])
        o_ref[s, 0] = acc.astype(o_ref.dtype)


def _compress(xk, xv, pe, w1, w2):
    B, ncp, cw = xk.shape
    x_spec = pl.BlockSpec((1, ncp, cw), lambda b: (b, 0, 0))
    const = lambda shape: pl.BlockSpec(shape, lambda b: (0, 0, 0, 0))
    return pl.pallas_call(
        _compress_kernel,
        grid=(B,),
        in_specs=[x_spec, x_spec, const(pe.shape), const(w1.shape), const(w2.shape)],
        out_specs=pl.BlockSpec((2, 1, ncp, LANES), lambda b: (0, b, 0, 0)),
        out_shape=jax.ShapeDtypeStruct((2, B, ncp, LANES), BF16),
        compiler_params=_params("arbitrary"),
        name="compress",
    )(xk, xv, pe, w1, w2)


def _cmp_kernel(tab_ref, q_ref, kc_ref, vc_ref, gate_ref, matt_ref, ocmp_ref, sel_ref, bias_ref, ball_ref,
                *, n_slc):
    tq = q_ref.shape[1]
    ncp = kc_ref.shape[2]
    q0 = pl.program_id(0) * tq
    lane = lax.broadcasted_iota(jnp.int32, (tq, LANES), 1)

    @pl.when((pl.program_id(0) == 0) & (pl.program_id(1) == 0))
    def _():
        for head in range(NSA_HEADS):
            bias_ref[head] = _rel_bias(lane, tab_ref, head, 0.0)

    @pl.when(pl.program_id(1) == 0)
    def _():
        t_idx = q0 + lax.broadcasted_iota(jnp.int32, (tq, ncp), 0)
        c_idx = lax.broadcasted_iota(jnp.int32, (tq, ncp), 1)
        dist = t_idx - (c_idx * CMP_STRIDE + (CMP_BLOCK - 1))
        for head in range(NSA_HEADS):
            ball_ref[head * tq:(head + 1) * tq, :] = jnp.where(dist >= 0, _bias_lookup(bias_ref[head], dist), NEG_INF)

    gates = jax.nn.sigmoid(gate_ref[0])
    eye = (lax.broadcasted_iota(jnp.int32, (tq, tq), 0)
           == lax.broadcasted_iota(jnp.int32, (tq, tq), 1)).astype(BF16)
    kc = kc_ref[0, 0]
    vc = vc_ref[0, 0]
    jrow_i = lax.broadcasted_iota(jnp.int32, (n_slc, tq), 0)
    cur = jnp.right_shift(q0 + lax.broadcasted_iota(jnp.int32, (n_slc, tq), 1), int(math.log2(SLC_BLOCK)))
    forced = (jrow_i == 0) | ((cur - jrow_i >= 0) & (cur - jrow_i < SLC_LOCAL))
    blk_valid = jrow_i <= cur
    jrow = jrow_i.astype(F32)
    q_parts = []
    for g in range(NSA_GROUPS):
        lane_g = (lane >= HEAD_DIM * g) & (lane < HEAD_DIM * (g + 1))
        for r in range(NSA_REP):
            qb = q_ref[0, :, r * LANES:(r + 1) * LANES]
            q_parts.append(jnp.where(lane_g, qb, jnp.zeros_like(qb)))
    logit = _dot_nt(jnp.concatenate(q_parts, axis=0), kc) + ball_ref[...]
    m = jnp.max(logit, axis=-1, keepdims=True)
    e = jnp.exp2(logit - m)
    any_valid = (q0 + lax.broadcasted_iota(jnp.int32, (tq, 1), 0) >= CMP_BLOCK - 1).astype(F32)
    p_all = e * (jnp.concatenate([any_valid] * NSA_HEADS, axis=0) / jnp.sum(e, axis=-1, keepdims=True))
    o_all = _dot(p_all.astype(BF16), vc)
    gated = [[gates[:, h * 3:h * 3 + 1] * o_all[h * tq:(h + 1) * tq]
              for h in range(g * NSA_REP, (g + 1) * NSA_REP)] for g in range(NSA_GROUPS)]
    for g in range(NSA_GROUPS):
        psum = functools.reduce(lambda a, b: a + b,
                                [p_all[h * tq:(h + 1) * tq] for h in range(g * NSA_REP, (g + 1) * NSA_REP)])
        hi = psum.astype(BF16)
        lo = (psum - hi.astype(F32)).astype(BF16)
        p_slc = _dot_nt(matt_ref[...], hi) + _dot_nt(matt_ref[...], lo)
        score = jnp.where(forced, BIG, jnp.where(blk_valid, p_slc, NEG_INF))
        sel = jnp.zeros((n_slc, tq), F32)
        for _ in range(min(SLC_TOPK, n_slc)):
            mx = jnp.max(score, axis=0, keepdims=True)
            idx = jnp.min(jnp.where(score == mx, jrow, float(n_slc)), axis=0, keepdims=True)
            hit = jrow == idx
            sel = jnp.where(hit, 1.0, sel)
            score = jnp.where(hit, -3.0e38, score)
        selm1 = (sel - 1.0).astype(BF16)
        if n_slc < LANES:
            selm1 = jnp.concatenate([selm1, jnp.zeros((LANES - n_slc, tq), BF16)], axis=0)
        sel_ref[0, g] = _dot_nt(eye, selm1).astype(sel_ref.dtype)
    for r in range(NSA_REP):
        ocmp_ref[0, :, r * LANES:(r + 1) * LANES] = jnp.where(lane < HEAD_DIM, gated[0][r], gated[1][r])


def _cmp_attention(tab, proj3, cmp_kv, gates3, matt, n_slc):
    B, S, _ = proj3.shape
    ncp = cmp_kv.shape[2]
    tq = CMP_TQ
    return pl.pallas_call(
        functools.partial(_cmp_kernel, n_slc=n_slc),
        grid=(S // tq, B),
        in_specs=[pl.BlockSpec(memory_space=pltpu.SMEM),
                  pl.BlockSpec((1, tq, 4 * LANES), lambda i, b: (b, i, 0)),
                  pl.BlockSpec((1, 1, ncp, LANES), lambda i, b: (0, b, 0, 0)),
                  pl.BlockSpec((1, 1, ncp, LANES), lambda i, b: (1, b, 0, 0)),
                  pl.BlockSpec((1, tq, LANES), lambda i, b: (b, i, 0)),
                  pl.BlockSpec((n_slc, ncp), lambda i, b: (0, 0))],
        out_specs=[pl.BlockSpec((1, tq, 4 * LANES), lambda i, b: (b, i, 0)),
                   pl.BlockSpec((1, NSA_GROUPS, tq, LANES), lambda i, b: (b, 0, i, 0))],
        out_shape=[jax.ShapeDtypeStruct((B, S, 4 * LANES), F32),
                   jax.ShapeDtypeStruct((B, NSA_GROUPS, S, LANES), BF16)],
        scratch_shapes=[pltpu.VMEM((NSA_HEADS, tq, LANES), F32), pltpu.VMEM((NSA_HEADS * tq, ncp), F32)],
        compiler_params=_params("arbitrary", "arbitrary"),
        name="cmp",
    )(tab, proj3, cmp_kv, cmp_kv, gates3, matt)


class _OnlineSoftmax:
    def __init__(self, score, value, m_ref, l_ref, acc_ref, sub_keys):
        self.score, self.value, self.sub_keys = score, value, sub_keys
        self.m_ref, self.l_ref, self.acc_ref = m_ref, l_ref, acc_ref
        m_ref[...] = jnp.full(m_ref.shape, NEG_INF, F32)
        l_ref[...] = jnp.zeros(l_ref.shape, F32)
        acc_ref[...] = jnp.zeros(acc_ref.shape, F32)

    def step(self, k0, width, bias):
        m_ref, l_ref, acc_ref = self.m_ref, self.l_ref, self.acc_ref
        for j in range(0, width, self.sub_keys):
            w = min(self.sub_keys, width - j)
            s = self.score(k0 + j, w)
            cols = []
            for c in range(0, w, LANES):
                b = None if bias is None else bias(j + c)
                cols.append(s[:, c:c + LANES] if b is None else s[:, c:c + LANES] + b)
            m_old = m_ref[...]
            m_new = jnp.maximum(m_old, jnp.max(functools.reduce(jnp.maximum, cols), axis=-1, keepdims=True))
            alpha = jnp.exp2(m_old - m_new)
            ps = [jnp.exp2(col - m_new) for col in cols]
            l_ref[...] = alpha * l_ref[...] + functools.reduce(lambda a, b: a + b, ps)
            acc_ref[...] = alpha * acc_ref[...] + _dot(
                jnp.concatenate([p.astype(BF16) for p in ps], axis=1), self.value(k0 + j, w))
            m_ref[...] = m_new

    def result(self):
        return self.acc_ref[...] / jnp.sum(self.l_ref[...], axis=-1, keepdims=True)


def _causal_far_loop(fn, qt, tq, far_keys):
    far_tiles = far_keys // tq

    def far_body(i, carry):
        fn(i * far_keys, far_keys, None)
        return carry

    lax.fori_loop(0, lax.div(jnp.maximum(qt - 1, 0), far_tiles), far_body, 0)


def _causal_segments(fn, qt, tq, far_keys, near_bias):
    n_far = jnp.maximum(qt - 1, 0)
    far_tiles = far_keys // tq
    _causal_far_loop(fn, qt, tq, far_keys)
    for rem in range(far_tiles):
        lead = rem * tq
        bias = (lambda c, lead=lead: None if c < lead else near_bias(c - lead))
        pl.when((qt >= 1) & (lax.rem(n_far, far_tiles) == rem))(
            functools.partial(fn, (qt - 1 - rem) * tq, lead + 2 * tq, bias))
    pl.when(qt == 0)(functools.partial(fn, 0, tq, lambda c: near_bias(tq + c)))


def _key_slice(k0, width, tq):
    return pl.ds(k0 if isinstance(k0, int) else pl.multiple_of(k0, tq), width)


def _rows(ref, k0, width, tq):
    return ref[0, _key_slice(k0, width, tq), :]


def _nsa_kernel(tab_ref, q_ref, ks_ref, vs_ref, kw_ref, vw_ref, sel_ref, et_ref, gate_ref, ocmp_ref,
                y_ref, tb_ref, m_ref, l_ref, acc_ref, wm_ref, wl_ref, wacc_ref):
    tq = q_ref.shape[1]
    n_win = WINDOW // tq
    qt = pl.program_id(1)

    @pl.when((pl.program_id(0) == 0) & (qt == 0))
    def _():
        ti = lax.broadcasted_iota(jnp.int32, (tq, tq), 0)
        ki = lax.broadcasted_iota(jnp.int32, (tq, tq), 1)
        for g in range(NSA_GROUPS):
            for r in range(NSA_REP):
                head = g * NSA_REP + r
                far_bias = tab_ref[REL_BUCKETS - 1, head]
                rows = slice(head * tq, (head + 1) * tq)
                tb_ref[rows, 0:tq] = jnp.where(ti < ki, 0.0, NEG_INF)
                for j in range(2, n_win):
                    tb_ref[rows, (n_win - j) * tq:(n_win - j + 1) * tq] = jnp.zeros((tq, tq), F32)
                table = _rel_bias(lax.broadcasted_iota(jnp.int32, (tq, LANES), 1), tab_ref, head, far_bias)
                tb_ref[rows, (n_win - 1) * tq:n_win * tq] = _bias_lookup(table, ti - ki + tq)
                tb_ref[rows, n_win * tq:(n_win + 1) * tq] = jnp.where(
                    ti >= ki, _bias_lookup(table, ti - ki), NEG_INF)

    lane = lax.broadcasted_iota(jnp.int32, (tq, LANES), 1)
    gates = jax.nn.sigmoid(gate_ref[0])

    q_parts, sel_parts = [], []
    for g in range(NSA_GROUPS):
        lane_g = (lane >= HEAD_DIM * g) & (lane < HEAD_DIM * (g + 1))
        for r in range(NSA_REP):
            qb = q_ref[0, :, r * LANES:(r + 1) * LANES]
            q_parts.append(jnp.where(lane_g, qb, jnp.zeros_like(qb)))
            sel_parts.append(sel_ref[0, g])
    q_all = jnp.concatenate(q_parts, axis=0)
    qs_all = jnp.concatenate([q_all, jnp.concatenate(sel_parts, axis=0)], axis=1)

    def bias_from(col0):
        return lambda c: tb_ref[:, col0 + c:col0 + c + LANES]

    def slc_score(k0, width):
        et = et_ref[_key_slice(k0, width, tq), :]
        return _dot_nt(qs_all, jnp.concatenate([_rows(ks_ref, k0, width, tq), et], axis=1))

    def win_score(k0, width):
        return _dot_nt(q_all, _rows(kw_ref, k0, width, tq))

    slc = _OnlineSoftmax(slc_score, lambda k0, width: _rows(vs_ref, k0, width, tq),
                         m_ref, l_ref, acc_ref, NSA_ONLINE_KEYS)
    win = _OnlineSoftmax(win_score, lambda k0, width: _rows(vw_ref, k0, width, tq),
                         wm_ref, wl_ref, wacc_ref, NSA_ONLINE_KEYS)

    def near_and_window(slc_tiles, win_tiles):
        slc_lead = (slc_tiles - 2) * tq if slc_tiles >= 2 else 0
        slc_bias = bias_from((n_win - 1) * tq) if slc_tiles >= 2 else bias_from(n_win * tq)
        slc.step((qt + 1 - slc_tiles) * tq, slc_tiles * tq,
                 lambda c: None if c < slc_lead else slc_bias(c - slc_lead))
        win.step((qt + 1 - win_tiles) * tq, win_tiles * tq, bias_from((n_win + 1 - win_tiles) * tq))

    _causal_far_loop(slc.step, qt, tq, NSA_FAR_KEYS)
    far_tiles = NSA_FAR_KEYS // tq
    left_over = lax.rem(jnp.maximum(qt - 1, 0), far_tiles)
    for n in range(n_win):
        pl.when(qt == n)(functools.partial(near_and_window, n + 1, n + 1))
    for rem in range(far_tiles):
        pl.when((qt >= n_win) & (left_over == rem))(functools.partial(near_and_window, rem + 2, n_win + 1))
    o_slc = slc.result()
    o_win = win.result()

    for r in range(NSA_REP):
        ys = []
        for g in range(NSA_GROUPS):
            head = g * NSA_REP + r
            rows = slice(head * tq, (head + 1) * tq)
            ys.append(gates[:, head * 3 + 1:head * 3 + 2] * o_slc[rows]
                      + gates[:, head * 3 + 2:head * 3 + 3] * o_win[rows])
        cols = slice(r * LANES, (r + 1) * LANES)
        y_ref[0, :, cols] = (ocmp_ref[0, :, cols] + jnp.where(lane < HEAD_DIM, ys[0], ys[1])).astype(y_ref.dtype)


def _nsa_attention(tab, proj3, sel, et, gates3, ocmp, col_blocks):
    B, S, _ = proj3.shape
    tq = NSA_TQ
    assert WINDOW % tq == 0 and WINDOW // tq >= 2 and S % tq == 0
    n_win = WINDOW // tq
    ks_c, vs_c, kw_c, vw_c = col_blocks
    rows = NSA_HEADS * tq

    def kv_spec(c):
        return pl.BlockSpec((1, S, LANES), lambda b, i: (b, 0, c))

    return pl.pallas_call(
        _nsa_kernel,
        grid=(B, S // tq),
        in_specs=[pl.BlockSpec(memory_space=pltpu.SMEM),
                  pl.BlockSpec((1, tq, 4 * LANES), lambda b, i: (b, i, 0)),
                  kv_spec(ks_c), kv_spec(vs_c), kv_spec(kw_c), kv_spec(vw_c),
                  pl.BlockSpec((1, NSA_GROUPS, tq, LANES), lambda b, i: (b, 0, i, 0)),
                  pl.BlockSpec((S, LANES), lambda b, i: (0, 0)),
                  pl.BlockSpec((1, tq, LANES), lambda b, i: (b, i, 0)),
                  pl.BlockSpec((1, tq, 4 * LANES), lambda b, i: (b, i, 0))],
        out_specs=pl.BlockSpec((1, tq, 4 * LANES), lambda b, i: (b, i, 0)),
        out_shape=jax.ShapeDtypeStruct((B, S, 4 * LANES), BF16),
        scratch_shapes=[pltpu.VMEM((rows, (n_win + 1) * tq), F32)] + [pltpu.VMEM((rows, LANES), F32)] * 6,
        compiler_params=_params("arbitrary", "arbitrary"),
        name="nsa",
    )(tab, proj3, proj3, proj3, proj3, proj3, sel, et, gates3, ocmp)


def _diff_kernel(tab_ref, q_ref, k_ref, v_ref, lam_ref, g_ref, y_ref, tb_ref, m_ref, l_ref, acc_ref,
                 *, lambda_init):
    tq = q_ref.shape[1]
    h = pl.program_id(0)
    qt = pl.program_id(2)
    lane = lax.broadcasted_iota(jnp.int32, (tq, LANES), 1)

    @pl.when((pl.program_id(1) == 0) & (qt == 0))
    def _():
        ti = lax.broadcasted_iota(jnp.int32, (tq, tq), 0)
        ki = lax.broadcasted_iota(jnp.int32, (tq, tq), 1)
        head = NSA_HEADS + h
        table = _rel_bias(lane, tab_ref, head, tab_ref[REL_BUCKETS - 1, head])
        tb_ref[:, 0:tq] = _bias_lookup(table, ti - ki + tq)
        tb_ref[:, tq:2 * tq] = jnp.where(ti >= ki, _bias_lookup(table, ti - ki), NEG_INF)

    qb = q_ref[0]
    zero = jnp.zeros_like(qb)
    q2 = jnp.concatenate([jnp.where(lane < HEAD_DIM, qb, zero), jnp.where(lane >= HEAD_DIM, qb, zero)], axis=0)

    def score(k0, width):
        return _dot_nt(q2, _rows(k_ref, k0, width, tq))

    def near_bias(c):
        tb = tb_ref[:, c:c + LANES]
        return jnp.concatenate([tb, tb], axis=0)

    attn = _OnlineSoftmax(score, lambda k0, width: _rows(v_ref, k0, width, tq), m_ref, l_ref, acc_ref,
                          DIFF_ONLINE_KEYS)
    _causal_segments(attn.step, qt, tq, DIFF_FAR_KEYS, near_bias)
    a = attn.result()
    lq1, lk1, lq2, lk2 = lam_ref[0:1, :], lam_ref[1:2, :], lam_ref[2:3, :], lam_ref[3:4, :]
    lam = (jnp.exp(jnp.sum(lq1 * lk1, axis=-1, keepdims=True))
           - jnp.exp(jnp.sum(lq2 * lk2, axis=-1, keepdims=True)) + lambda_init)
    o = a[:tq] - lam * a[tq:]
    o = o * lax.rsqrt(jnp.mean(o * o, axis=-1, keepdims=True) + LN_EPS) * g_ref[...]
    y_ref[0] = (o * (1.0 - lambda_init)).astype(y_ref.dtype)


def _diff_attention(tab, proj3, lam4, subln_g, col_blocks, lambda_init):
    B, S, _ = proj3.shape
    tq = min(DIFF_TQ, S)
    q_c, k_c, v_c = col_blocks
    return pl.pallas_call(
        functools.partial(_diff_kernel, lambda_init=lambda_init),
        grid=(DIFF_HEADS, B, S // tq),
        in_specs=[pl.BlockSpec(memory_space=pltpu.SMEM),
                  pl.BlockSpec((1, tq, LANES), lambda h, b, i: (b, i, q_c + h)),
                  pl.BlockSpec((1, S, LANES), lambda h, b, i: (b, 0, k_c + h)),
                  pl.BlockSpec((1, S, LANES), lambda h, b, i: (b, 0, v_c + h)),
                  pl.BlockSpec((SUBLANES, HEAD_DIM), lambda h, b, i: (0, 0)),
                  pl.BlockSpec((1, LANES), lambda h, b, i: (0, 0))],
        out_specs=pl.BlockSpec((1, tq, LANES), lambda h, b, i: (b, i, h)),
        out_shape=jax.ShapeDtypeStruct((B, S, DIFF_HEADS * LANES), BF16),
        scratch_shapes=[pltpu.VMEM((tq, 2 * tq), F32),
                        pltpu.VMEM((2 * tq, LANES), F32),
                        pltpu.VMEM((2 * tq, LANES), F32),
                        pltpu.VMEM((2 * tq, LANES), F32)],
        compiler_params=_params("arbitrary", "arbitrary", "arbitrary"),
        name="diff",
    )(tab, proj3, proj3, proj3, lam4, subln_g)


def _merge_kernel(x_ref, yn_ref, yd_ref, wgn_ref, wgd_ref, wbn_ref, wbd_ref, wo_ref, g_ref, b_ref, o_ref,
                  *, alpha):
    for r0 in range(0, x_ref.shape[0], ROW_TILE):
        rows = slice(r0, r0 + ROW_TILE)
        x = x_ref[rows, :]
        xb = x.astype(BF16)
        merged = (jax.nn.sigmoid(_dot(xb, wgn_ref[...])) * _dot(yn_ref[rows, :], wbn_ref[...])
                  + jax.nn.sigmoid(_dot(xb, wgd_ref[...])) * _dot(yd_ref[rows, :], wbd_ref[...]))
        z = alpha * x + _dot(merged.astype(BF16), wo_ref[...])
        o_ref[rows, :] = _layer_norm(z, g_ref[...], b_ref[...])


def _merge(x2d, y_nsa, y_diff, wgn, wgd, wbn, wbd, wo, ln_g, ln_b, alpha):
    T, D = x2d.shape
    const = lambda shape: pl.BlockSpec(shape, lambda i: (0, 0))
    return pl.pallas_call(
        functools.partial(_merge_kernel, alpha=alpha),
        grid=(T // WIDE_TILE,),
        in_specs=[pl.BlockSpec((WIDE_TILE, D), lambda i: (i, 0)),
                  pl.BlockSpec((WIDE_TILE, y_nsa.shape[1]), lambda i: (i, 0)),
                  pl.BlockSpec((WIDE_TILE, y_diff.shape[1]), lambda i: (i, 0)),
                  const(wgn.shape), const(wgd.shape), const(wbn.shape), const(wbd.shape), const(wo.shape),
                  const((1, D)), const((1, D))],
        out_specs=pl.BlockSpec((WIDE_TILE, D), lambda i: (i, 0)),
        out_shape=jax.ShapeDtypeStruct((T, D), F32),
        compiler_params=_params("arbitrary"),
        name="merge",
    )(x2d, y_nsa, y_diff, wgn, wgd, wbn, wbd, wo, ln_g, ln_b)


def _ffn_kernel(x_ref, halo_ref, p_ref, w_ref, cw_ref, cb_ref, wd_ref, g_ref, b_ref, wpg_ref, wpp_ref,
                o_ref, acc_ref, *, alpha, tiles_per_seq):
    d_ff = wd_ref.shape[0]
    keep = (pl.program_id(0) % tiles_per_seq != 0).astype(F32)
    for r0 in range(0, x_ref.shape[0], ROW_TILE):
        rows = slice(r0, r0 + ROW_TILE)
        x = x_ref[rows, :]
        xb = x.astype(BF16)
        hb = (halo_ref[...] if r0 == 0 else x_ref[r0 - HALO:r0, :]).astype(BF16)
        for c in range(0, d_ff, FF_CHUNK):
            width = min(FF_CHUNK, d_ff - c)
            cols = slice(c, c + width)
            wg = w_ref[:, cols]
            gm = _dot(xb, wg)
            gh = _dot(hb, wg)
            if r0 == 0:
                gh = gh * keep
            um = _dot(xb, w_ref[:, d_ff + c:d_ff + c + width])
            gext = jnp.concatenate([gh, gm], axis=0)
            g1 = pltpu.roll(gext, 1, 0)[HALO:]
            g2 = pltpu.roll(gext, 2, 0)[HALO:]
            conv = cb_ref[:, cols] + cw_ref[0:1, cols] * g2 + cw_ref[1:2, cols] * g1 + cw_ref[2:3, cols] * gm
            acc_ref[rows, cols] = (_gelu(conv) * um).astype(BF16)
        x2 = _layer_norm(alpha * x + _dot(acc_ref[rows, :], wd_ref[...]), g_ref[...], b_ref[...])
        gate = jax.nn.sigmoid(_dot(x2.astype(BF16), wpg_ref[...]))
        o_ref[rows, :] = x2 + gate * _dot(p_ref[rows, :].astype(BF16), wpp_ref[...])


def _ffn(x1, p2d, w_in, cw, cb, wd, ln_g, ln_b, wpg, wpp, alpha, seq):
    T, D = x1.shape
    tm = WIDE_TILE
    assert seq % tm == 0 and wd.shape[0] % LANES == 0
    hb = tm // HALO
    const = lambda shape: pl.BlockSpec(shape, lambda i: (0, 0))
    return pl.pallas_call(
        functools.partial(_ffn_kernel, alpha=alpha, tiles_per_seq=seq // tm),
        grid=(T // tm,),
        in_specs=[pl.BlockSpec((tm, D), lambda i: (i, 0)),
                  pl.BlockSpec((HALO, D), lambda i: (jnp.maximum(i * hb - 1, 0), 0)),
                  pl.BlockSpec((tm, p2d.shape[1]), lambda i: (i, 0)),
                  const(w_in.shape), const(cw.shape), const(cb.shape), const(wd.shape),
                  const((1, D)), const((1, D)), const(wpg.shape), const(wpp.shape)],
        out_specs=pl.BlockSpec((tm, D), lambda i: (i, 0)),
        out_shape=jax.ShapeDtypeStruct((T, D), F32),
        scratch_shapes=[pltpu.VMEM((tm, wd.shape[0]), BF16)],
        compiler_params=_params("arbitrary"),
        name="ffn",
    )(x1, x1, p2d, w_in, cw, cb, wd, ln_g, ln_b, wpg, wpp)


def _slc_from_cmp_t(ncp, n_slc):
    ratio = SLC_BLOCK // CMP_STRIDE
    span = CMP_BLOCK // CMP_STRIDE
    mat = np.zeros((n_slc, ncp), np.float32)
    for j in range(n_slc):
        for m in range(ratio):
            for n in range(span):
                i = ratio * j + m - n
                if 0 <= i < ncp - 1:
                    mat[j, i] += 1.0
    return mat


def _layer(x, p_l, w_in, pe_k, w1_k, w2_k, pe_v, w1_v, w2_v, lq1, lk1, lq2, lk2, subln_g, w_bn, w_bd, w_out,
           ln1_g, ln1_b, w_ffn_in, conv_w, conv_b, w_down, ln2_g, ln2_b, w_pp, w_pg, tab, lambda_init,
           alpha):
    B, S, D = x.shape
    T = B * S
    ncp = S // CMP_STRIDE
    n_slc = S // SLC_BLOCK
    q_w = NSA_HEADS * HEAD_DIM
    kv_w = NSA_GROUPS * HEAD_DIM
    dqk_w = DIFF_HEADS * 2 * HEAD_DIM
    sizes = (q_w,) + (kv_w,) * 6 + (NSA_HEADS * 3, dqk_w, dqk_w, dqk_w, D, D)
    offs = np.concatenate([[0], np.cumsum(sizes)])
    col = lambda i: w_in[:, int(offs[i]):int(offs[i + 1])]
    scale = HEAD_DIM ** -0.5 * LOG2E

    n_idx = np.arange(q_w)
    perm = (NSA_REP * ((n_idx % LANES) // HEAD_DIM) + n_idx // LANES) * HEAD_DIM + n_idx % HEAD_DIM
    w_main = jnp.concatenate([col(0)[:, perm] * scale] + [col(i) for i in range(3, 7)]
                             + [col(8) * scale, col(9), col(10)], axis=1).astype(BF16)
    w_cmp = jnp.concatenate([col(1), col(2)], axis=1).astype(BF16)
    w_gate = jnp.pad(col(7), ((0, 0), (0, LANES - NSA_HEADS * 3))).astype(BF16)
    x2d = x.reshape(T, D)
    proj, k_cmp, v_cmp, gates = _proj(x2d, w_main, w_cmp, w_gate)
    proj3 = proj.reshape(B, S, proj.shape[1])
    gates3 = gates.reshape(B, S, LANES)
    c_kslc, c_vslc, c_kwin, c_vwin = (q_w // LANES + i for i in range(4))
    c_dq = q_w // LANES + 4
    c_dk = c_dq + DIFF_HEADS
    c_dv = c_dk + DIFF_HEADS

    cw = CMP_STRIDE * kv_w
    rep = lambda a: jnp.broadcast_to(a.reshape(2, 2, CMP_STRIDE, 1, HEAD_DIM, -1),
                                     (2, 2, CMP_STRIDE, NSA_GROUPS, HEAD_DIM, a.shape[-1]))
    pe = rep(jnp.stack([pe_k, pe_v])[..., None]).reshape(2, 2, 1, cw)
    w1 = rep(jnp.stack([w1_k, w1_v])).reshape(2, 2, cw, CMP_HIDDEN).astype(BF16)
    w2 = jnp.stack([w2_k, w2_v])
    w2p = jnp.stack([jnp.pad(w2, ((0, 0), (0, 0), (g * HEAD_DIM, LANES - (g + 1) * HEAD_DIM)))
                     for g in range(NSA_GROUPS)], axis=1).astype(BF16)
    cmp_kv = _compress(k_cmp.reshape(B, ncp, cw), v_cmp.reshape(B, ncp, cw), pe, w1, w2p)

    matt = jnp.asarray(_slc_from_cmp_t(ncp, n_slc), BF16)
    ocmp, sel = _cmp_attention(tab, proj3, cmp_kv, gates3, matt, n_slc)
    et_np = np.zeros((S, LANES), np.float32)
    et_np[np.arange(S), np.arange(S) // SLC_BLOCK] = 2.0 ** MASK_EXP
    y_nsa = _nsa_attention(tab, proj3, sel, jnp.asarray(et_np, BF16), gates3, ocmp,
                           (c_kslc, c_vslc, c_kwin, c_vwin))

    lam4 = jnp.pad(jnp.stack([lq1, lk1, lq2, lk2]), ((0, SUBLANES - 4), (0, 0)))
    y_diff = _diff_attention(tab, proj3, lam4, subln_g.reshape(1, LANES), (c_dq, c_dk, c_dv), lambda_init)

    x1 = _merge(x2d, y_nsa.reshape(T, q_w), y_diff.reshape(T, dqk_w),
                col(11).astype(BF16), col(12).astype(BF16), w_bn[perm].astype(BF16), w_bd.astype(BF16),
                w_out.astype(BF16), ln1_g.reshape(1, D), ln1_b.reshape(1, D), alpha)

    cw = jnp.pad(conv_w, ((0, SUBLANES - CONV_WIDTH), (0, 0)))
    out = _ffn(x1, p_l.reshape(T, p_l.shape[-1]), w_ffn_in.astype(BF16), cw, conv_b.reshape(1, -1),
               w_down.astype(BF16), ln2_g.reshape(1, D), ln2_b.reshape(1, D),
               w_pg.astype(BF16), w_pp.astype(BF16), alpha, S)
    return out.reshape(B, S, D)


def kernel(x, p, w_in, nsa_cmp_pe_k, nsa_cmp_w1_k, nsa_cmp_w2_k, nsa_cmp_pe_v, nsa_cmp_w1_v, nsa_cmp_w2_v, diff_lambda_q1, diff_lambda_k1, diff_lambda_q2, diff_lambda_k2, diff_subln_g, w_branch_nsa, w_branch_diff, w_out, ln1_g, ln1_b, w_ffn_in, ffn_conv_w, ffn_conv_b, w_ffn_down, ln2_g, ln2_b, w_ple_proj, w_ple_gate, rel_bias_table):
    depth = w_in.shape[0]
    alpha = (2.0 * depth) ** 0.25
    for l in range(depth):
        lambda_init = 0.8 - 0.6 * math.exp(-0.3 * l)
        x = _layer(x, p[l], w_in[l], nsa_cmp_pe_k[l], nsa_cmp_w1_k[l], nsa_cmp_w2_k[l], nsa_cmp_pe_v[l],
                   nsa_cmp_w1_v[l], nsa_cmp_w2_v[l], diff_lambda_q1[l], diff_lambda_k1[l], diff_lambda_q2[l],
                   diff_lambda_k2[l], diff_subln_g[l], w_branch_nsa[l], w_branch_diff[l], w_out[l], ln1_g[l],
                   ln1_b[l], w_ffn_in[l], ffn_conv_w[l], ffn_conv_b[l], w_ffn_down[l], ln2_g[l], ln2_b[l],
                   w_ple_proj[l], w_ple_gate[l], rel_bias_table, lambda_init, alpha)
    return x
```

```python
import functools
import math

import jax
import jax.numpy as jnp
import numpy as np
from jax import lax
from jax.experimental import pallas as pl
from jax.experimental.pallas import tpu as pltpu

F32 = jnp.float32
BF16 = jnp.bfloat16

NSA_HEADS = 8
NSA_GROUPS = 2
NSA_REP = NSA_HEADS // NSA_GROUPS
HEAD_DIM = 64
CMP_BLOCK = 32
CMP_STRIDE = 16
CMP_HIDDEN = 256
SLC_BLOCK = 64
SLC_TOPK = 16
SLC_LOCAL = 2
WINDOW = 512
DIFF_HEADS = 4
REL_BUCKETS = 32
REL_MAX_EXACT = 16
REL_MAX_DIST = 128
D_FF = 2816
CONV_WIDTH = 3
LN_EPS = 1e-5
NEG_INF = -1e30
BIG = 1e30
LOG2E = math.log2(math.e)
MASK_EXP = 100

LANES = 128
SUBLANES = 8
VMEM_LIMIT = 56 * 1024 * 1024

CMP_TQ = 256
NSA_TQ = 256
DIFF_TQ = 512
NSA_FAR_KEYS = 512
DIFF_FAR_KEYS = 2048
NSA_ONLINE_KEYS = 512
DIFF_ONLINE_KEYS = 2048
ROW_TILE = 512
WIDE_TILE = 1024
FF_CHUNK = 512
HALO = 16


def _rel_breakpoints():
    n = np.arange(0, 4 * REL_MAX_DIST)
    large = REL_MAX_EXACT + (np.log(np.maximum(n, 1).astype(np.float32) / REL_MAX_EXACT)
                             / np.float32(math.log(REL_MAX_DIST / REL_MAX_EXACT))
                             * (REL_BUCKETS - REL_MAX_EXACT)).astype(np.int32)
    bucket = np.where(n < REL_MAX_EXACT, n, np.minimum(large, REL_BUCKETS - 1))
    assert np.all(np.diff(bucket) >= 0)
    return [int(np.argmax(bucket >= b)) for b in range(1, REL_BUCKETS)]


REL_BREAKS = _rel_breakpoints()


def _dot(a, b):
    return jnp.dot(a, b, preferred_element_type=F32)


def _dot_nt(a, b):
    return lax.dot_general(a, b, (((1,), (1,)), ((), ())), preferred_element_type=F32)


def _rel_bias(dist, tab_ref, head, shift):
    val = jnp.full(dist.shape, (tab_ref[0, head] - shift) * LOG2E, F32)
    for b, brk in enumerate(REL_BREAKS, start=1):
        val = jnp.where(dist >= brk, (tab_ref[b, head] - shift) * LOG2E, val)
    return val


def _bias_lookup(table, dist):
    assert REL_BREAKS[-1] < LANES
    near = jnp.clip(dist, 0, LANES - 1)
    cols = [jnp.take_along_axis(table, near[:, c:c + LANES], axis=1) for c in range(0, dist.shape[1], LANES)]
    return jnp.concatenate(cols, axis=1)


def _gelu(x):
    c = math.sqrt(2.0 / math.pi)
    half = 0.5 * x
    return half + half * jnp.tanh(x * (c + (c * 0.044715) * (x * x)))


def _layer_norm(z, g, b):
    mu = jnp.mean(z, axis=-1, keepdims=True)
    zc = z - mu
    var = jnp.mean(zc * zc, axis=-1, keepdims=True)
    return zc * lax.rsqrt(var + LN_EPS) * g + b


def _params(*sem):
    return pltpu.CompilerParams(dimension_semantics=sem, vmem_limit_bytes=VMEM_LIMIT)


def _proj_kernel(x_ref, wm_ref, wc_ref, wg_ref, om_ref, ok_ref, ov_ref, og_ref, k_scr, v_scr):
    xb = x_ref[...].astype(BF16)
    n = wm_ref.shape[1]
    for c in range(0, n, 2 * LANES):
        w = min(2 * LANES, n - c)
        om_ref[:, c:c + w] = _dot(xb, wm_ref[:, c:c + w]).astype(om_ref.dtype)
    kv = _dot(xb, wc_ref[...])
    k_scr[...] = kv[:, :LANES]
    v_scr[...] = kv[:, LANES:]
    chunks = k_scr.shape[0] // CMP_STRIDE
    for t in range(CMP_STRIDE):
        token_t = pl.ds(t, chunks, stride=CMP_STRIDE)
        ok_ref[:, t * LANES:(t + 1) * LANES] = k_scr[token_t, :].astype(ok_ref.dtype)
        ov_ref[:, t * LANES:(t + 1) * LANES] = v_scr[token_t, :].astype(ov_ref.dtype)
    og_ref[...] = _dot(xb, wg_ref[...])


def _proj(x2d, w_main, w_cmp, w_gate):
    T, D = x2d.shape
    n = w_main.shape[1]
    row = lambda width: pl.BlockSpec((ROW_TILE, width), lambda i: (i, 0))
    const = lambda width: pl.BlockSpec((D, width), lambda i: (0, 0))
    chunk_rows = pl.BlockSpec((ROW_TILE // CMP_STRIDE, CMP_STRIDE * LANES), lambda i: (i, 0))
    return pl.pallas_call(
        _proj_kernel,
        grid=(T // ROW_TILE,),
        in_specs=[row(D), const(n), const(2 * LANES), const(LANES)],
        out_specs=[row(n), chunk_rows, chunk_rows, row(LANES)],
        out_shape=[jax.ShapeDtypeStruct((T, n), BF16),
                   jax.ShapeDtypeStruct((T // CMP_STRIDE, CMP_STRIDE * LANES), BF16),
                   jax.ShapeDtypeStruct((T // CMP_STRIDE, CMP_STRIDE * LANES), BF16),
                   jax.ShapeDtypeStruct((T, LANES), F32)],
        scratch_shapes=[pltpu.VMEM((ROW_TILE, LANES), F32), pltpu.VMEM((ROW_TILE, LANES), F32)],
        compiler_params=_params("arbitrary"),
        name="proj",
    )(x2d, w_main, w_cmp, w_gate)


def _compress_kernel(xk_ref, xv_ref, pe_ref, w1_ref, w2_ref, o_ref):
    ncp, cw = xk_ref.shape[1], xk_ref.shape[2]
    lane = lax.broadcasted_iota(jnp.int32, (ncp, cw), 1)
    group = jnp.bitwise_and(jnp.right_shift(lane, int(math.log2(HEAD_DIM))), NSA_GROUPS - 1)
    for s, x_ref in enumerate((xk_ref, xv_ref)):
        x = x_ref[0].astype(F32)
        xa = x + pe_ref[s, 0]
        xb = x + pe_ref[s, 1]
        acc = jnp.zeros((ncp, LANES), F32)
        for g in range(NSA_GROUPS):
            a = _dot(jnp.where(group == g, xa, 0.0).astype(BF16), w1_ref[s, 0])
            b = _dot(jnp.where(group == g, xb, 0.0).astype(BF16), w1_ref[s, 1])
            h = a + pltpu.roll(b, ncp - 1, 0)
            acc = acc + _dot(_gelu(h).astype(BF16), w2_ref[s, g])
        o_ref[s, 0] = acc.astype(o_ref.dtype)


def _compress(xk, xv, pe, w1, w2):
    B, ncp, cw = xk.shape
    x_spec = pl.BlockSpec((1, ncp, cw), lambda b: (b, 0, 0))
    const = lambda shape: pl.BlockSpec(shape, lambda b: (0, 0, 0, 0))
    return pl.pallas_call(
        _compress_kernel,
        grid=(B,),
        in_specs=[x_spec, x_spec, const(pe.shape), const(w1.shape), const(w2.shape)],
        out_specs=pl.BlockSpec((2, 1, ncp, LANES), lambda b: (0, b, 0, 0)),
        out_shape=jax.ShapeDtypeStruct((2, B, ncp, LANES), BF16),
        compiler_params=_params("arbitrary"),
        name="compress",
    )(xk, xv, pe, w1, w2)


def _cmp_kernel(tab_ref, q_ref, kc_ref, vc_ref, gate_ref, matt_ref, ocmp_ref, sel_ref, bias_ref, ball_ref,
                *, n_slc):
    tq = q_ref.shape[1]
    ncp = kc_ref.shape[2]
    q0 = pl.program_id(0) * tq
    lane = lax.broadcasted_iota(jnp.int32, (tq, LANES), 1)

    @pl.when((pl.program_id(0) == 0) & (pl.program_id(1) == 0))
    def _():
        for head in range(NSA_HEADS):
            bias_ref[head] = _rel_bias(lane, tab_ref, head, 0.0)

    @pl.when(pl.program_id(1) == 0)
    def _():
        t_idx = q0 + lax.broadcasted_iota(jnp.int32, (tq, ncp), 0)
        c_idx = lax.broadcasted_iota(jnp.int32, (tq, ncp), 1)
        dist = t_idx - (c_idx * CMP_STRIDE + (CMP_BLOCK - 1))
        for head in range(NSA_HEADS):
            ball_ref[head * tq:(head + 1) * tq, :] = jnp.where(dist >= 0, _bias_lookup(bias_ref[head], dist), NEG_INF)

    gates = jax.nn.sigmoid(gate_ref[0])
    eye = (lax.broadcasted_iota(jnp.int32, (tq, tq), 0)
           == lax.broadcasted_iota(jnp.int32, (tq, tq), 1)).astype(BF16)
    kc = kc_ref[0, 0]
    vc = vc_ref[0, 0]
    jrow_i = lax.broadcasted_iota(jnp.int32, (n_slc, tq), 0)
    cur = jnp.right_shift(q0 + lax.broadcasted_iota(jnp.int32, (n_slc, tq), 1), int(math.log2(SLC_BLOCK)))
    forced = (jrow_i == 0) | ((cur - jrow_i >= 0) & (cur - jrow_i < SLC_LOCAL))
    blk_valid = jrow_i <= cur
    jrow = jrow_i.astype(F32)
    q_parts = []
    for g in range(NSA_GROUPS):
        lane_g = (lane >= HEAD_DIM * g) & (lane < HEAD_DIM * (g + 1))
        for r in range(NSA_REP):
            qb = q_ref[0, :, r * LANES:(r + 1) * LANES]
            q_parts.append(jnp.where(lane_g, qb, jnp.zeros_like(qb)))
    logit = _dot_nt(jnp.concatenate(q_parts, axis=0), kc) + ball_ref[...]
    m = jnp.max(logit, axis=-1, keepdims=True)
    e = jnp.exp2(logit - m)
    any_valid = (q0 + lax.broadcasted_iota(jnp.int32, (tq, 1), 0) >= CMP_BLOCK - 1).astype(F32)
    p_all = e * (jnp.concatenate([any_valid] * NSA_HEADS, axis=0) / jnp.sum(e, axis=-1, keepdims=True))
    o_all = _dot(p_all.astype(BF16), vc)
    gated = [[gates[:, h * 3:h * 3 + 1] * o_all[h * tq:(h + 1) * tq]
              for h in range(g * NSA_REP, (g + 1) * NSA_REP)] for g in range(NSA_GROUPS)]
    for g in range(NSA_GROUPS):
        psum = functools.reduce(lambda a, b: a + b,
                                [p_all[h * tq:(h + 1) * tq] for h in range(g * NSA_REP, (g + 1) * NSA_REP)])
        hi = psum.astype(BF16)
        lo = (psum - hi.astype(F32)).astype(BF16)
        p_slc = _dot_nt(matt_ref[...], hi) + _dot_nt(matt_ref[...], lo)
        score = jnp.where(forced, BIG, jnp.where(blk_valid, p_slc, NEG_INF))
        sel = jnp.zeros((n_slc, tq), F32)
        for _ in range(min(SLC_TOPK, n_slc)):
            mx = jnp.max(score, axis=0, keepdims=True)
            idx = jnp.min(jnp.where(score == mx, jrow, float(n_slc)), axis=0, keepdims=True)
            hit = jrow == idx
            sel = jnp.where(hit, 1.0, sel)
            score = jnp.where(hit, -3.0e38, score)
        selm1 = (sel - 1.0).astype(BF16)
        if n_slc < LANES:
            selm1 = jnp.concatenate([selm1, jnp.zeros((LANES - n_slc, tq), BF16)], axis=0)
        sel_ref[0, g] = _dot_nt(eye, selm1).astype(sel_ref.dtype)
    for r in range(NSA_REP):
        ocmp_ref[0, :, r * LANES:(r + 1) * LANES] = jnp.where(lane < HEAD_DIM, gated[0][r], gated[1][r])


def _cmp_attention(tab, proj3, cmp_kv, gates3, matt, n_slc):
    B, S, _ = proj3.shape
    ncp = cmp_kv.shape[2]
    tq = CMP_TQ
    return pl.pallas_call(
        functools.partial(_cmp_kernel, n_slc=n_slc),
        grid=(S // tq, B),
        in_specs=[pl.BlockSpec(memory_space=pltpu.SMEM),
                  pl.BlockSpec((1, tq, 4 * LANES), lambda i, b: (b, i, 0)),
                  pl.BlockSpec((1, 1, ncp, LANES), lambda i, b: (0, b, 0, 0)),
                  pl.BlockSpec((1, 1, ncp, LANES), lambda i, b: (1, b, 0, 0)),
                  pl.BlockSpec((1, tq, LANES), lambda i, b: (b, i, 0)),
                  pl.BlockSpec((n_slc, ncp), lambda i, b: (0, 0))],
        out_specs=[pl.BlockSpec((1, tq, 4 * LANES), lambda i, b: (b, i, 0)),
                   pl.BlockSpec((1, NSA_GROUPS, tq, LANES), lambda i, b: (b, 0, i, 0))],
        out_shape=[jax.ShapeDtypeStruct((B, S, 4 * LANES), F32),
                   jax.ShapeDtypeStruct((B, NSA_GROUPS, S, LANES), BF16)],
        scratch_shapes=[pltpu.VMEM((NSA_HEADS, tq, LANES), F32), pltpu.VMEM((NSA_HEADS * tq, ncp), F32)],
        compiler_params=_params("arbitrary", "arbitrary"),
        name="cmp",
    )(tab, proj3, cmp_kv, cmp_kv, gates3, matt)


class _OnlineSoftmax:
    def __init__(self, score, value, m_ref, l_ref, acc_ref, sub_keys, init=True):
        self.score, self.value, self.sub_keys = score, value, sub_keys
        self.m_ref, self.l_ref, self.acc_ref = m_ref, l_ref, acc_ref
        if init:
            m_ref[...] = jnp.full(m_ref.shape, NEG_INF, F32)
            l_ref[...] = jnp.zeros(l_ref.shape, F32)
            acc_ref[...] = jnp.zeros(acc_ref.shape, F32)

    def step(self, k0, width, bias, first=False):
        m_ref, l_ref, acc_ref = self.m_ref, self.l_ref, self.acc_ref
        for j in range(0, width, self.sub_keys):
            w = min(self.sub_keys, width - j)
            s = self.score(k0 + j, w)
            cols = []
            for c in range(0, w, LANES):
                b = None if bias is None else bias(j + c)
                cols.append(s[:, c:c + LANES] if b is None else s[:, c:c + LANES] + b)
            row_max = jnp.max(functools.reduce(jnp.maximum, cols), axis=-1, keepdims=True)
            assign = first and j == 0
            m_old = jnp.full(m_ref.shape, NEG_INF, F32) if assign else m_ref[...]
            m_new = jnp.maximum(m_old, row_max)
            ps = [jnp.exp2(col - m_new) for col in cols]
            p_sum = functools.reduce(lambda a, b: a + b, ps)
            pv = _dot(jnp.concatenate([p.astype(BF16) for p in ps], axis=1), self.value(k0 + j, w))
            if assign:
                l_ref[...] = p_sum
                acc_ref[...] = pv
            else:
                alpha = jnp.exp2(m_old - m_new)
                l_ref[...] = alpha * l_ref[...] + p_sum
                acc_ref[...] = alpha * acc_ref[...] + pv
            m_ref[...] = m_new

    def result(self):
        return self.acc_ref[...] / jnp.sum(self.l_ref[...], axis=-1, keepdims=True)


def _causal_far_loop(fn, qt, tq, far_keys):
    far_tiles = far_keys // tq

    def far_body(i, carry):
        fn(i * far_keys, far_keys, None)
        return carry

    lax.fori_loop(0, lax.div(jnp.maximum(qt - 1, 0), far_tiles), far_body, 0)


def _causal_segments(fn, qt, tq, far_keys, near_bias):
    n_far = jnp.maximum(qt - 1, 0)
    far_tiles = far_keys // tq
    _causal_far_loop(fn, qt, tq, far_keys)
    for rem in range(far_tiles):
        lead = rem * tq
        bias = (lambda c, lead=lead: None if c < lead else near_bias(c - lead))
        pl.when((qt >= 1) & (lax.rem(n_far, far_tiles) == rem))(
            functools.partial(fn, (qt - 1 - rem) * tq, lead + 2 * tq, bias))
    pl.when(qt == 0)(functools.partial(fn, 0, tq, lambda c: near_bias(tq + c)))


def _key_slice(k0, width, tq):
    return pl.ds(k0 if isinstance(k0, int) else pl.multiple_of(k0, tq), width)


def _rows(ref, k0, width, tq):
    return ref[0, _key_slice(k0, width, tq), :]


def _nsa_kernel(tab_ref, q_ref, ks_ref, vs_ref, kw_ref, vw_ref, sel_ref, et_ref, gate_ref, ocmp_ref,
                y_ref, tb_ref, m_ref, l_ref, acc_ref, wm_ref, wl_ref, wacc_ref):
    tq = q_ref.shape[1]
    n_win = WINDOW // tq
    qt = pl.program_id(1)

    @pl.when((pl.program_id(0) == 0) & (qt == 0))
    def _():
        ti = lax.broadcasted_iota(jnp.int32, (tq, tq), 0)
        ki = lax.broadcasted_iota(jnp.int32, (tq, tq), 1)
        for g in range(NSA_GROUPS):
            for r in range(NSA_REP):
                head = g * NSA_REP + r
                far_bias = tab_ref[REL_BUCKETS - 1, head]
                rows = slice(head * tq, (head + 1) * tq)
                tb_ref[rows, 0:tq] = jnp.where(ti < ki, 0.0, NEG_INF)
                for j in range(2, n_win):
                    tb_ref[rows, (n_win - j) * tq:(n_win - j + 1) * tq] = jnp.zeros((tq, tq), F32)
                table = _rel_bias(lax.broadcasted_iota(jnp.int32, (tq, LANES), 1), tab_ref, head, far_bias)
                tb_ref[rows, (n_win - 1) * tq:n_win * tq] = _bias_lookup(table, ti - ki + tq)
                tb_ref[rows, n_win * tq:(n_win + 1) * tq] = jnp.where(
                    ti >= ki, _bias_lookup(table, ti - ki), NEG_INF)

    lane = lax.broadcasted_iota(jnp.int32, (tq, LANES), 1)
    gates = jax.nn.sigmoid(gate_ref[0])

    q_parts, sel_parts = [], []
    for g in range(NSA_GROUPS):
        lane_g = (lane >= HEAD_DIM * g) & (lane < HEAD_DIM * (g + 1))
        for r in range(NSA_REP):
            qb = q_ref[0, :, r * LANES:(r + 1) * LANES]
            q_parts.append(jnp.where(lane_g, qb, jnp.zeros_like(qb)))
            sel_parts.append(sel_ref[0, g])
    q_all = jnp.concatenate(q_parts, axis=0)
    qs_all = jnp.concatenate([q_all, jnp.concatenate(sel_parts, axis=0)], axis=1)

    def bias_from(col0):
        return lambda c: tb_ref[:, col0 + c:col0 + c + LANES]

    def slc_score(k0, width):
        et = et_ref[_key_slice(k0, width, tq), :]
        return _dot_nt(qs_all, jnp.concatenate([_rows(ks_ref, k0, width, tq), et], axis=1))

    def win_score(k0, width):
        return _dot_nt(q_all, _rows(kw_ref, k0, width, tq))

    slc = _OnlineSoftmax(slc_score, lambda k0, width: _rows(vs_ref, k0, width, tq),
                         m_ref, l_ref, acc_ref, NSA_ONLINE_KEYS)
    win = _OnlineSoftmax(win_score, lambda k0, width: _rows(vw_ref, k0, width, tq),
                         wm_ref, wl_ref, wacc_ref, NSA_ONLINE_KEYS, init=False)

    def near_and_window(slc_tiles, win_tiles):
        slc_lead = (slc_tiles - 2) * tq if slc_tiles >= 2 else 0
        slc_bias = bias_from((n_win - 1) * tq) if slc_tiles >= 2 else bias_from(n_win * tq)
        slc.step((qt + 1 - slc_tiles) * tq, slc_tiles * tq,
                 lambda c: None if c < slc_lead else slc_bias(c - slc_lead))
        win.step((qt + 1 - win_tiles) * tq, win_tiles * tq, bias_from((n_win + 1 - win_tiles) * tq), first=True)

    _causal_far_loop(slc.step, qt, tq, NSA_FAR_KEYS)
    far_tiles = NSA_FAR_KEYS // tq
    left_over = lax.rem(jnp.maximum(qt - 1, 0), far_tiles)
    for n in range(n_win):
        pl.when(qt == n)(functools.partial(near_and_window, n + 1, n + 1))
    for rem in range(far_tiles):
        pl.when((qt >= n_win) & (left_over == rem))(functools.partial(near_and_window, rem + 2, n_win + 1))
    o_slc = slc.result()
    o_win = win.result()

    for r in range(NSA_REP):
        ys = []
        for g in range(NSA_GROUPS):
            head = g * NSA_REP + r
            rows = slice(head * tq, (head + 1) * tq)
            ys.append(gates[:, head * 3 + 1:head * 3 + 2] * o_slc[rows]
                      + gates[:, head * 3 + 2:head * 3 + 3] * o_win[rows])
        cols = slice(r * LANES, (r + 1) * LANES)
        y_ref[0, :, cols] = (ocmp_ref[0, :, cols] + jnp.where(lane < HEAD_DIM, ys[0], ys[1])).astype(y_ref.dtype)


def _nsa_attention(tab, proj3, sel, et, gates3, ocmp, col_blocks):
    B, S, _ = proj3.shape
    tq = NSA_TQ
    assert WINDOW % tq == 0 and WINDOW // tq >= 2 and S % tq == 0
    n_win = WINDOW // tq
    ks_c, vs_c, kw_c, vw_c = col_blocks
    rows = NSA_HEADS * tq

    def kv_spec(c):
        return pl.BlockSpec((1, S, LANES), lambda b, i: (b, 0, c))

    return pl.pallas_call(
        _nsa_kernel,
        grid=(B, S // tq),
        in_specs=[pl.BlockSpec(memory_space=pltpu.SMEM),
                  pl.BlockSpec((1, tq, 4 * LANES), lambda b, i: (b, i, 0)),
                  kv_spec(ks_c), kv_spec(vs_c), kv_spec(kw_c), kv_spec(vw_c),
                  pl.BlockSpec((1, NSA_GROUPS, tq, LANES), lambda b, i: (b, 0, i, 0)),
                  pl.BlockSpec((S, LANES), lambda b, i: (0, 0)),
                  pl.BlockSpec((1, tq, LANES), lambda b, i: (b, i, 0)),
                  pl.BlockSpec((1, tq, 4 * LANES), lambda b, i: (b, i, 0))],
        out_specs=pl.BlockSpec((1, tq, 4 * LANES), lambda b, i: (b, i, 0)),
        out_shape=jax.ShapeDtypeStruct((B, S, 4 * LANES), BF16),
        scratch_shapes=[pltpu.VMEM((rows, (n_win + 1) * tq), F32)] + [pltpu.VMEM((rows, LANES), F32)] * 6,
        compiler_params=_params("arbitrary", "arbitrary"),
        name="nsa",
    )(tab, proj3, proj3, proj3, proj3, proj3, sel, et, gates3, ocmp)


def _diff_kernel(tab_ref, q_ref, k_ref, v_ref, lam_ref, g_ref, y_ref, tb_ref, m_ref, l_ref, acc_ref,
                 *, lambda_init):
    tq = q_ref.shape[1]
    h = pl.program_id(0)
    qt = pl.program_id(2)
    lane = lax.broadcasted_iota(jnp.int32, (tq, LANES), 1)

    @pl.when((pl.program_id(1) == 0) & (qt == 0))
    def _():
        ti = lax.broadcasted_iota(jnp.int32, (tq, tq), 0)
        ki = lax.broadcasted_iota(jnp.int32, (tq, tq), 1)
        head = NSA_HEADS + h
        table = _rel_bias(lane, tab_ref, head, tab_ref[REL_BUCKETS - 1, head])
        tb_ref[:, 0:tq] = _bias_lookup(table, ti - ki + tq)
        tb_ref[:, tq:2 * tq] = jnp.where(ti >= ki, _bias_lookup(table, ti - ki), NEG_INF)

    qb = q_ref[0]
    zero = jnp.zeros_like(qb)
    q2 = jnp.concatenate([jnp.where(lane < HEAD_DIM, qb, zero), jnp.where(lane >= HEAD_DIM, qb, zero)], axis=0)

    def score(k0, width):
        return _dot_nt(q2, _rows(k_ref, k0, width, tq))

    def near_bias(c):
        tb = tb_ref[:, c:c + LANES]
        return jnp.concatenate([tb, tb], axis=0)

    attn = _OnlineSoftmax(score, lambda k0, width: _rows(v_ref, k0, width, tq), m_ref, l_ref, acc_ref,
                          DIFF_ONLINE_KEYS)
    _causal_segments(attn.step, qt, tq, DIFF_FAR_KEYS, near_bias)
    a = attn.result()
    lq1, lk1, lq2, lk2 = lam_ref[0:1, :], lam_ref[1:2, :], lam_ref[2:3, :], lam_ref[3:4, :]
    lam = (jnp.exp(jnp.sum(lq1 * lk1, axis=-1, keepdims=True))
           - jnp.exp(jnp.sum(lq2 * lk2, axis=-1, keepdims=True)) + lambda_init)
    o = a[:tq] - lam * a[tq:]
    o = o * lax.rsqrt(jnp.mean(o * o, axis=-1, keepdims=True) + LN_EPS) * g_ref[...]
    y_ref[0] = (o * (1.0 - lambda_init)).astype(y_ref.dtype)


def _diff_attention(tab, proj3, lam4, subln_g, col_blocks, lambda_init):
    B, S, _ = proj3.shape
    tq = min(DIFF_TQ, S)
    q_c, k_c, v_c = col_blocks
    return pl.pallas_call(
        functools.partial(_diff_kernel, lambda_init=lambda_init),
        grid=(DIFF_HEADS, B, S // tq),
        in_specs=[pl.BlockSpec(memory_space=pltpu.SMEM),
                  pl.BlockSpec((1, tq, LANES), lambda h, b, i: (b, i, q_c + h)),
                  pl.BlockSpec((1, S, LANES), lambda h, b, i: (b, 0, k_c + h)),
                  pl.BlockSpec((1, S, LANES), lambda h, b, i: (b, 0, v_c + h)),
                  pl.BlockSpec((SUBLANES, HEAD_DIM), lambda h, b, i: (0, 0)),
                  pl.BlockSpec((1, LANES), lambda h, b, i: (0, 0))],
        out_specs=pl.BlockSpec((1, tq, LANES), lambda h, b, i: (b, i, h)),
        out_shape=jax.ShapeDtypeStruct((B, S, DIFF_HEADS * LANES), BF16),
        scratch_shapes=[pltpu.VMEM((tq, 2 * tq), F32),
                        pltpu.VMEM((2 * tq, LANES), F32),
                        pltpu.VMEM((2 * tq, LANES), F32),
                        pltpu.VMEM((2 * tq, LANES), F32)],
        compiler_params=_params("arbitrary", "arbitrary", "arbitrary"),
        name="diff",
    )(tab, proj3, proj3, proj3, lam4, subln_g)


def _merge_kernel(x_ref, yn_ref, yd_ref, wgn_ref, wgd_ref, wbn_ref, wbd_ref, wo_ref, g_ref, b_ref, o_ref,
                  *, alpha):
    for r0 in range(0, x_ref.shape[0], ROW_TILE):
        rows = slice(r0, r0 + ROW_TILE)
        x = x_ref[rows, :]
        xb = x.astype(BF16)
        merged = (jax.nn.sigmoid(_dot(xb, wgn_ref[...])) * _dot(yn_ref[rows, :], wbn_ref[...])
                  + jax.nn.sigmoid(_dot(xb, wgd_ref[...])) * _dot(yd_ref[rows, :], wbd_ref[...]))
        z = alpha * x + _dot(merged.astype(BF16), wo_ref[...])
        o_ref[rows, :] = _layer_norm(z, g_ref[...], b_ref[...])


def _merge(x2d, y_nsa, y_diff, wgn, wgd, wbn, wbd, wo, ln_g, ln_b, alpha):
    T, D = x2d.shape
    const = lambda shape: pl.BlockSpec(shape, lambda i: (0, 0))
    return pl.pallas_call(
        functools.partial(_merge_kernel, alpha=alpha),
        grid=(T // WIDE_TILE,),
        in_specs=[pl.BlockSpec((WIDE_TILE, D), lambda i: (i, 0)),
                  pl.BlockSpec((WIDE_TILE, y_nsa.shape[1]), lambda i: (i, 0)),
                  pl.BlockSpec((WIDE_TILE, y_diff.shape[1]), lambda i: (i, 0)),
                  const(wgn.shape), const(wgd.shape), const(wbn.shape), const(wbd.shape), const(wo.shape),
                  const((1, D)), const((1, D))],
        out_specs=pl.BlockSpec((WIDE_TILE, D), lambda i: (i, 0)),
        out_shape=jax.ShapeDtypeStruct((T, D), F32),
        compiler_params=_params("arbitrary"),
        name="merge",
    )(x2d, y_nsa, y_diff, wgn, wgd, wbn, wbd, wo, ln_g, ln_b)


def _ffn_kernel(x_ref, halo_ref, p_ref, w_ref, cw_ref, cb_ref, wd_ref, g_ref, b_ref, wpg_ref, wpp_ref,
                o_ref, acc_ref, *, alpha, tiles_per_seq):
    d_ff = wd_ref.shape[0]
    keep = (pl.program_id(0) % tiles_per_seq != 0).astype(F32)
    for r0 in range(0, x_ref.shape[0], ROW_TILE):
        rows = slice(r0, r0 + ROW_TILE)
        x = x_ref[rows, :]
        xb = x.astype(BF16)
        hb = (halo_ref[...] if r0 == 0 else x_ref[r0 - HALO:r0, :]).astype(BF16)
        for c in range(0, d_ff, FF_CHUNK):
            width = min(FF_CHUNK, d_ff - c)
            cols = slice(c, c + width)
            wg = w_ref[:, cols]
            gm = _dot(xb, wg)
            gh = _dot(hb, wg)
            if r0 == 0:
                gh = gh * keep
            um = _dot(xb, w_ref[:, d_ff + c:d_ff + c + width])
            gext = jnp.concatenate([gh, gm], axis=0)
            g1 = pltpu.roll(gext, 1, 0)[HALO:]
            g2 = pltpu.roll(gext, 2, 0)[HALO:]
            conv = cb_ref[:, cols] + cw_ref[0:1, cols] * g2 + cw_ref[1:2, cols] * g1 + cw_ref[2:3, cols] * gm
            acc_ref[rows, cols] = (_gelu(conv) * um).astype(BF16)
        x2 = _layer_norm(alpha * x + _dot(acc_ref[rows, :], wd_ref[...]), g_ref[...], b_ref[...])
        gate = jax.nn.sigmoid(_dot(x2.astype(BF16), wpg_ref[...]))
        o_ref[rows, :] = x2 + gate * _dot(p_ref[rows, :].astype(BF16), wpp_ref[...])


def _ffn(x1, p2d, w_in, cw, cb, wd, ln_g, ln_b, wpg, wpp, alpha, seq):
    T, D = x1.shape
    tm = WIDE_TILE
    assert seq % tm == 0 and wd.shape[0] % LANES == 0
    hb = tm // HALO
    const = lambda shape: pl.BlockSpec(shape, lambda i: (0, 0))
    return pl.pallas_call(
        functools.partial(_ffn_kernel, alpha=alpha, tiles_per_seq=seq // tm),
        grid=(T // tm,),
        in_specs=[pl.BlockSpec((tm, D), lambda i: (i, 0)),
                  pl.BlockSpec((HALO, D), lambda i: (jnp.maximum(i * hb - 1, 0), 0)),
                  pl.BlockSpec((tm, p2d.shape[1]), lambda i: (i, 0)),
                  const(w_in.shape), const(cw.shape), const(cb.shape), const(wd.shape),
                  const((1, D)), const((1, D)), const(wpg.shape), const(wpp.shape)],
        out_specs=pl.BlockSpec((tm, D), lambda i: (i, 0)),
        out_shape=jax.ShapeDtypeStruct((T, D), F32),
        scratch_shapes=[pltpu.VMEM((tm, wd.shape[0]), BF16)],
        compiler_params=_params("arbitrary"),
        name="ffn",
    )(x1, x1, p2d, w_in, cw, cb, wd, ln_g, ln_b, wpg, wpp)


def _slc_from_cmp_t(ncp, n_slc):
    ratio = SLC_BLOCK // CMP_STRIDE
    span = CMP_BLOCK // CMP_STRIDE
    mat = np.zeros((n_slc, ncp), np.float32)
    for j in range(n_slc):
        for m in range(ratio):
            for n in range(span):
                i = ratio * j + m - n
                if 0 <= i < ncp - 1:
                    mat[j, i] += 1.0
    return mat


def _layer(x, p_l, w_in, pe_k, w1_k, w2_k, pe_v, w1_v, w2_v, lq1, lk1, lq2, lk2, subln_g, w_bn, w_bd, w_out,
           ln1_g, ln1_b, w_ffn_in, conv_w, conv_b, w_down, ln2_g, ln2_b, w_pp, w_pg, tab, lambda_init,
           alpha):
    B, S, D = x.shape
    T = B * S
    ncp = S // CMP_STRIDE
    n_slc = S // SLC_BLOCK
    q_w = NSA_HEADS * HEAD_DIM
    kv_w = NSA_GROUPS * HEAD_DIM
    dqk_w = DIFF_HEADS * 2 * HEAD_DIM
    sizes = (q_w,) + (kv_w,) * 6 + (NSA_HEADS * 3, dqk_w, dqk_w, dqk_w, D, D)
    offs = np.concatenate([[0], np.cumsum(sizes)])
    col = lambda i: w_in[:, int(offs[i]):int(offs[i + 1])]
    scale = HEAD_DIM ** -0.5 * LOG2E

    n_idx = np.arange(q_w)
    perm = (NSA_REP * ((n_idx % LANES) // HEAD_DIM) + n_idx // LANES) * HEAD_DIM + n_idx % HEAD_DIM
    w_main = jnp.concatenate([col(0)[:, perm] * scale] + [col(i) for i in range(3, 7)]
                             + [col(8) * scale, col(9), col(10)], axis=1).astype(BF16)
    w_cmp = jnp.concatenate([col(1), col(2)], axis=1).astype(BF16)
    w_gate = jnp.pad(col(7), ((0, 0), (0, LANES - NSA_HEADS * 3))).astype(BF16)
    x2d = x.reshape(T, D)
    proj, k_cmp, v_cmp, gates = _proj(x2d, w_main, w_cmp, w_gate)
    proj3 = proj.reshape(B, S, proj.shape[1])
    gates3 = gates.reshape(B, S, LANES)
    c_kslc, c_vslc, c_kwin, c_vwin = (q_w // LANES + i for i in range(4))
    c_dq = q_w // LANES + 4
    c_dk = c_dq + DIFF_HEADS
    c_dv = c_dk + DIFF_HEADS

    cw = CMP_STRIDE * kv_w
    rep = lambda a: jnp.broadcast_to(a.reshape(2, 2, CMP_STRIDE, 1, HEAD_DIM, -1),
                                     (2, 2, CMP_STRIDE, NSA_GROUPS, HEAD_DIM, a.shape[-1]))
    pe = rep(jnp.stack([pe_k, pe_v])[..., None]).reshape(2, 2, 1, cw)
    w1 = rep(jnp.stack([w1_k, w1_v])).reshape(2, 2, cw, CMP_HIDDEN).astype(BF16)
    w2 = jnp.stack([w2_k, w2_v])
    w2p = jnp.stack([jnp.pad(w2, ((0, 0), (0, 0), (g * HEAD_DIM, LANES - (g + 1) * HEAD_DIM)))
                     for g in range(NSA_GROUPS)], axis=1).astype(BF16)
    cmp_kv = _compress(k_cmp.reshape(B, ncp, cw), v_cmp.reshape(B, ncp, cw), pe, w1, w2p)

    matt = jnp.asarray(_slc_from_cmp_t(ncp, n_slc), BF16)
    ocmp, sel = _cmp_attention(tab, proj3, cmp_kv, gates3, matt, n_slc)
    et_np = np.zeros((S, LANES), np.float32)
    et_np[np.arange(S), np.arange(S) // SLC_BLOCK] = 2.0 ** MASK_EXP
    y_nsa = _nsa_attention(tab, proj3, sel, jnp.asarray(et_np, BF16), gates3, ocmp,
                           (c_kslc, c_vslc, c_kwin, c_vwin))

    lam4 = jnp.pad(jnp.stack([lq1, lk1, lq2, lk2]), ((0, SUBLANES - 4), (0, 0)))
    y_diff = _diff_attention(tab, proj3, lam4, subln_g.reshape(1, LANES), (c_dq, c_dk, c_dv), lambda_init)

    x1 = _merge(x2d, y_nsa.reshape(T, q_w), y_diff.reshape(T, dqk_w),
                col(11).astype(BF16), col(12).astype(BF16), w_bn[perm].astype(BF16), w_bd.astype(BF16),
                w_out.astype(BF16), ln1_g.reshape(1, D), ln1_b.reshape(1, D), alpha)

    cw = jnp.pad(conv_w, ((0, SUBLANES - CONV_WIDTH), (0, 0)))
    out = _ffn(x1, p_l.reshape(T, p_l.shape[-1]), w_ffn_in.astype(BF16), cw, conv_b.reshape(1, -1),
               w_down.astype(BF16), ln2_g.reshape(1, D), ln2_b.reshape(1, D),
               w_pg.astype(BF16), w_pp.astype(BF16), alpha, S)
    return out.reshape(B, S, D)


def kernel(x, p, w_in, nsa_cmp_pe_k, nsa_cmp_w1_k, nsa_cmp_w2_k, nsa_cmp_pe_v, nsa_cmp_w1_v, nsa_cmp_w2_v, diff_lambda_q1, diff_lambda_k1, diff_lambda_q2, diff_lambda_k2, diff_subln_g, w_branch_nsa, w_branch_diff, w_out, ln1_g, ln1_b, w_ffn_in, ffn_conv_w, ffn_conv_b, w_ffn_down, ln2_g, ln2_b, w_ple_proj, w_ple_gate, rel_bias_table):
    depth = w_in.shape[0]
    alpha = (2.0 * depth) ** 0.25
    for l in range(depth):
        lambda_init = 0.8 - 0.6 * math.exp(-0.3 * l)
        x = _layer(x, p[l], w_in[l], nsa_cmp_pe_k[l], nsa_cmp_w1_k[l], nsa_cmp_w2_k[l], nsa_cmp_pe_v[l],
                   nsa_cmp_w1_v[l], nsa_cmp_w2_v[l], diff_lambda_q1[l], diff_lambda_k1[l], diff_lambda_q2[l],
                   diff_lambda_k2[l], diff_subln_g[l], w_branch_nsa[l], w_branch_diff[l], w_out[l], ln1_g[l],
                   ln1_b[l], w_ffn_in[l], ffn_conv_w[l], ffn_conv_b[l], w_ffn_down[l], ln2_g[l], ln2_b[l],
                   w_ple_proj[l], w_ple_gate[l], rel_bias_table, lambda_init, alpha)
    return x
```

```python
import functools
import math

import jax
import jax.numpy as jnp
import numpy as np
from jax import lax
from jax.experimental import pallas as pl
from jax.experimental.pallas import tpu as pltpu

F32 = jnp.float32
BF16 = jnp.bfloat16

NSA_HEADS = 8
NSA_GROUPS = 2
NSA_REP = NSA_HEADS // NSA_GROUPS
HEAD_DIM = 64
CMP_BLOCK = 32
CMP_STRIDE = 16
CMP_HIDDEN = 256
SLC_BLOCK = 64
SLC_TOPK = 16
SLC_LOCAL = 2
WINDOW = 512
DIFF_HEADS = 4
REL_BUCKETS = 32
REL_MAX_EXACT = 16
REL_MAX_DIST = 128
D_FF = 2816
CONV_WIDTH = 3
LN_EPS = 1e-5
NEG_INF = -1e30
BIG = 1e30
LOG2E = math.log2(math.e)
MASK_EXP = 100

LANES = 128
SUBLANES = 8
VMEM_LIMIT = 56 * 1024 * 1024

CMP_TQ = 256
NSA_TQ = 256
DIFF_TQ = 512
NSA_FAR_KEYS = 512
DIFF_FAR_KEYS = 2048
NSA_ONLINE_KEYS = 512
DIFF_ONLINE_KEYS = 2048
ROW_TILE = 512
WIDE_TILE = 1024
FF_CHUNK = 512
HALO = 16


def _rel_breakpoints():
    n = np.arange(0, 4 * REL_MAX_DIST)
    large = REL_MAX_EXACT + (np.log(np.maximum(n, 1).astype(np.float32) / REL_MAX_EXACT)
                             / np.float32(math.log(REL_MAX_DIST / REL_MAX_EXACT))
                             * (REL_BUCKETS - REL_MAX_EXACT)).astype(np.int32)
    bucket = np.where(n < REL_MAX_EXACT, n, np.minimum(large, REL_BUCKETS - 1))
    assert np.all(np.diff(bucket) >= 0)
    return [int(np.argmax(bucket >= b)) for b in range(1, REL_BUCKETS)]


REL_BREAKS = _rel_breakpoints()


def _dot(a, b):
    return jnp.dot(a, b, preferred_element_type=F32)


def _dot_nt(a, b):
    return lax.dot_general(a, b, (((1,), (1,)), ((), ())), preferred_element_type=F32)


def _rel_bias(dist, tab_ref, head, shift):
    val = jnp.full(dist.shape, (tab_ref[0, head] - shift) * LOG2E, F32)
    for b, brk in enumerate(REL_BREAKS, start=1):
        val = jnp.where(dist >= brk, (tab_ref[b, head] - shift) * LOG2E, val)
    return val


def _bias_lookup(table, dist):
    assert REL_BREAKS[-1] < LANES
    near = jnp.clip(dist, 0, LANES - 1)
    cols = [jnp.take_along_axis(table, near[:, c:c + LANES], axis=1) for c in range(0, dist.shape[1], LANES)]
    return jnp.concatenate(cols, axis=1)


def _gelu(x):
    c = math.sqrt(2.0 / math.pi)
    half = 0.5 * x
    return half + half * jnp.tanh(x * (c + (c * 0.044715) * (x * x)))


def _layer_norm(z, g, b):
    mu = jnp.mean(z, axis=-1, keepdims=True)
    zc = z - mu
    var = jnp.mean(zc * zc, axis=-1, keepdims=True)
    return zc * lax.rsqrt(var + LN_EPS) * g + b


def _params(*sem):
    return pltpu.CompilerParams(dimension_semantics=sem, vmem_limit_bytes=VMEM_LIMIT)


def _proj_kernel(x_ref, wm_ref, wc_ref, wg_ref, om_ref, ok_ref, ov_ref, og_ref, k_scr, v_scr):
    xb = x_ref[...].astype(BF16)
    n = wm_ref.shape[1]
    for c in range(0, n, 2 * LANES):
        w = min(2 * LANES, n - c)
        om_ref[:, c:c + w] = _dot(xb, wm_ref[:, c:c + w]).astype(om_ref.dtype)
    kv = _dot(xb, wc_ref[...])
    k_scr[...] = kv[:, :LANES]
    v_scr[...] = kv[:, LANES:]
    chunks = k_scr.shape[0] // CMP_STRIDE
    for t in range(CMP_STRIDE):
        token_t = pl.ds(t, chunks, stride=CMP_STRIDE)
        ok_ref[:, t * LANES:(t + 1) * LANES] = k_scr[token_t, :].astype(ok_ref.dtype)
        ov_ref[:, t * LANES:(t + 1) * LANES] = v_scr[token_t, :].astype(ov_ref.dtype)
    og_ref[...] = _dot(xb, wg_ref[...])


def _proj(x2d, w_main, w_cmp, w_gate):
    T, D = x2d.shape
    n = w_main.shape[1]
    row = lambda width: pl.BlockSpec((ROW_TILE, width), lambda i: (i, 0))
    const = lambda width: pl.BlockSpec((D, width), lambda i: (0, 0))
    chunk_rows = pl.BlockSpec((ROW_TILE // CMP_STRIDE, CMP_STRIDE * LANES), lambda i: (i, 0))
    return pl.pallas_call(
        _proj_kernel,
        grid=(T // ROW_TILE,),
        in_specs=[row(D), const(n), const(2 * LANES), const(LANES)],
        out_specs=[row(n), chunk_rows, chunk_rows, row(LANES)],
        out_shape=[jax.ShapeDtypeStruct((T, n), BF16),
                   jax.ShapeDtypeStruct((T // CMP_STRIDE, CMP_STRIDE * LANES), BF16),
                   jax.ShapeDtypeStruct((T // CMP_STRIDE, CMP_STRIDE * LANES), BF16),
                   jax.ShapeDtypeStruct((T, LANES), F32)],
        scratch_shapes=[pltpu.VMEM((ROW_TILE, LANES), F32), pltpu.VMEM((ROW_TILE, LANES), F32)],
        compiler_params=_params("arbitrary"),
        name="proj",
    )(x2d, w_main, w_cmp, w_gate)


def _compress_kernel(xk_ref, xv_ref, pe_ref, w1_ref, w2_ref, o_ref):
    ncp, cw = xk_ref.shape[1], xk_ref.shape[2]
    lane = lax.broadcasted_iota(jnp.int32, (ncp, cw), 1)
    group = jnp.bitwise_and(jnp.right_shift(lane, int(math.log2(HEAD_DIM))), NSA_GROUPS - 1)
    for s, x_ref in enumerate((xk_ref, xv_ref)):
        x = x_ref[0].astype(F32)
        xa = x + pe_ref[s, 0]
        xb = x + pe_ref[s, 1]
        acc = jnp.zeros((ncp, LANES), F32)
        for g in range(NSA_GROUPS):
            a = _dot(jnp.where(group == g, xa, 0.0).astype(BF16), w1_ref[s, 0])
            b = _dot(jnp.where(group == g, xb, 0.0).astype(BF16), w1_ref[s, 1])
            h = a + pltpu.roll(b, ncp - 1, 0)
            acc = acc + _dot(_gelu(h).astype(BF16), w2_ref[s, g])
        o_ref[s, 0] = acc.astype(o_ref.dtype)


def _compress(xk, xv, pe, w1, w2):
    B, ncp, cw = xk.shape
    x_spec = pl.BlockSpec((1, ncp, cw), lambda b: (b, 0, 0))
    const = lambda shape: pl.BlockSpec(shape, lambda b: (0, 0, 0, 0))
    return pl.pallas_call(
        _compress_kernel,
        grid=(B,),
        in_specs=[x_spec, x_spec, const(pe.shape), const(w1.shape), const(w2.shape)],
        out_specs=pl.BlockSpec((2, 1, ncp, LANES), lambda b: (0, b, 0, 0)),
        out_shape=jax.ShapeDtypeStruct((2, B, ncp, LANES), BF16),
        compiler_params=_params("arbitrary"),
        name="compress",
    )(xk, xv, pe, w1, w2)


def _cmp_kernel(tab_ref, q_ref, kc_ref, vc_ref, gate_ref, matt_ref, ocmp_ref, sel_ref, bias_ref, ball_ref,
                *, n_slc):
    tq = q_ref.shape[1]
    ncp = kc_ref.shape[2]
    q0 = pl.program_id(0) * tq
    lane = lax.broadcasted_iota(jnp.int32, (tq, LANES), 1)

    @pl.when((pl.program_id(0) == 0) & (pl.program_id(1) == 0))
    def _():
        for head in range(NSA_HEADS):
            bias_ref[head] = _rel_bias(lane, tab_ref, head, 0.0)

    @pl.when(pl.program_id(1) == 0)
    def _():
        t_idx = q0 + lax.broadcasted_iota(jnp.int32, (tq, ncp), 0)
        c_idx = lax.broadcasted_iota(jnp.int32, (tq, ncp), 1)
        dist = t_idx - (c_idx * CMP_STRIDE + (CMP_BLOCK - 1))
        for head in range(NSA_HEADS):
            ball_ref[head * tq:(head + 1) * tq, :] = jnp.where(dist >= 0, _bias_lookup(bias_ref[head], dist), NEG_INF)

    gates = jax.nn.sigmoid(gate_ref[0])
    eye = (lax.broadcasted_iota(jnp.int32, (tq, tq), 0)
           == lax.broadcasted_iota(jnp.int32, (tq, tq), 1)).astype(BF16)
    kc = kc_ref[0, 0]
    vc = vc_ref[0, 0]
    jrow_i = lax.broadcasted_iota(jnp.int32, (n_slc, tq), 0)
    cur = jnp.right_shift(q0 + lax.broadcasted_iota(jnp.int32, (n_slc, tq), 1), int(math.log2(SLC_BLOCK)))
    forced = (jrow_i == 0) | ((cur - jrow_i >= 0) & (cur - jrow_i < SLC_LOCAL))
    blk_valid = jrow_i <= cur
    jrow = jrow_i.astype(F32)
    q_parts = []
    for g in range(NSA_GROUPS):
        lane_g = (lane >= HEAD_DIM * g) & (lane < HEAD_DIM * (g + 1))
        for r in range(NSA_REP):
            qb = q_ref[0, :, r * LANES:(r + 1) * LANES]
            q_parts.append(jnp.where(lane_g, qb, jnp.zeros_like(qb)))
    logit = _dot_nt(jnp.concatenate(q_parts, axis=0), kc) + ball_ref[...]
    m = jnp.max(logit, axis=-1, keepdims=True)
    e = jnp.exp2(logit - m)
    any_valid = (q0 + lax.broadcasted_iota(jnp.int32, (tq, 1), 0) >= CMP_BLOCK - 1).astype(F32)
    p_all = e * (jnp.concatenate([any_valid] * NSA_HEADS, axis=0) / jnp.sum(e, axis=-1, keepdims=True))
    o_all = _dot(p_all.astype(BF16), vc)
    gated = [[gates[:, h * 3:h * 3 + 1] * o_all[h * tq:(h + 1) * tq]
              for h in range(g * NSA_REP, (g + 1) * NSA_REP)] for g in range(NSA_GROUPS)]
    for g in range(NSA_GROUPS):
        psum = functools.reduce(lambda a, b: a + b,
                                [p_all[h * tq:(h + 1) * tq] for h in range(g * NSA_REP, (g + 1) * NSA_REP)])
        hi = psum.astype(BF16)
        lo = (psum - hi.astype(F32)).astype(BF16)
        p_slc = _dot_nt(matt_ref[...], hi) + _dot_nt(matt_ref[...], lo)
        score = jnp.where(forced, BIG, jnp.where(blk_valid, p_slc, NEG_INF))
        sel = jnp.zeros((n_slc, tq), F32)
        for _ in range(min(SLC_TOPK, n_slc)):
            mx = jnp.max(score, axis=0, keepdims=True)
            idx = jnp.min(jnp.where(score == mx, jrow, float(n_slc)), axis=0, keepdims=True)
            hit = jrow == idx
            sel = jnp.where(hit, 1.0, sel)
            score = jnp.where(hit, -3.0e38, score)
        selm1 = (sel - 1.0).astype(BF16)
        if n_slc < LANES:
            selm1 = jnp.concatenate([selm1, jnp.zeros((LANES - n_slc, tq), BF16)], axis=0)
        sel_ref[0, g] = _dot_nt(eye, selm1).astype(sel_ref.dtype)
    for r in range(NSA_REP):
        ocmp_ref[0, :, r * LANES:(r + 1) * LANES] = jnp.where(lane < HEAD_DIM, gated[0][r], gated[1][r])


def _cmp_attention(tab, proj3, cmp_kv, gates3, matt, n_slc):
    B, S, _ = proj3.shape
    ncp = cmp_kv.shape[2]
    tq = CMP_TQ
    return pl.pallas_call(
        functools.partial(_cmp_kernel, n_slc=n_slc),
        grid=(S // tq, B),
        in_specs=[pl.BlockSpec(memory_space=pltpu.SMEM),
                  pl.BlockSpec((1, tq, 4 * LANES), lambda i, b: (b, i, 0)),
                  pl.BlockSpec((1, 1, ncp, LANES), lambda i, b: (0, b, 0, 0)),
                  pl.BlockSpec((1, 1, ncp, LANES), lambda i, b: (1, b, 0, 0)),
                  pl.BlockSpec((1, tq, LANES), lambda i, b: (b, i, 0)),
                  pl.BlockSpec((n_slc, ncp), lambda i, b: (0, 0))],
        out_specs=[pl.BlockSpec((1, tq, 4 * LANES), lambda i, b: (b, i, 0)),
                   pl.BlockSpec((1, NSA_GROUPS, tq, LANES), lambda i, b: (b, 0, i, 0))],
        out_shape=[jax.ShapeDtypeStruct((B, S, 4 * LANES), F32),
                   jax.ShapeDtypeStruct((B, NSA_GROUPS, S, LANES), BF16)],
        scratch_shapes=[pltpu.VMEM((NSA_HEADS, tq, LANES), F32), pltpu.VMEM((NSA_HEADS * tq, ncp), F32)],
        compiler_params=_params("arbitrary", "arbitrary"),
        name="cmp",
    )(tab, proj3, cmp_kv, cmp_kv, gates3, matt)


class _OnlineSoftmax:
    def __init__(self, score, value, m_ref, l_ref, acc_ref, sub_keys, init=True):
        self.score, self.value, self.sub_keys = score, value, sub_keys
        self.m_ref, self.l_ref, self.acc_ref = m_ref, l_ref, acc_ref
        if init:
            m_ref[...] = jnp.full(m_ref.shape, NEG_INF, F32)
            l_ref[...] = jnp.zeros(l_ref.shape, F32)
            acc_ref[...] = jnp.zeros(acc_ref.shape, F32)

    def step(self, k0, width, bias, first=False):
        m_ref, l_ref, acc_ref = self.m_ref, self.l_ref, self.acc_ref
        for j in range(0, width, self.sub_keys):
            w = min(self.sub_keys, width - j)
            s = self.score(k0 + j, w)
            cols = []
            for c in range(0, w, LANES):
                b = None if bias is None else bias(j + c)
                cols.append(s[:, c:c + LANES] if b is None else s[:, c:c + LANES] + b)
            row_max = jnp.max(functools.reduce(jnp.maximum, cols), axis=-1, keepdims=True)
            assign = first and j == 0
            m_old = jnp.full(m_ref.shape, NEG_INF, F32) if assign else m_ref[...]
            m_new = jnp.maximum(m_old, row_max)
            ps = [jnp.exp2(col - m_new) for col in cols]
            p_sum = functools.reduce(lambda a, b: a + b, ps)
            pv = _dot(jnp.concatenate([p.astype(BF16) for p in ps], axis=1), self.value(k0 + j, w))
            if assign:
                l_ref[...] = p_sum
                acc_ref[...] = pv
            else:
                alpha = jnp.exp2(m_old - m_new)
                l_ref[...] = alpha * l_ref[...] + p_sum
                acc_ref[...] = alpha * acc_ref[...] + pv
            m_ref[...] = m_new

    def result(self):
        return self.acc_ref[...] / jnp.sum(self.l_ref[...], axis=-1, keepdims=True)


def _causal_far_loop(fn, qt, tq, far_keys):
    far_tiles = far_keys // tq

    def far_body(i, carry):
        fn(i * far_keys, far_keys, None)
        return carry

    lax.fori_loop(0, lax.div(jnp.maximum(qt - 1, 0), far_tiles), far_body, 0)


def _causal_segments(fn, qt, tq, far_keys, near_bias):
    n_far = jnp.maximum(qt - 1, 0)
    far_tiles = far_keys // tq
    for rem in range(far_tiles):
        lead = rem * tq
        bias = (lambda c, lead=lead: None if c < lead else near_bias(c - lead))
        pl.when((qt >= 1) & (lax.rem(n_far, far_tiles) == rem))(
            functools.partial(fn, (qt - 1 - rem) * tq, lead + 2 * tq, bias, first=True))
    pl.when(qt == 0)(functools.partial(fn, 0, tq, lambda c: near_bias(tq + c), first=True))
    _causal_far_loop(fn, qt, tq, far_keys)


def _key_slice(k0, width, tq):
    return pl.ds(k0 if isinstance(k0, int) else pl.multiple_of(k0, tq), width)


def _rows(ref, k0, width, tq):
    return ref[0, _key_slice(k0, width, tq), :]


def _nsa_kernel(tab_ref, q_ref, ks_ref, vs_ref, kw_ref, vw_ref, sel_ref, et_ref, gate_ref, ocmp_ref,
                y_ref, tb_ref, m_ref, l_ref, acc_ref, wm_ref, wl_ref, wacc_ref):
    tq = q_ref.shape[1]
    n_win = WINDOW // tq
    qt = pl.program_id(1)

    @pl.when((pl.program_id(0) == 0) & (qt == 0))
    def _():
        ti = lax.broadcasted_iota(jnp.int32, (tq, tq), 0)
        ki = lax.broadcasted_iota(jnp.int32, (tq, tq), 1)
        for g in range(NSA_GROUPS):
            for r in range(NSA_REP):
                head = g * NSA_REP + r
                far_bias = tab_ref[REL_BUCKETS - 1, head]
                rows = slice(head * tq, (head + 1) * tq)
                tb_ref[rows, 0:tq] = jnp.where(ti < ki, 0.0, NEG_INF)
                for j in range(2, n_win):
                    tb_ref[rows, (n_win - j) * tq:(n_win - j + 1) * tq] = jnp.zeros((tq, tq), F32)
                table = _rel_bias(lax.broadcasted_iota(jnp.int32, (tq, LANES), 1), tab_ref, head, far_bias)
                tb_ref[rows, (n_win - 1) * tq:n_win * tq] = _bias_lookup(table, ti - ki + tq)
                tb_ref[rows, n_win * tq:(n_win + 1) * tq] = jnp.where(
                    ti >= ki, _bias_lookup(table, ti - ki), NEG_INF)

    lane = lax.broadcasted_iota(jnp.int32, (tq, LANES), 1)
    gates = jax.nn.sigmoid(gate_ref[0])

    q_parts, sel_parts = [], []
    for g in range(NSA_GROUPS):
        lane_g = (lane >= HEAD_DIM * g) & (lane < HEAD_DIM * (g + 1))
        for r in range(NSA_REP):
            qb = q_ref[0, :, r * LANES:(r + 1) * LANES]
            q_parts.append(jnp.where(lane_g, qb, jnp.zeros_like(qb)))
            sel_parts.append(sel_ref[0, g])
    q_all = jnp.concatenate(q_parts, axis=0)
    qs_all = jnp.concatenate([q_all, jnp.concatenate(sel_parts, axis=0)], axis=1)

    def bias_from(col0):
        return lambda c: tb_ref[:, col0 + c:col0 + c + LANES]

    def slc_score(k0, width):
        et = et_ref[_key_slice(k0, width, tq), :]
        return _dot_nt(qs_all, jnp.concatenate([_rows(ks_ref, k0, width, tq), et], axis=1))

    def win_score(k0, width):
        return _dot_nt(q_all, _rows(kw_ref, k0, width, tq))

    slc = _OnlineSoftmax(slc_score, lambda k0, width: _rows(vs_ref, k0, width, tq),
                         m_ref, l_ref, acc_ref, NSA_ONLINE_KEYS, init=False)
    win = _OnlineSoftmax(win_score, lambda k0, width: _rows(vw_ref, k0, width, tq),
                         wm_ref, wl_ref, wacc_ref, NSA_ONLINE_KEYS, init=False)

    def near_and_window(slc_tiles, win_tiles):
        slc_lead = (slc_tiles - 2) * tq if slc_tiles >= 2 else 0
        slc_bias = bias_from((n_win - 1) * tq) if slc_tiles >= 2 else bias_from(n_win * tq)
        slc.step((qt + 1 - slc_tiles) * tq, slc_tiles * tq,
                 lambda c: None if c < slc_lead else slc_bias(c - slc_lead), first=True)
        win.step((qt + 1 - win_tiles) * tq, win_tiles * tq, bias_from((n_win + 1 - win_tiles) * tq), first=True)

    far_tiles = NSA_FAR_KEYS // tq
    left_over = lax.rem(jnp.maximum(qt - 1, 0), far_tiles)
    for n in range(n_win):
        pl.when(qt == n)(functools.partial(near_and_window, n + 1, n + 1))
    for rem in range(far_tiles):
        pl.when((qt >= n_win) & (left_over == rem))(functools.partial(near_and_window, rem + 2, n_win + 1))
    _causal_far_loop(slc.step, qt, tq, NSA_FAR_KEYS)
    o_slc = slc.result()
    o_win = win.result()

    for r in range(NSA_REP):
        ys = []
        for g in range(NSA_GROUPS):
            head = g * NSA_REP + r
            rows = slice(head * tq, (head + 1) * tq)
            ys.append(gates[:, head * 3 + 1:head * 3 + 2] * o_slc[rows]
                      + gates[:, head * 3 + 2:head * 3 + 3] * o_win[rows])
        cols = slice(r * LANES, (r + 1) * LANES)
        y_ref[0, :, cols] = (ocmp_ref[0, :, cols] + jnp.where(lane < HEAD_DIM, ys[0], ys[1])).astype(y_ref.dtype)


def _nsa_attention(tab, proj3, sel, et, gates3, ocmp, col_blocks):
    B, S, _ = proj3.shape
    tq = NSA_TQ
    assert WINDOW % tq == 0 and WINDOW // tq >= 2 and S % tq == 0
    n_win = WINDOW // tq
    ks_c, vs_c, kw_c, vw_c = col_blocks
    rows = NSA_HEADS * tq

    def kv_spec(c):
        return pl.BlockSpec((1, S, LANES), lambda b, i: (b, 0, c))

    return pl.pallas_call(
        _nsa_kernel,
        grid=(B, S // tq),
        in_specs=[pl.BlockSpec(memory_space=pltpu.SMEM),
                  pl.BlockSpec((1, tq, 4 * LANES), lambda b, i: (b, i, 0)),
                  kv_spec(ks_c), kv_spec(vs_c), kv_spec(kw_c), kv_spec(vw_c),
                  pl.BlockSpec((1, NSA_GROUPS, tq, LANES), lambda b, i: (b, 0, i, 0)),
                  pl.BlockSpec((S, LANES), lambda b, i: (0, 0)),
                  pl.BlockSpec((1, tq, LANES), lambda b, i: (b, i, 0)),
                  pl.BlockSpec((1, tq, 4 * LANES), lambda b, i: (b, i, 0))],
        out_specs=pl.BlockSpec((1, tq, 4 * LANES), lambda b, i: (b, i, 0)),
        out_shape=jax.ShapeDtypeStruct((B, S, 4 * LANES), BF16),
        scratch_shapes=[pltpu.VMEM((rows, (n_win + 1) * tq), F32)] + [pltpu.VMEM((rows, LANES), F32)] * 6,
        compiler_params=_params("arbitrary", "arbitrary"),
        name="nsa",
    )(tab, proj3, proj3, proj3, proj3, proj3, sel, et, gates3, ocmp)


def _diff_kernel(tab_ref, q_ref, k_ref, v_ref, lam_ref, g_ref, y_ref, tb_ref, m_ref, l_ref, acc_ref,
                 *, lambda_init):
    tq = q_ref.shape[1]
    h = pl.program_id(0)
    qt = pl.program_id(2)
    lane = lax.broadcasted_iota(jnp.int32, (tq, LANES), 1)

    @pl.when((pl.program_id(1) == 0) & (qt == 0))
    def _():
        ti = lax.broadcasted_iota(jnp.int32, (tq, tq), 0)
        ki = lax.broadcasted_iota(jnp.int32, (tq, tq), 1)
        head = NSA_HEADS + h
        table = _rel_bias(lane, tab_ref, head, tab_ref[REL_BUCKETS - 1, head])
        tb_ref[:, 0:tq] = _bias_lookup(table, ti - ki + tq)
        tb_ref[:, tq:2 * tq] = jnp.where(ti >= ki, _bias_lookup(table, ti - ki), NEG_INF)

    qb = q_ref[0]
    zero = jnp.zeros_like(qb)
    q2 = jnp.concatenate([jnp.where(lane < HEAD_DIM, qb, zero), jnp.where(lane >= HEAD_DIM, qb, zero)], axis=0)

    def score(k0, width):
        return _dot_nt(q2, _rows(k_ref, k0, width, tq))

    def near_bias(c):
        tb = tb_ref[:, c:c + LANES]
        return jnp.concatenate([tb, tb], axis=0)

    attn = _OnlineSoftmax(score, lambda k0, width: _rows(v_ref, k0, width, tq), m_ref, l_ref, acc_ref,
                          DIFF_ONLINE_KEYS, init=False)
    _causal_segments(attn.step, qt, tq, DIFF_FAR_KEYS, near_bias)
    a = attn.result()
    lq1, lk1, lq2, lk2 = lam_ref[0:1, :], lam_ref[1:2, :], lam_ref[2:3, :], lam_ref[3:4, :]
    lam = (jnp.exp(jnp.sum(lq1 * lk1, axis=-1, keepdims=True))
           - jnp.exp(jnp.sum(lq2 * lk2, axis=-1, keepdims=True)) + lambda_init)
    o = a[:tq] - lam * a[tq:]
    o = o * lax.rsqrt(jnp.mean(o * o, axis=-1, keepdims=True) + LN_EPS) * g_ref[...]
    y_ref[0] = (o * (1.0 - lambda_init)).astype(y_ref.dtype)


def _diff_attention(tab, proj3, lam4, subln_g, col_blocks, lambda_init):
    B, S, _ = proj3.shape
    tq = min(DIFF_TQ, S)
    q_c, k_c, v_c = col_blocks
    return pl.pallas_call(
        functools.partial(_diff_kernel, lambda_init=lambda_init),
        grid=(DIFF_HEADS, B, S // tq),
        in_specs=[pl.BlockSpec(memory_space=pltpu.SMEM),
                  pl.BlockSpec((1, tq, LANES), lambda h, b, i: (b, i, q_c + h)),
                  pl.BlockSpec((1, S, LANES), lambda h, b, i: (b, 0, k_c + h)),
                  pl.BlockSpec((1, S, LANES), lambda h, b, i: (b, 0, v_c + h)),
                  pl.BlockSpec((SUBLANES, HEAD_DIM), lambda h, b, i: (0, 0)),
                  pl.BlockSpec((1, LANES), lambda h, b, i: (0, 0))],
        out_specs=pl.BlockSpec((1, tq, LANES), lambda h, b, i: (b, i, h)),
        out_shape=jax.ShapeDtypeStruct((B, S, DIFF_HEADS * LANES), BF16),
        scratch_shapes=[pltpu.VMEM((tq, 2 * tq), F32),
                        pltpu.VMEM((2 * tq, LANES), F32),
                        pltpu.VMEM((2 * tq, LANES), F32),
                        pltpu.VMEM((2 * tq, LANES), F32)],
        compiler_params=_params("arbitrary", "arbitrary", "arbitrary"),
        name="diff",
    )(tab, proj3, proj3, proj3, lam4, subln_g)


def _merge_kernel(x_ref, yn_ref, yd_ref, wgn_ref, wgd_ref, wbn_ref, wbd_ref, wo_ref, g_ref, b_ref, o_ref,
                  *, alpha):
    for r0 in range(0, x_ref.shape[0], ROW_TILE):
        rows = slice(r0, r0 + ROW_TILE)
        x = x_ref[rows, :]
        xb = x.astype(BF16)
        merged = (jax.nn.sigmoid(_dot(xb, wgn_ref[...])) * _dot(yn_ref[rows, :], wbn_ref[...])
                  + jax.nn.sigmoid(_dot(xb, wgd_ref[...])) * _dot(yd_ref[rows, :], wbd_ref[...]))
        z = alpha * x + _dot(merged.astype(BF16), wo_ref[...])
        o_ref[rows, :] = _layer_norm(z, g_ref[...], b_ref[...])


def _merge(x2d, y_nsa, y_diff, wgn, wgd, wbn, wbd, wo, ln_g, ln_b, alpha):
    T, D = x2d.shape
    const = lambda shape: pl.BlockSpec(shape, lambda i: (0, 0))
    return pl.pallas_call(
        functools.partial(_merge_kernel, alpha=alpha),
        grid=(T // WIDE_TILE,),
        in_specs=[pl.BlockSpec((WIDE_TILE, D), lambda i: (i, 0)),
                  pl.BlockSpec((WIDE_TILE, y_nsa.shape[1]), lambda i: (i, 0)),
                  pl.BlockSpec((WIDE_TILE, y_diff.shape[1]), lambda i: (i, 0)),
                  const(wgn.shape), const(wgd.shape), const(wbn.shape), const(wbd.shape), const(wo.shape),
                  const((1, D)), const((1, D))],
        out_specs=pl.BlockSpec((WIDE_TILE, D), lambda i: (i, 0)),
        out_shape=jax.ShapeDtypeStruct((T, D), F32),
        compiler_params=_params("arbitrary"),
        name="merge",
    )(x2d, y_nsa, y_diff, wgn, wgd, wbn, wbd, wo, ln_g, ln_b)


def _ffn_kernel(x_ref, halo_ref, p_ref, w_ref, cw_ref, cb_ref, wd_ref, g_ref, b_ref, wpg_ref, wpp_ref,
                o_ref, acc_ref, *, alpha, tiles_per_seq):
    d_ff = wd_ref.shape[0]
    keep = (pl.program_id(0) % tiles_per_seq != 0).astype(F32)
    for r0 in range(0, x_ref.shape[0], ROW_TILE):
        rows = slice(r0, r0 + ROW_TILE)
        x = x_ref[rows, :]
        xb = x.astype(BF16)
        hb = (halo_ref[...] if r0 == 0 else x_ref[r0 - HALO:r0, :]).astype(BF16)
        for c in range(0, d_ff, FF_CHUNK):
            width = min(FF_CHUNK, d_ff - c)
            cols = slice(c, c + width)
            wg = w_ref[:, cols]
            gm = _dot(xb, wg)
            gh = _dot(hb, wg)
            if r0 == 0:
                gh = gh * keep
            um = _dot(xb, w_ref[:, d_ff + c:d_ff + c + width])
            gext = jnp.concatenate([gh, gm], axis=0)
            g1 = pltpu.roll(gext, 1, 0)[HALO:]
            g2 = pltpu.roll(gext, 2, 0)[HALO:]
            conv = cb_ref[:, cols] + cw_ref[0:1, cols] * g2 + cw_ref[1:2, cols] * g1 + cw_ref[2:3, cols] * gm
            acc_ref[rows, cols] = (_gelu(conv) * um).astype(BF16)
        x2 = _layer_norm(alpha * x + _dot(acc_ref[rows, :], wd_ref[...]), g_ref[...], b_ref[...])
        gate = jax.nn.sigmoid(_dot(x2.astype(BF16), wpg_ref[...]))
        o_ref[rows, :] = x2 + gate * _dot(p_ref[rows, :].astype(BF16), wpp_ref[...])


def _ffn(x1, p2d, w_in, cw, cb, wd, ln_g, ln_b, wpg, wpp, alpha, seq):
    T, D = x1.shape
    tm = WIDE_TILE
    assert seq % tm == 0 and wd.shape[0] % LANES == 0
    hb = tm // HALO
    const = lambda shape: pl.BlockSpec(shape, lambda i: (0, 0))
    return pl.pallas_call(
        functools.partial(_ffn_kernel, alpha=alpha, tiles_per_seq=seq // tm),
        grid=(T // tm,),
        in_specs=[pl.BlockSpec((tm, D), lambda i: (i, 0)),
                  pl.BlockSpec((HALO, D), lambda i: (jnp.maximum(i * hb - 1, 0), 0)),
                  pl.BlockSpec((tm, p2d.shape[1]), lambda i: (i, 0)),
                  const(w_in.shape), const(cw.shape), const(cb.shape), const(wd.shape),
                  const((1, D)), const((1, D)), const(wpg.shape), const(wpp.shape)],
        out_specs=pl.BlockSpec((tm, D), lambda i: (i, 0)),
        out_shape=jax.ShapeDtypeStruct((T, D), F32),
        scratch_shapes=[pltpu.VMEM((tm, wd.shape[0]), BF16)],
        compiler_params=_params("arbitrary"),
        name="ffn",
    )(x1, x1, p2d, w_in, cw, cb, wd, ln_g, ln_b, wpg, wpp)


def _slc_from_cmp_t(ncp, n_slc):
    ratio = SLC_BLOCK // CMP_STRIDE
    span = CMP_BLOCK // CMP_STRIDE
    mat = np.zeros((n_slc, ncp), np.float32)
    for j in range(n_slc):
        for m in range(ratio):
            for n in range(span):
                i = ratio * j + m - n
                if 0 <= i < ncp - 1:
                    mat[j, i] += 1.0
    return mat


def _layer(x, p_l, w_in, pe_k, w1_k, w2_k, pe_v, w1_v, w2_v, lq1, lk1, lq2, lk2, subln_g, w_bn, w_bd, w_out,
           ln1_g, ln1_b, w_ffn_in, conv_w, conv_b, w_down, ln2_g, ln2_b, w_pp, w_pg, tab, lambda_init,
           alpha):
    B, S, D = x.shape
    T = B * S
    ncp = S // CMP_STRIDE
    n_slc = S // SLC_BLOCK
    q_w = NSA_HEADS * HEAD_DIM
    kv_w = NSA_GROUPS * HEAD_DIM
    dqk_w = DIFF_HEADS * 2 * HEAD_DIM
    sizes = (q_w,) + (kv_w,) * 6 + (NSA_HEADS * 3, dqk_w, dqk_w, dqk_w, D, D)
    offs = np.concatenate([[0], np.cumsum(sizes)])
    col = lambda i: w_in[:, int(offs[i]):int(offs[i + 1])]
    scale = HEAD_DIM ** -0.5 * LOG2E

    n_idx = np.arange(q_w)
    perm = (NSA_REP * ((n_idx % LANES) // HEAD_DIM) + n_idx // LANES) * HEAD_DIM + n_idx % HEAD_DIM
    w_main = jnp.concatenate([col(0)[:, perm] * scale] + [col(i) for i in range(3, 7)]
                             + [col(8) * scale, col(9), col(10)], axis=1).astype(BF16)
    w_cmp = jnp.concatenate([col(1), col(2)], axis=1).astype(BF16)
    w_gate = jnp.pad(col(7), ((0, 0), (0, LANES - NSA_HEADS * 3))).astype(BF16)
    x2d = x.reshape(T, D)
    proj, k_cmp, v_cmp, gates = _proj(x2d, w_main, w_cmp, w_gate)
    proj3 = proj.reshape(B, S, proj.shape[1])
    gates3 = gates.reshape(B, S, LANES)
    c_kslc, c_vslc, c_kwin, c_vwin = (q_w // LANES + i for i in range(4))
    c_dq = q_w // LANES + 4
    c_dk = c_dq + DIFF_HEADS
    c_dv = c_dk + DIFF_HEADS

    cw = CMP_STRIDE * kv_w
    rep = lambda a: jnp.broadcast_to(a.reshape(2, 2, CMP_STRIDE, 1, HEAD_DIM, -1),
                                     (2, 2, CMP_STRIDE, NSA_GROUPS, HEAD_DIM, a.shape[-1]))
    pe = rep(jnp.stack([pe_k, pe_v])[..., None]).reshape(2, 2, 1, cw)
    w1 = rep(jnp.stack([w1_k, w1_v])).reshape(2, 2, cw, CMP_HIDDEN).astype(BF16)
    w2 = jnp.stack([w2_k, w2_v])
    w2p = jnp.stack([jnp.pad(w2, ((0, 0), (0, 0), (g * HEAD_DIM, LANES - (g + 1) * HEAD_DIM)))
                     for g in range(NSA_GROUPS)], axis=1).astype(BF16)
    cmp_kv = _compress(k_cmp.reshape(B, ncp, cw), v_cmp.reshape(B, ncp, cw), pe, w1, w2p)

    matt = jnp.asarray(_slc_from_cmp_t(ncp, n_slc), BF16)
    ocmp, sel = _cmp_attention(tab, proj3, cmp_kv, gates3, matt, n_slc)
    et_np = np.zeros((S, LANES), np.float32)
    et_np[np.arange(S), np.arange(S) // SLC_BLOCK] = 2.0 ** MASK_EXP
    y_nsa = _nsa_attention(tab, proj3, sel, jnp.asarray(et_np, BF16), gates3, ocmp,
                           (c_kslc, c_vslc, c_kwin, c_vwin))

    lam4 = jnp.pad(jnp.stack([lq1, lk1, lq2, lk2]), ((0, SUBLANES - 4), (0, 0)))
    y_diff = _diff_attention(tab, proj3, lam4, subln_g.reshape(1, LANES), (c_dq, c_dk, c_dv), lambda_init)

    x1 = _merge(x2d, y_nsa.reshape(T, q_w), y_diff.reshape(T, dqk_w),
                col(11).astype(BF16), col(12).astype(BF16), w_bn[perm].astype(BF16), w_bd.astype(BF16),
                w_out.astype(BF16), ln1_g.reshape(1, D), ln1_b.reshape(1, D), alpha)

    cw = jnp.pad(conv_w, ((0, SUBLANES - CONV_WIDTH), (0, 0)))
    out = _ffn(x1, p_l.reshape(T, p_l.shape[-1]), w_ffn_in.astype(BF16), cw, conv_b.reshape(1, -1),
               w_down.astype(BF16), ln2_g.reshape(1, D), ln2_b.reshape(1, D),
               w_pg.astype(BF16), w_pp.astype(BF16), alpha, S)
    return out.reshape(B, S, D)


def kernel(x, p, w_in, nsa_cmp_pe_k, nsa_cmp_w1_k, nsa_cmp_w2_k, nsa_cmp_pe_v, nsa_cmp_w1_v, nsa_cmp_w2_v, diff_lambda_q1, diff_lambda_k1, diff_lambda_q2, diff_lambda_k2, diff_subln_g, w_branch_nsa, w_branch_diff, w_out, ln1_g, ln1_b, w_ffn_in, ffn_conv_w, ffn_conv_b, w_ffn_down, ln2_g, ln2_b, w_ple_proj, w_ple_gate, rel_bias_table):
    depth = w_in.shape[0]
    alpha = (2.0 * depth) ** 0.25
    for l in range(depth):
        lambda_init = 0.8 - 0.6 * math.exp(-0.3 * l)
        x = _layer(x, p[l], w_in[l], nsa_cmp_pe_k[l], nsa_cmp_w1_k[l], nsa_cmp_w2_k[l], nsa_cmp_pe_v[l],
                   nsa_cmp_w1_v[l], nsa_cmp_w2_v[l], diff_lambda_q1[l], diff_lambda_k1[l], diff_lambda_q2[l],
                   diff_lambda_k2[l], diff_subln_g[l], w_branch_nsa[l], w_branch_diff[l], w_out[l], ln1_g[l],
                   ln1_b[l], w_ffn_in[l], ffn_conv_w[l], ffn_conv_b[l], w_ffn_down[l], ln2_g[l], ln2_b[l],
                   w_ple_proj[l], w_ple_gate[l], rel_bias_table, lambda_init, alpha)
    return x
```
